```python
import jax
import jax.numpy as jnp
from jax import lax
import numpy as np

D_MODEL = 1024
BATCH = 8
SEQ = 4096
DEPTH = 1

GM_WIDTH = 1024
GM_GROUPS = 8
GM_GROUP_DIM = GM_WIDTH // GM_GROUPS
GM_CHUNK = 128

MLA_HEADS = 16
MLA_NOPE = 64
MLA_ROPE = 32
MLA_V = 64
MLA_Q_LORA = 384
MLA_KV_LORA = 256
ROPE_THETA = 10000.0
Q_BLOCK = 128

MEM_LEN = 256
MEM_HEADS = 4
MEM_HEAD_DIM = D_MODEL // MEM_HEADS

N_GROUPS = 8
EXPERTS_PER_GROUP = 8
N_EXPERTS = N_GROUPS * EXPERTS_PER_GROUP
TOP_K = 2
D_EXPERT = 256
MOE_BLOCK = 128

DEEPNORM_ALPHA = (2 * DEPTH) ** 0.25
DEEPNORM_BETA = (8 * DEPTH) ** -0.25
LN_EPS = 1e-5
RMS_EPS = 1e-6
MAX_POS_OFFSET = 2048

IN_SPLITS = (
    GM_WIDTH,
    2 * GM_WIDTH,
    2 * GM_WIDTH + MLA_Q_LORA,
    2 * GM_WIDTH + MLA_Q_LORA + MLA_KV_LORA,
    2 * GM_WIDTH + MLA_Q_LORA + MLA_KV_LORA + MLA_ROPE,
    2 * GM_WIDTH + MLA_Q_LORA + MLA_KV_LORA + MLA_ROPE + D_MODEL,
)
IN_COLS = IN_SPLITS[-1] + D_MODEL

kernel_name = "hybrid_gmlp_mla_memxattn_hmoe_deepnorm"


def layer_norm(x, g, b):
    xf = x.astype(jnp.float32)
    mu = jnp.mean(xf, axis=-1, keepdims=True)
    var = jnp.mean(jnp.square(xf - mu), axis=-1, keepdims=True)
    return ((xf - mu) * lax.rsqrt(var + LN_EPS) * g + b).astype(x.dtype)


def rms_norm(x, g):
    xf = x.astype(jnp.float32)
    return (xf * lax.rsqrt(jnp.mean(jnp.square(xf), axis=-1, keepdims=True) + RMS_EPS) * g).astype(x.dtype)


def rope(x, positions):
    half = MLA_ROPE // 2
    inv_freq = ROPE_THETA ** (-jnp.arange(half, dtype=jnp.float32) / half)
    ang = positions.astype(jnp.float32)[..., None] * inv_freq
    ang = ang.reshape(ang.shape[:2] + (1,) * (x.ndim - 3) + (half,))
    cos, sin = jnp.cos(ang), jnp.sin(ang)
    x1 = x[..., :half].astype(jnp.float32)
    x2 = x[..., half:].astype(jnp.float32)
    return jnp.concatenate([x1 * cos - x2 * sin, x2 * cos + x1 * sin], axis=-1).astype(x.dtype)


def gmlp_branch(u, v, ln_g, ln_b, w_s, b_s):
    B, S, _ = v.shape
    v = layer_norm(v, ln_g, ln_b)
    vc = v.reshape(B, S // GM_CHUNK, GM_CHUNK, GM_GROUPS, GM_GROUP_DIM)
    causal = jnp.tril(jnp.ones((GM_CHUNK, GM_CHUNK), dtype=bool))
    w = jnp.where(causal, w_s, 0).astype(v.dtype)
    mixed = jnp.einsum('gts,bcsgd->bctgd', w, vc) + b_s.T[None, None, :, :, None]
    return u * mixed.reshape(B, S, GM_WIDTH)


def mla_branch(c_q, c_kv, k_rope_raw, positions, q_norm_g, kv_norm_g, w_uq, w_uk, w_uv):
    B, S, _ = c_q.shape
    q = (rms_norm(c_q, q_norm_g) @ w_uq).reshape(B, S, MLA_HEADS, MLA_NOPE + MLA_ROPE)
    q_nope = q[..., :MLA_NOPE]
    q_rope = rope(q[..., MLA_NOPE:], positions)
    ckv = rms_norm(c_kv, kv_norm_g)
    k_nope = (ckv @ w_uk).reshape(B, S, MLA_HEADS, MLA_NOPE)
    v = (ckv @ w_uv).reshape(B, S, MLA_HEADS, MLA_V)
    k_rope = rope(k_rope_raw, positions)
    scale = (MLA_NOPE + MLA_ROPE) ** -0.5
    n_blocks = S // Q_BLOCK
    key_idx = jnp.arange(S)

    def to_blocks(t):
        return jnp.moveaxis(t.reshape((B, n_blocks, Q_BLOCK) + t.shape[2:]), 1, 0)

    def attend(args):
        qn, qr, blk = args
        s = (jnp.einsum('bqhd,bkhd->bhqk', qn, k_nope, preferred_element_type=jnp.float32)
             + jnp.einsum('bqhr,bkr->bhqk', qr, k_rope, preferred_element_type=jnp.float32)) * scale
        q_idx = blk * Q_BLOCK + jnp.arange(Q_BLOCK)
        s = jnp.where(key_idx[None, :] <= q_idx[:, None], s, -jnp.inf)
        p = jax.nn.softmax(s, axis=-1).astype(v.dtype)
        return jnp.einsum('bhqk,bkhd->bqhd', p, v)

    o = lax.map(attend, (to_blocks(q_nope), to_blocks(q_rope), jnp.arange(n_blocks)))
    return jnp.moveaxis(o, 0, 1).reshape(B, S, MLA_HEADS * MLA_V)


def memory_cross_attention(x, mem, w_mq, w_mk, w_mv, w_mo):
    B, S, _ = x.shape
    M = mem.shape[1]
    q = (x @ w_mq).reshape(B, S, MEM_HEADS, MEM_HEAD_DIM)
    k = (mem @ w_mk).reshape(B, M, MEM_HEADS, MEM_HEAD_DIM)
    v = (mem @ w_mv).reshape(B, M, MEM_HEADS, MEM_HEAD_DIM)
    s = jnp.einsum('bqhd,bmhd->bhqm', q, k, preferred_element_type=jnp.float32) * (MEM_HEAD_DIM ** -0.5)
    p = jax.nn.softmax(s, axis=-1).astype(v.dtype)
    o = jnp.einsum('bhqm,bmhd->bqhd', p, v).reshape(B, S, MEM_HEADS * MEM_HEAD_DIM)
    return o @ w_mo


def swiglu(xb, w_gate, w_up, w_down):
    return (jax.nn.silu(xb @ w_gate) * (xb @ w_up)) @ w_down


def hierarchical_moe(x, w_group_router, b_group_router, w_expert_router, b_expert_router,
                     w_exp_gate, w_exp_up, w_exp_down):
    B, S, D = x.shape
    T = B * S
    xf = x.reshape(T, D)
    g_logits = (xf @ w_group_router).astype(jnp.float32) + b_group_router
    g_prob = jax.nn.softmax(g_logits, axis=-1)
    g_sel = jnp.argmax(g_logits, axis=-1)
    g_w = jnp.take_along_axis(g_prob, g_sel[:, None], axis=-1)
    e_logits = ((xf @ w_expert_router).astype(jnp.float32) + b_expert_router).reshape(T, N_GROUPS, EXPERTS_PER_GROUP)
    e_logits = jnp.take_along_axis(e_logits, g_sel[:, None, None], axis=1)[:, 0]
    top_val, top_loc = lax.top_k(e_logits, TOP_K)
    top_w = jax.nn.softmax(top_val, axis=-1) * g_w
    top_e = g_sel[:, None] * EXPERTS_PER_GROUP + top_loc

    A = T * TOP_K
    flat_e = top_e.reshape(A)
    flat_tok = jnp.repeat(jnp.arange(T, dtype=jnp.int32), TOP_K)
    flat_w = top_w.reshape(A)
    order = jnp.argsort(flat_e)
    sorted_e, sorted_tok, sorted_w = flat_e[order], flat_tok[order], flat_w[order]
    counts = jnp.zeros((N_EXPERTS,), jnp.int32).at[flat_e].add(1)
    starts = jnp.cumsum(counts) - counts
    padded = (counts + MOE_BLOCK - 1) // MOE_BLOCK * MOE_BLOCK
    padded_ends = jnp.cumsum(padded)
    padded_starts = padded_ends - padded
    dest = padded_starts[sorted_e] + (jnp.arange(A, dtype=jnp.int32) - starts[sorted_e])
    P = A + N_EXPERTS * MOE_BLOCK
    n_blocks = P // MOE_BLOCK
    xd = jnp.zeros((P, D), x.dtype).at[dest].set(xf[sorted_tok])
    block_start = jnp.arange(n_blocks, dtype=jnp.int32) * MOE_BLOCK
    block_e = jnp.minimum(jnp.sum(padded_ends[None, :] <= block_start[:, None], axis=1), N_EXPERTS - 1)

    def run_block(args):
        xb, e = args
        return swiglu(xb, w_exp_gate[e], w_exp_up[e], w_exp_down[e])

    yd = lax.map(run_block, (xd.reshape(n_blocks, MOE_BLOCK, D), block_e)).reshape(P, D)
    contrib = yd[dest] * sorted_w[:, None].astype(yd.dtype)
    y = jnp.zeros((T, D), yd.dtype).at[sorted_tok].add(contrib)
    return y.reshape(B, S, D)


def hybrid_layer(x, mem, positions, w_in, b_in, gm_ln_g, gm_ln_b, gm_w_s, gm_b_s, w_gm_out,
                 mla_q_norm_g, mla_kv_norm_g, w_uq, w_uk, w_uv, w_mla_out, w_o, ln1_g, ln1_b,
                 w_mq, w_mk, w_mv, w_mo, ln2_g, ln2_b,
                 w_group_router, b_group_router, w_expert_router, b_expert_router,
                 w_exp_gate, w_exp_up, w_exp_down, ln3_g, ln3_b):
    proj = x @ w_in + b_in
    u, v, c_q, c_kv, k_rope_raw, gate_gm, gate_mla = jnp.split(proj, IN_SPLITS, axis=-1)
    y_gm = gmlp_branch(jax.nn.gelu(u, approximate=False), jax.nn.gelu(v, approximate=False),
                       gm_ln_g, gm_ln_b, gm_w_s, gm_b_s) @ w_gm_out
    y_mla = mla_branch(c_q, c_kv, k_rope_raw, positions, mla_q_norm_g, mla_kv_norm_g,
                       w_uq, w_uk, w_uv) @ w_mla_out
    merged = jax.nn.sigmoid(gate_gm) * y_gm + jax.nn.sigmoid(gate_mla) * y_mla
    x = layer_norm(DEEPNORM_ALPHA * x + merged @ w_o, ln1_g, ln1_b)
    x = layer_norm(DEEPNORM_ALPHA * x + memory_cross_attention(x, mem, w_mq, w_mk, w_mv, w_mo), ln2_g, ln2_b)
    y_moe = hierarchical_moe(x, w_group_router, b_group_router, w_expert_router, b_expert_router,
                             w_exp_gate, w_exp_up, w_exp_down)
    return layer_norm(DEEPNORM_ALPHA * x + y_moe, ln3_g, ln3_b)


def setup_inputs(seed: int = 0) -> dict:
    key = jax.random.key(seed)
    keys = jax.random.split(key, 40)
    counter = [0]

    def next_key():
        k = keys[counter[0]]
        counter[0] += 1
        return k

    def nrm(shape, scale):
        return jax.random.normal(next_key(), shape, jnp.float32) * scale

    def gain(shape):
        return 1.0 + nrm(shape, 0.02)

    L = DEPTH
    beta = DEEPNORM_BETA
    x = nrm((BATCH, SEQ, D_MODEL), 1.0)
    mem = nrm((BATCH, MEM_LEN, D_MODEL), 1.0)
    positions = (jnp.arange(SEQ, dtype=jnp.int32)[None, :]
                 + jax.random.randint(next_key(), (BATCH, 1), 0, MAX_POS_OFFSET, dtype=jnp.int32))
    return {
        "x": x,
        "mem": mem,
        "positions": positions,
        "w_in": nrm((L, D_MODEL, IN_COLS), D_MODEL ** -0.5),
        "b_in": nrm((L, IN_COLS), 0.02),
        "gm_ln_g": gain((L, GM_WIDTH)),
        "gm_ln_b": nrm((L, GM_WIDTH), 0.02),
        "gm_w_s": jnp.tril(nrm((L, GM_GROUPS, GM_CHUNK, GM_CHUNK), GM_CHUNK ** -0.5)),
        "gm_b_s": gain((L, GM_GROUPS, GM_CHUNK)),
        "w_gm_out": nrm((L, GM_WIDTH, D_MODEL), GM_WIDTH ** -0.5 * beta),
        "mla_q_norm_g": gain((L, MLA_Q_LORA)),
        "mla_kv_norm_g": gain((L, MLA_KV_LORA)),
        "w_uq": nrm((L, MLA_Q_LORA, MLA_HEADS * (MLA_NOPE + MLA_ROPE)), MLA_Q_LORA ** -0.5),
        "w_uk": nrm((L, MLA_KV_LORA, MLA_HEADS * MLA_NOPE), MLA_KV_LORA ** -0.5),
        "w_uv": nrm((L, MLA_KV_LORA, MLA_HEADS * MLA_V), MLA_KV_LORA ** -0.5 * beta),
        "w_mla_out": nrm((L, MLA_HEADS * MLA_V, D_MODEL), (MLA_HEADS * MLA_V) ** -0.5 * beta),
        "w_o": nrm((L, D_MODEL, D_MODEL), D_MODEL ** -0.5 * beta),
        "ln1_g": gain((L, D_MODEL)),
        "ln1_b": nrm((L, D_MODEL), 0.02),
        "w_mq": nrm((L, D_MODEL, MEM_HEADS * MEM_HEAD_DIM), D_MODEL ** -0.5),
        "w_mk": nrm((L, D_MODEL, MEM_HEADS * MEM_HEAD_DIM), D_MODEL ** -0.5),
        "w_mv": nrm((L, D_MODEL, MEM_HEADS * MEM_HEAD_DIM), D_MODEL ** -0.5 * beta),
        "w_mo": nrm((L, MEM_HEADS * MEM_HEAD_DIM, D_MODEL), D_MODEL ** -0.5 * beta),
        "ln2_g": gain((L, D_MODEL)),
        "ln2_b": nrm((L, D_MODEL), 0.02),
        "w_group_router": nrm((L, D_MODEL, N_GROUPS), D_MODEL ** -0.5),
        "b_group_router": nrm((L, N_GROUPS), 0.01),
        "w_expert_router": nrm((L, D_MODEL, N_EXPERTS), D_MODEL ** -0.5),
        "b_expert_router": nrm((L, N_EXPERTS), 0.01),
        "w_exp_gate": nrm((L, N_EXPERTS, D_MODEL, D_EXPERT), D_MODEL ** -0.5 * beta),
        "w_exp_up": nrm((L, N_EXPERTS, D_MODEL, D_EXPERT), D_MODEL ** -0.5 * beta),
        "w_exp_down": nrm((L, N_EXPERTS, D_EXPERT, D_MODEL), D_EXPERT ** -0.5 * beta),
        "ln3_g": gain((L, D_MODEL)),
        "ln3_b": nrm((L, D_MODEL), 0.02),
    }


def reference(x, mem, positions, w_in, b_in, gm_ln_g, gm_ln_b, gm_w_s, gm_b_s, w_gm_out,
              mla_q_norm_g, mla_kv_norm_g, w_uq, w_uk, w_uv, w_mla_out, w_o, ln1_g, ln1_b,
              w_mq, w_mk, w_mv, w_mo, ln2_g, ln2_b,
              w_group_router, b_group_router, w_expert_router, b_expert_router,
              w_exp_gate, w_exp_up, w_exp_down, ln3_g, ln3_b):
    h = x
    for l in range(DEPTH):
        h = hybrid_layer(h, mem, positions, w_in[l], b_in[l], gm_ln_g[l], gm_ln_b[l], gm_w_s[l], gm_b_s[l],
                         w_gm_out[l], mla_q_norm_g[l], mla_kv_norm_g[l], w_uq[l], w_uk[l], w_uv[l],
                         w_mla_out[l], w_o[l], ln1_g[l], ln1_b[l],
                         w_mq[l], w_mk[l], w_mv[l], w_mo[l], ln2_g[l], ln2_b[l],
                         w_group_router[l], b_group_router[l], w_expert_router[l], b_expert_router[l],
                         w_exp_gate[l], w_exp_up[l], w_exp_down[l], ln3_g[l], ln3_b[l])
    return h
```

```python
import functools
import math

import jax
import jax.numpy as jnp
from jax import lax
from jax.experimental import pallas as pl
from jax.experimental.pallas import tpu as pltpu

F32 = jnp.float32
BF16 = jnp.bfloat16

GM_WIDTH = 1024
GM_GROUPS = 8
GM_CHUNK = 128
MLA_HEADS = 16
MLA_NOPE = 64
MLA_ROPE = 32
MLA_V = 64
MLA_Q_LORA = 384
MLA_KV_LORA = 256
ROPE_THETA = 10000.0
MEM_HEADS = 4
N_GROUPS = 8
EXPERTS_PER_GROUP = 8
N_EXPERTS = N_GROUPS * EXPERTS_PER_GROUP
D_EXPERT = 256
LN_EPS = 1e-5
RMS_EPS = 1e-6

LANES = 128
HEAD_PAD = 128
ROPE_HALF = MLA_ROPE // 2

TM_IN = 256
TQ = 256
TK = 256
TM_POST = 256
MOE_ROWS = 256
TM_OUT = 256
VMEM_LIMIT = 56 * 1024 * 1024


def _const_spec(shape):
    nd = len(shape)
    return pl.BlockSpec(shape, lambda *_: (0,) * nd, pipeline_mode=pl.Buffered(1))


def _layer_norm(x, g, b):
    mu = jnp.mean(x, axis=-1, keepdims=True)
    xc = x - mu
    var = jnp.mean(xc * xc, axis=-1, keepdims=True)
    return xc * lax.rsqrt(var + LN_EPS) * g + b


def _rms_norm(x, g):
    return x * lax.rsqrt(jnp.mean(x * x, axis=-1, keepdims=True) + RMS_EPS) * g


def _gelu(x):
    return x * (lax.erf(x * (1.0 / math.sqrt(2.0))) + 1.0) * 0.5


def _dot(a, b):
    return jnp.dot(a, b, preferred_element_type=F32)


def _dot_nt(a, b):
    return lax.dot_general(a, b, (((1,), (1,)), ((), ())), preferred_element_type=F32)


def _rope_kernel(inv_ref, pos_ref, cos_ref, sin_ref):
    p = pos_ref[...].astype(F32)
    for j in range(ROPE_HALF):
        ang = p * inv_ref[j]
        cos_ref[j] = jnp.cos(ang)
        sin_ref[j] = jnp.sin(ang)


def _rope_tables(positions, inv_freq):
    t = positions.size
    rows = t // LANES
    pos2 = positions.reshape(rows, LANES)
    cos, sin = pl.pallas_call(
        _rope_kernel,
        out_shape=(jax.ShapeDtypeStruct((ROPE_HALF, rows, LANES), F32),) * 2,
        in_specs=[pl.BlockSpec(memory_space=pltpu.SMEM),
                  pl.BlockSpec((rows, LANES), lambda: (0, 0))],
        out_specs=(pl.BlockSpec((ROPE_HALF, rows, LANES), lambda: (0, 0, 0)),) * 2,
        name="rope_tables",
    )(inv_freq, pos2)
    cos = jnp.tile(cos.reshape(ROPE_HALF, t).T, (1, LANES // ROPE_HALF))
    sin = jnp.tile(sin.reshape(ROPE_HALF, t).T, (1, LANES // ROPE_HALF))
    return cos, sin


def _mixer_in_kernel(x_ref, cos_ref, sin_ref, w_uv_ref, b_uv_ref, w_gate_ref, b_gate_ref,
                     w_lat_ref, b_lat_ref, ln_g_ref, ln_b_ref, ws_ref, bs_ref, w_gm_out_ref,
                     qn_g_ref, kvn_g_ref, w_uq_ref, w_uk_ref, w_uvv_ref,
                     gm_ref, sg_ref, q_ref, k_ref, v_ref, mixed_ref, *, tm, scale):
    xb = x_ref[...].astype(BF16)

    uv = _gelu(_dot(xb, w_uv_ref[...]) + b_uv_ref[...])
    u = uv[:, :GM_WIDTH]
    vn = _layer_norm(uv[:, GM_WIDTH:], ln_g_ref[...], ln_b_ref[...]).astype(BF16)
    r = lax.broadcasted_iota(jnp.int32, (GM_CHUNK, GM_CHUNK), 0)
    c = lax.broadcasted_iota(jnp.int32, (GM_CHUNK, GM_CHUNK), 1)
    causal = c <= r
    for g in range(GM_GROUPS):
        w = jnp.where(causal, ws_ref[g], jnp.zeros_like(ws_ref[g]))
        for ch in range(tm // GM_CHUNK):
            rows = slice(ch * GM_CHUNK, (ch + 1) * GM_CHUNK)
            cols = slice(g * GM_CHUNK, (g + 1) * GM_CHUNK)
            mixed_ref[rows, cols] = _dot(w, vn[rows, cols]) + bs_ref[:, cols]
    gated = (u * mixed_ref[...]).astype(BF16)
    y_gm = _dot(gated, w_gm_out_ref[...])
    gates = _dot(xb, w_gate_ref[...]) + b_gate_ref[...]
    gm_ref[...] = (jax.nn.sigmoid(gates[:, :GM_WIDTH]) * y_gm).astype(BF16)
    sg_ref[...] = jax.nn.sigmoid(gates[:, GM_WIDTH:]).astype(BF16)

    lat = _dot(xb, w_lat_ref[...]) + b_lat_ref[...]
    cq = _rms_norm(lat[:, :MLA_Q_LORA], qn_g_ref[...]).astype(BF16)
    ckv = _rms_norm(lat[:, MLA_Q_LORA:MLA_Q_LORA + MLA_KV_LORA], kvn_g_ref[...]).astype(BF16)
    kr = lat[:, MLA_Q_LORA + MLA_KV_LORA:]

    lane = lax.broadcasted_iota(jnp.int32, (tm, HEAD_PAD), 1)
    cosv = cos_ref[...]
    sinv = sin_ref[...]
    in_rope = (lane >= MLA_NOPE) & (lane < MLA_NOPE + MLA_ROPE)
    cfac = jnp.where(lane < MLA_NOPE, 1.0, jnp.where(in_rope, cosv, 0.0))
    sfac = jnp.where(in_rope, jnp.where(lane < MLA_NOPE + ROPE_HALF, -sinv, sinv), 0.0)

    def rot(t):
        return t * cfac + pltpu.roll(t, HEAD_PAD - ROPE_HALF, 1) * sfac

    kr_rot = rot(kr)
    qa = _dot(cq, w_uq_ref[...])
    kn = _dot(ckv, w_uk_ref[...])
    for h in range(MLA_HEADS):
        cols = slice(h * HEAD_PAD, (h + 1) * HEAD_PAD)
        q_ref[:, cols] = (rot(qa[:, cols]) * scale).astype(BF16)
        k_ref[:, cols] = (kn[:, cols] + kr_rot).astype(BF16)
    v_ref[...] = _dot(ckv, w_uvv_ref[...]).astype(BF16)


def _mixer_in(x2d, cos, sin, p):
    t, d = x2d.shape
    tm = TM_IN
    scale = (MLA_NOPE + MLA_ROPE) ** -0.5
    row = lambda n: pl.BlockSpec((tm, n), lambda i: (i, 0))
    weights = [p["w_uv"], p["b_uv"], p["w_gate"], p["b_gate"], p["w_lat"], p["b_lat"],
               p["gm_ln_g"], p["gm_ln_b"], p["ws"], p["bs_full"], p["w_gm_out"],
               p["qn_g"], p["kvn_g"], p["w_uq"], p["w_uk"], p["w_uvv"]]
    qk_w = MLA_HEADS * HEAD_PAD
    v_w = MLA_HEADS * MLA_V
    return pl.pallas_call(
        functools.partial(_mixer_in_kernel, tm=tm, scale=scale),
        grid=(t // tm,),
        out_shape=(jax.ShapeDtypeStruct((t, d), BF16), jax.ShapeDtypeStruct((t, d), BF16),
                   jax.ShapeDtypeStruct((t, qk_w), BF16), jax.ShapeDtypeStruct((t, qk_w), BF16),
                   jax.ShapeDtypeStruct((t, v_w), BF16)),
        in_specs=[row(d), row(LANES), row(LANES)] + [_const_spec(w.shape) for w in weights],
        out_specs=(row(d), row(d), row(qk_w), row(qk_w), row(v_w)),
        scratch_shapes=[pltpu.VMEM((tm, GM_WIDTH), F32)],
        compiler_params=pltpu.CompilerParams(dimension_semantics=("arbitrary",),
                                             vmem_limit_bytes=VMEM_LIMIT),
        name="mixer_in",
    )(x2d, cos, sin, *weights)


def _attn_kernel(q_ref, k_ref, v_ref, o_ref, acc_ref, *, tq, tk):
    i = pl.program_id(2)
    q = q_ref[...]
    first = lax.broadcasted_iota(jnp.int32, (tq, 2 * MLA_V), 1) < MLA_V
    acc_ref[...] = jnp.zeros_like(acc_ref)

    def step(j, carry, masked):
        start = pl.multiple_of(j * tk, tk)
        kb = k_ref[pl.ds(start, tk), :]
        vb = v_ref[pl.ds(start, tk), :]
        new, alphas, pvs = [], [], []
        for a in range(2):
            m, l = carry[2 * a], carry[2 * a + 1]
            cols = slice(a * HEAD_PAD, (a + 1) * HEAD_PAD)
            s = _dot_nt(q[:, cols], kb[:, cols])
            if masked:
                row = i * tq + lax.broadcasted_iota(jnp.int32, (tq, tk), 0)
                col = j * tk + lax.broadcasted_iota(jnp.int32, (tq, tk), 1)
                s = jnp.where(col <= row, s, -jnp.inf)
            m_new = jnp.maximum(m, jnp.max(s, axis=1, keepdims=True))
            alpha = jnp.exp(m - m_new)
            pr = jnp.exp(s - m_new)
            new += [m_new, alpha * l + jnp.sum(pr, axis=1, keepdims=True)]
            alphas.append(alpha)
            pvs.append(_dot(pr.astype(BF16), vb))
        acc_ref[...] = (acc_ref[...] * jnp.where(first, alphas[0], alphas[1])
                        + jnp.where(first, pvs[0], pvs[1]))
        return tuple(new)

    neg = jnp.full((tq, 1), -jnp.inf, F32)
    zero = jnp.zeros((tq, 1), F32)
    carry = lax.fori_loop(0, i, lambda j, c: step(j, c, False), (neg, zero, neg, zero))
    carry = step(i, carry, True)
    o_ref[...] = (acc_ref[...] / jnp.where(first, carry[1], carry[3])).astype(BF16)


def _mla_attn(q, k, v, batch, seq):
    assert TQ == TK
    nq = seq // TQ
    return pl.pallas_call(
        functools.partial(_attn_kernel, tq=TQ, tk=TK),
        grid=(batch, MLA_HEADS // 2, nq),
        out_shape=jax.ShapeDtypeStruct((batch * seq, MLA_HEADS * MLA_V), BF16),
        in_specs=[pl.BlockSpec((TQ, 2 * HEAD_PAD), lambda b, h, i: (b * nq + i, h)),
                  pl.BlockSpec((seq, 2 * HEAD_PAD), lambda b, h, i: (b, h)),
                  pl.BlockSpec((seq, 2 * MLA_V), lambda b, h, i: (b, h))],
        out_specs=pl.BlockSpec((TQ, 2 * MLA_V), lambda b, h, i: (b * nq + i, h)),
        scratch_shapes=[pltpu.VMEM((TQ, 2 * MLA_V), F32)],
        compiler_params=pltpu.CompilerParams(
            dimension_semantics=("arbitrary", "arbitrary", "arbitrary"), vmem_limit_bytes=VMEM_LIMIT),
        name="mla_attn",
    )(q, k, v)


def _mem_kv_kernel(mem_ref, wk_ref, wv_ref, k_ref, v_ref):
    mb = mem_ref[...].astype(BF16)
    k_ref[...] = _dot(mb, wk_ref[...]).astype(BF16)
    v_ref[...] = _dot(mb, wv_ref[...]).astype(BF16)


def _mem_kv(mem2d, w_mk, w_mv, mem_len):
    rows, d = mem2d.shape
    blk = pl.BlockSpec((mem_len, d), lambda i: (i, 0))
    return pl.pallas_call(
        _mem_kv_kernel,
        grid=(rows // mem_len,),
        out_shape=(jax.ShapeDtypeStruct((rows, d), BF16),) * 2,
        in_specs=[blk, _const_spec(w_mk.shape), _const_spec(w_mv.shape)],
        out_specs=(blk, blk),
        compiler_params=pltpu.CompilerParams(dimension_semantics=("arbitrary",)),
        name="mem_kv",
    )(mem2d, w_mk, w_mv)


def _post_mixer_kernel(x_ref, o_ref, gm_ref, sg_ref, mk_ref, mv_ref,
                       w_mla_out_ref, w_o_ref, ln1_g_ref, ln1_b_ref,
                       w_mq_ref, w_mo_ref, ln2_g_ref, ln2_b_ref,
                       wr_hi_ref, wr_lo_ref, br_ref,
                       x2_ref, route_ref, counts_ref, run_ref, *, tm, alpha, mem_scale):
    i = pl.program_id(0)

    @pl.when(i == 0)
    def _():
        run_ref[...] = jnp.zeros_like(run_ref)

    y_mla = _dot(o_ref[...], w_mla_out_ref[...])
    merged = gm_ref[...].astype(F32) + sg_ref[...].astype(F32) * y_mla
    x1 = _layer_norm(alpha * x_ref[...] + _dot(merged.astype(BF16), w_o_ref[...]),
                     ln1_g_ref[...], ln1_b_ref[...])

    qm = (_dot(x1.astype(BF16), w_mq_ref[...]) * mem_scale).astype(BF16)
    hd = qm.shape[1] // MEM_HEADS
    heads = []
    for h in range(MEM_HEADS):
        cols = slice(h * hd, (h + 1) * hd)
        s = _dot_nt(qm[:, cols], mk_ref[:, cols])
        pr = jnp.exp(s - jnp.max(s, axis=1, keepdims=True))
        heads.append(_dot(pr.astype(BF16), mv_ref[:, cols]) / jnp.sum(pr, axis=1, keepdims=True))
    om = jnp.concatenate(heads, axis=1).astype(BF16)
    x2 = _layer_norm(alpha * x1 + _dot(om, w_mo_ref[...]), ln2_g_ref[...], ln2_b_ref[...])
    x2_ref[...] = x2

    x_hi = x2.astype(BF16)
    x_lo = (x2 - x_hi.astype(F32)).astype(BF16)
    logits = (_dot(x_hi, wr_hi_ref[...]) + _dot(x_lo, wr_hi_ref[...])
              + _dot(x_hi, wr_lo_ref[...]) + br_ref[...])
    lane = lax.broadcasted_iota(jnp.int32, (tm, LANES), 1).astype(F32)
    big = jnp.float32(1e9)

    def first_argmax(vals, vmax):
        return jnp.min(jnp.where(vals == vmax, lane, big), axis=1, keepdims=True)

    g_mask = (lane >= N_EXPERTS) & (lane < N_EXPERTS + N_GROUPS)
    lg = jnp.where(g_mask, logits, -jnp.inf)
    g_max = jnp.max(lg, axis=1, keepdims=True)
    g_sel = first_argmax(lg, g_max) - N_EXPERTS
    g_w = 1.0 / jnp.sum(jnp.where(g_mask, jnp.exp(logits - g_max), 0.0), axis=1, keepdims=True)
    in_group = jnp.floor(lane * (1.0 / EXPERTS_PER_GROUP)) == g_sel
    le = jnp.where(in_group, logits, -jnp.inf)
    v1 = jnp.max(le, axis=1, keepdims=True)
    e1 = first_argmax(le, v1)
    le2 = jnp.where(lane == e1, -jnp.inf, le)
    v2 = jnp.max(le2, axis=1, keepdims=True)
    e2 = first_argmax(le2, v2)
    t2 = jnp.exp(v2 - v1)
    w1 = (1.0 / (1.0 + t2)) * g_w
    w2 = (t2 / (1.0 + t2)) * g_w

    hit1 = lane == e1
    hit2 = lane == e2
    onehot = jnp.where(hit1 | hit2, 1.0, 0.0)
    r_i = lax.broadcasted_iota(jnp.int32, (tm, tm), 0)
    c_i = lax.broadcasted_iota(jnp.int32, (tm, tm), 1)
    tri = jnp.where(c_i < r_i, 1.0, 0.0).astype(BF16)
    before = _dot(tri, onehot.astype(BF16)) + run_ref[...]
    r1 = jnp.sum(jnp.where(hit1, before, 0.0), axis=1, keepdims=True)
    r2 = jnp.sum(jnp.where(hit2, before, 0.0), axis=1, keepdims=True)
    run_ref[...] = run_ref[...] + jnp.sum(onehot, axis=0, keepdims=True)
    counts_ref[...] = jnp.broadcast_to(run_ref[...], counts_ref.shape)

    packed = jnp.zeros((tm, LANES), F32)
    for idx, val in enumerate((e1, e2, w1, w2, r1, r2)):
        packed = jnp.where(lane == idx, val, packed)
    route_ref[...] = packed


def _post_mixer(x2d, o, gm, sg, mk, mv, p, seq, mem_len):
    t, d = x2d.shape
    tm = TM_POST
    per_batch = seq // tm
    row = lambda n: pl.BlockSpec((tm, n), lambda i: (i, 0))
    memblk = pl.BlockSpec((mem_len, d), lambda i: (i // per_batch, 0))
    weights = [p["w_mla_out"], p["w_o"], p["ln1_g"], p["ln1_b"], p["w_mq"], p["w_mo"],
               p["ln2_g"], p["ln2_b"], p["wr_hi"], p["wr_lo"], p["br"]]
    return pl.pallas_call(
        functools.partial(_post_mixer_kernel, tm=tm, alpha=p["alpha"],
                          mem_scale=(d // MEM_HEADS) ** -0.5),
        grid=(t // tm,),
        out_shape=(jax.ShapeDtypeStruct((t, d), F32), jax.ShapeDtypeStruct((t, LANES), F32),
                   jax.ShapeDtypeStruct((8, LANES), F32)),
        in_specs=[row(d), row(d), row(d), row(d), memblk, memblk]
                 + [_const_spec(w.shape) for w in weights],
        out_specs=(row(d), row(LANES), pl.BlockSpec((8, LANES), lambda i: (0, 0))),
        scratch_shapes=[pltpu.VMEM((1, LANES), F32)],
        compiler_params=pltpu.CompilerParams(dimension_semantics=("arbitrary",),
                                             vmem_limit_bytes=VMEM_LIMIT),
        name="post_mixer",
    )(x2d, o, gm, sg, mk, mv, *weights)


def _gather_rows(idx_ref, src_hbm, dst, sem, n):
    def body(r, carry):
        pltpu.make_async_copy(src_hbm.at[pl.ds(idx_ref[0, 0, r], 1)], dst.at[pl.ds(r, 1)], sem).start()
        return carry
    lax.fori_loop(0, n, body, 0, unroll=8)


def _wait_rows(src_hbm, dst, sem, n):
    pltpu.make_async_copy(src_hbm.at[pl.ds(0, n)], dst, sem).wait()


def _experts_kernel(be_ref, nvalid_ref, src_cur_ref, src_nxt_ref, wg_ref, wu_ref, wd_ref, x_hbm,
                    yd_ref, xbuf, sems, wgu_bf, wd_bf, *, rows, nblocks):
    i = pl.program_id(0)
    slot = i % 2
    nvalid = nvalid_ref[0]

    @pl.when(i == 0)
    def _():
        _gather_rows(src_cur_ref, x_hbm, xbuf.at[0], sems.at[0], rows)

    @pl.when(i + 1 < nvalid)
    def _():
        _gather_rows(src_nxt_ref, x_hbm, xbuf.at[1 - slot], sems.at[1 - slot], rows)

    changed = jnp.logical_or(i == 0, be_ref[i] != be_ref[jnp.maximum(i - 1, 0)])

    @pl.when(changed)
    def _():
        wgu_bf[:, :D_EXPERT] = wg_ref[...].astype(BF16)
        wgu_bf[:, D_EXPERT:] = wu_ref[...].astype(BF16)
        wd_bf[...] = wd_ref[...].astype(BF16)

    @pl.when(jnp.logical_or(i < nvalid, i == 0))
    def _():
        _wait_rows(x_hbm, xbuf.at[slot], sems.at[slot], rows)

    @pl.when(i < nvalid)
    def _():
        xb = xbuf[slot].astype(BF16)
        gu = _dot(xb, wgu_bf[...])
        gate = gu[:, :D_EXPERT]
        hidden = (gate * jax.nn.sigmoid(gate) * gu[:, D_EXPERT:]).astype(BF16)
        yd_ref[...] = _dot(hidden, wd_bf[...])

    @pl.when(i >= nvalid)
    def _():
        yd_ref[...] = jnp.zeros_like(yd_ref)


def _moe_experts(x2, block_e, nvalid, row_src, w_gate, w_up, w_down):
    t, d = x2.shape
    nblocks = block_e.shape[0]
    rows = MOE_ROWS
    src3 = row_src.reshape(nblocks, 1, rows)
    smem_blk = lambda f: pl.BlockSpec((1, 1, rows), f, memory_space=pltpu.SMEM)
    grid_spec = pltpu.PrefetchScalarGridSpec(
        num_scalar_prefetch=2,
        grid=(nblocks,),
        in_specs=[smem_blk(lambda i, be, nv: (i, 0, 0)),
                  smem_blk(lambda i, be, nv: (jnp.minimum(i + 1, nblocks - 1), 0, 0)),
                  pl.BlockSpec((None, d, D_EXPERT), lambda i, be, nv: (be[i], 0, 0)),
                  pl.BlockSpec((None, d, D_EXPERT), lambda i, be, nv: (be[i], 0, 0)),
                  pl.BlockSpec((None, D_EXPERT, d), lambda i, be, nv: (be[i], 0, 0)),
                  pl.BlockSpec(memory_space=pl.ANY)],
        out_specs=pl.BlockSpec((rows, d), lambda i, be, nv: (i, 0)),
        scratch_shapes=[pltpu.VMEM((2, rows, d), F32), pltpu.SemaphoreType.DMA((2,)),
                        pltpu.VMEM((d, 2 * D_EXPERT), BF16), pltpu.VMEM((D_EXPERT, d), BF16)],
    )
    return pl.pallas_call(
        functools.partial(_experts_kernel, rows=rows, nblocks=nblocks),
        grid_spec=grid_spec,
        out_shape=jax.ShapeDtypeStruct((nblocks * rows, d), F32),
        compiler_params=pltpu.CompilerParams(dimension_semantics=("arbitrary",),
                                             vmem_limit_bytes=VMEM_LIMIT),
        name="moe_experts",
    )(block_e, nvalid, src3, src3, w_gate, w_up, w_down, x2)


def _combine_kernel(dst_cur_ref, dst_nxt_ref, x2_ref, route_ref, g_ref, b_ref, yd_hbm,
                    out_ref, ybuf, sems, *, tm, nsteps, alpha):
    i = pl.program_id(0)
    slot = i % 2

    @pl.when(i == 0)
    def _():
        _gather_rows(dst_cur_ref, yd_hbm, ybuf.at[0], sems.at[0], 2 * tm)

    @pl.when(i + 1 < nsteps)
    def _():
        _gather_rows(dst_nxt_ref, yd_hbm, ybuf.at[1 - slot], sems.at[1 - slot], 2 * tm)

    _wait_rows(yd_hbm, ybuf.at[slot], sems.at[slot], 2 * tm)
    route = route_ref[...]
    y = ybuf[slot, :tm, :] * route[:, 2:3] + ybuf[slot, tm:, :] * route[:, 3:4]
    out_ref[...] = _layer_norm(alpha * x2_ref[...] + y, g_ref[...], b_ref[...])


def _moe_combine(x2, route, dest, yd, ln_g, ln_b, alpha):
    t, d = x2.shape
    tm = TM_OUT
    nsteps = t // tm
    dst3 = dest.reshape(nsteps, tm, 2).transpose(0, 2, 1).reshape(nsteps, 1, 2 * tm)
    smem_blk = lambda f: pl.BlockSpec((1, 1, 2 * tm), f, memory_space=pltpu.SMEM)
    row = lambda n: pl.BlockSpec((tm, n), lambda i: (i, 0))
    return pl.pallas_call(
        functools.partial(_combine_kernel, tm=tm, nsteps=nsteps, alpha=alpha),
        grid=(nsteps,),
        out_shape=jax.ShapeDtypeStruct((t, d), F32),
        in_specs=[smem_blk(lambda i: (i, 0, 0)),
                  smem_blk(lambda i: (jnp.minimum(i + 1, nsteps - 1), 0, 0)),
                  row(d), row(LANES), _const_spec(ln_g.shape), _const_spec(ln_b.shape),
                  pl.BlockSpec(memory_space=pl.ANY)],
        out_specs=row(d),
        scratch_shapes=[pltpu.VMEM((2, 2 * tm, d), F32), pltpu.SemaphoreType.DMA((2,))],
        compiler_params=pltpu.CompilerParams(dimension_semantics=("arbitrary",),
                                             vmem_limit_bytes=VMEM_LIMIT),
        name="moe_combine",
    )(dst3, dst3, x2, route, ln_g, ln_b, yd)


def _prep_layer(w_in, b_in, gm_ln_g, gm_ln_b, gm_w_s, gm_b_s, w_gm_out, mla_q_norm_g, mla_kv_norm_g,
                w_uq, w_uk, w_uv, w_mla_out, w_o, ln1_g, ln1_b, w_mq, w_mk, w_mv, w_mo, ln2_g, ln2_b,
                w_group_router, b_group_router, w_expert_router, b_expert_router, depth):
    d = w_in.shape[0]
    s_v = 2 * GM_WIDTH
    s_q = s_v + MLA_Q_LORA
    s_kv = s_q + MLA_KV_LORA
    s_r = s_kv + MLA_ROPE
    rowv = lambda a: a.reshape(1, -1).astype(F32)

    def lat_cols(a):
        z = jnp.zeros(a.shape[:-1] + (MLA_NOPE,), a.dtype)
        kr = a[..., s_kv:s_r]
        return jnp.concatenate([a[..., s_v:s_kv], z, kr, kr], axis=-1)

    wq3 = w_uq.reshape(MLA_Q_LORA, MLA_HEADS, MLA_NOPE + MLA_ROPE)
    wq_pad = jnp.concatenate([wq3, wq3[..., MLA_NOPE:]], axis=-1).reshape(MLA_Q_LORA, MLA_HEADS * HEAD_PAD)
    wk3 = w_uk.reshape(MLA_KV_LORA, MLA_HEADS, MLA_NOPE)
    wk_pad = jnp.pad(wk3, ((0, 0), (0, 0), (0, HEAD_PAD - MLA_NOPE))).reshape(MLA_KV_LORA, MLA_HEADS * HEAD_PAD)
    w_r = jnp.zeros((d, LANES), F32)
    w_r = w_r.at[:, :N_EXPERTS].set(w_expert_router).at[:, N_EXPERTS:N_EXPERTS + N_GROUPS].set(w_group_router)
    b_r = jnp.zeros((LANES,), F32)
    b_r = b_r.at[:N_EXPERTS].set(b_expert_router).at[N_EXPERTS:N_EXPERTS + N_GROUPS].set(b_group_router)
    wr_hi = w_r.astype(BF16)
    return dict(
        w_uv=w_in[:, :s_v].astype(BF16), b_uv=rowv(b_in[:s_v]),
        w_gate=w_in[:, s_r:].astype(BF16), b_gate=rowv(b_in[s_r:]),
        w_lat=lat_cols(w_in).astype(BF16), b_lat=rowv(lat_cols(b_in)),
        gm_ln_g=rowv(gm_ln_g), gm_ln_b=rowv(gm_ln_b),
        ws=gm_w_s.astype(BF16), bs_full=jnp.repeat(gm_b_s.T, GM_CHUNK, axis=1).astype(F32),
        w_gm_out=w_gm_out.astype(BF16), qn_g=rowv(mla_q_norm_g), kvn_g=rowv(mla_kv_norm_g),
        w_uq=wq_pad.astype(BF16), w_uk=wk_pad.astype(BF16), w_uvv=w_uv.astype(BF16),
        w_mla_out=w_mla_out.astype(BF16), w_o=w_o.astype(BF16), ln1_g=rowv(ln1_g), ln1_b=rowv(ln1_b),
        w_mq=w_mq.astype(BF16), w_mk=w_mk.astype(BF16), w_mv=w_mv.astype(BF16), w_mo=w_mo.astype(BF16),
        ln2_g=rowv(ln2_g), ln2_b=rowv(ln2_b),
        wr_hi=wr_hi, wr_lo=(w_r - wr_hi.astype(F32)).astype(BF16), br=rowv(b_r),
        alpha=(2 * depth) ** 0.25,
    )


def _dispatch_tables(route, counts, t):
    rows = MOE_ROWS
    e = route[:, 0:2].astype(jnp.int32)
    rank = route[:, 4:6].astype(jnp.int32)
    cnt = counts[0, :N_EXPERTS].astype(jnp.int32)
    padded = (cnt + rows - 1) // rows * rows
    ends = jnp.cumsum(padded)
    starts = ends - padded
    dest = starts[e] + rank
    total = 2 * t + N_EXPERTS * rows
    nblocks = total // rows
    tok = jnp.broadcast_to(jnp.arange(t, dtype=jnp.int32)[:, None], (t, 2))
    row_src = jnp.zeros((total,), jnp.int32).at[dest.reshape(-1)].set(tok.reshape(-1))
    block_start = jnp.arange(nblocks, dtype=jnp.int32) * rows
    block_e = jnp.minimum(jnp.sum(ends[None, :] <= block_start[:, None], axis=1), N_EXPERTS - 1)
    nvalid = (ends[-1] // rows).reshape(1).astype(jnp.int32)
    return dest, row_src, block_e.astype(jnp.int32), nvalid


def _layer(x, mem, positions, depth, w_in, b_in, gm_ln_g, gm_ln_b, gm_w_s, gm_b_s, w_gm_out,
           mla_q_norm_g, mla_kv_norm_g, w_uq, w_uk, w_uv, w_mla_out, w_o, ln1_g, ln1_b,
           w_mq, w_mk, w_mv, w_mo, ln2_g, ln2_b,
           w_group_router, b_group_router, w_expert_router, b_expert_router,
           w_exp_gate, w_exp_up, w_exp_down, ln3_g, ln3_b):
    b, s, d = x.shape
    t = b * s
    mem_len = mem.shape[1]
    p = _prep_layer(w_in, b_in, gm_ln_g, gm_ln_b, gm_w_s, gm_b_s, w_gm_out, mla_q_norm_g, mla_kv_norm_g,
                    w_uq, w_uk, w_uv, w_mla_out, w_o, ln1_g, ln1_b, w_mq, w_mk, w_mv, w_mo, ln2_g, ln2_b,
                    w_group_router, b_group_router, w_expert_router, b_expert_router, depth)
    x2d = x.reshape(t, d)
    inv_freq = ROPE_THETA ** (-jnp.arange(ROPE_HALF, dtype=F32) / ROPE_HALF)
    cos, sin = _rope_tables(positions, inv_freq)
    gm, sg, q, k, v = _mixer_in(x2d, cos, sin, p)
    o = _mla_attn(q, k, v, b, s)
    mk, mv = _mem_kv(mem.reshape(b * mem_len, d), p["w_mk"], p["w_mv"], mem_len)
    x2, route, counts = _post_mixer(x2d, o, gm, sg, mk, mv, p, s, mem_len)
    dest, row_src, block_e, nvalid = _dispatch_tables(route, counts, t)
    yd = _moe_experts(x2, block_e, nvalid, row_src, w_exp_gate, w_exp_up, w_exp_down)
    out = _moe_combine(x2, route, dest, yd, ln3_g.reshape(1, d), ln3_b.reshape(1, d), p["alpha"])
    return out.reshape(b, s, d)


def kernel(x, mem, positions, w_in, b_in, gm_ln_g, gm_ln_b, gm_w_s, gm_b_s, w_gm_out, mla_q_norm_g, mla_kv_norm_g, w_uq, w_uk, w_uv, w_mla_out, w_o, ln1_g, ln1_b, w_mq, w_mk, w_mv, w_mo, ln2_g, ln2_b, w_group_router, b_group_router, w_expert_router, b_expert_router, w_exp_gate, w_exp_up, w_exp_down, ln3_g, ln3_b):
    depth = w_in.shape[0]
    per_layer = (w_in, b_in, gm_ln_g, gm_ln_b, gm_w_s, gm_b_s, w_gm_out, mla_q_norm_g, mla_kv_norm_g,
                 w_uq, w_uk, w_uv, w_mla_out, w_o, ln1_g, ln1_b, w_mq, w_mk, w_mv, w_mo, ln2_g, ln2_b,
                 w_group_router, b_group_router, w_expert_router, b_expert_router,
                 w_exp_gate, w_exp_up, w_exp_down, ln3_g, ln3_b)
    h = x
    for l in range(depth):
        h = _layer(h, mem, positions, depth, *[w[l] for w in per_layer])
    return h
```

```python
import functools
import math

import jax
import jax.numpy as jnp
from jax import lax
from jax.experimental import pallas as pl
from jax.experimental.pallas import tpu as pltpu

F32 = jnp.float32
BF16 = jnp.bfloat16

GM_WIDTH = 1024
GM_GROUPS = 8
GM_CHUNK = 128
MLA_HEADS = 16
MLA_NOPE = 64
MLA_ROPE = 32
MLA_V = 64
MLA_Q_LORA = 384
MLA_KV_LORA = 256
ROPE_THETA = 10000.0
MEM_HEADS = 4
N_GROUPS = 8
EXPERTS_PER_GROUP = 8
N_EXPERTS = N_GROUPS * EXPERTS_PER_GROUP
D_EXPERT = 256
LN_EPS = 1e-5
RMS_EPS = 1e-6

LANES = 128
HEAD_PAD = 128
V_ROWS = 80
ROPE_HALF = MLA_ROPE // 2

TM_IN = 256
TQ = 512
TK = 256
TM_POST = 256
MOE_ROWS = 256
TM_OUT = 256
VMEM_LIMIT = 56 * 1024 * 1024


def _const_spec(shape):
    nd = len(shape)
    return pl.BlockSpec(shape, lambda *_: (0,) * nd, pipeline_mode=pl.Buffered(1))


def _layer_norm(x, g, b):
    mu = jnp.mean(x, axis=-1, keepdims=True)
    xc = x - mu
    var = jnp.mean(xc * xc, axis=-1, keepdims=True)
    return xc * lax.rsqrt(var + LN_EPS) * g + b


def _rms_norm(x, g):
    return x * lax.rsqrt(jnp.mean(x * x, axis=-1, keepdims=True) + RMS_EPS) * g


def _gelu(x):
    return x * (lax.erf(x * (1.0 / math.sqrt(2.0))) + 1.0) * 0.5


def _dot(a, b):
    return jnp.dot(a, b, preferred_element_type=F32)


def _dot_nt(a, b):
    return lax.dot_general(a, b, (((1,), (1,)), ((), ())), preferred_element_type=F32)


def _rope_kernel(inv_ref, pos_ref, cos_ref, sin_ref):
    p = pos_ref[...].astype(F32)
    for j in range(ROPE_HALF):
        ang = p * inv_ref[j]
        cos_ref[j] = jnp.cos(ang)
        sin_ref[j] = jnp.sin(ang)


def _rope_tables(positions, inv_freq):
    t = positions.size
    rows = t // LANES
    pos2 = positions.reshape(rows, LANES)
    cos, sin = pl.pallas_call(
        _rope_kernel,
        out_shape=(jax.ShapeDtypeStruct((ROPE_HALF, rows, LANES), F32),) * 2,
        in_specs=[pl.BlockSpec(memory_space=pltpu.SMEM),
                  pl.BlockSpec((rows, LANES), lambda: (0, 0))],
        out_specs=(pl.BlockSpec((ROPE_HALF, rows, LANES), lambda: (0, 0, 0)),) * 2,
        name="rope_tables",
    )(inv_freq, pos2)
    cos = jnp.tile(cos.reshape(ROPE_HALF, t).T, (1, LANES // ROPE_HALF))
    sin = jnp.tile(sin.reshape(ROPE_HALF, t).T, (1, LANES // ROPE_HALF))
    return cos, sin


def _mixer_in_kernel(x_ref, cos_ref, sin_ref, w_uv_ref, b_uv_ref, w_gate_ref, b_gate_ref,
                     w_lat_ref, b_lat_ref, ln_g_ref, ln_b_ref, ws_ref, bs_ref, w_gm_out_ref,
                     qn_g_ref, kvn_g_ref, w_uq_ref, w_uk_ref, w_uvv_ref, b_v_ref,
                     gm_ref, sg_ref, q_ref, k_ref, v_ref, mixed_ref, *, tm, scale):
    xb = x_ref[...].astype(BF16)

    uv = _gelu(_dot(xb, w_uv_ref[...]) + b_uv_ref[...])
    u = uv[:, :GM_WIDTH]
    vn = _layer_norm(uv[:, GM_WIDTH:], ln_g_ref[...], ln_b_ref[...]).astype(BF16)
    r = lax.broadcasted_iota(jnp.int32, (GM_CHUNK, GM_CHUNK), 0)
    c = lax.broadcasted_iota(jnp.int32, (GM_CHUNK, GM_CHUNK), 1)
    causal = c <= r
    for g in range(GM_GROUPS):
        w = jnp.where(causal, ws_ref[g], jnp.zeros_like(ws_ref[g]))
        for ch in range(tm // GM_CHUNK):
            rows = slice(ch * GM_CHUNK, (ch + 1) * GM_CHUNK)
            cols = slice(g * GM_CHUNK, (g + 1) * GM_CHUNK)
            mixed_ref[rows, cols] = _dot(w, vn[rows, cols]) + bs_ref[:, cols]
    gated = (u * mixed_ref[...]).astype(BF16)
    y_gm = _dot(gated, w_gm_out_ref[...])
    gates = _dot(xb, w_gate_ref[...]) + b_gate_ref[...]
    gm_ref[...] = (jax.nn.sigmoid(gates[:, :GM_WIDTH]) * y_gm).astype(BF16)
    sg_ref[...] = jax.nn.sigmoid(gates[:, GM_WIDTH:]).astype(BF16)

    lat = _dot(xb, w_lat_ref[...]) + b_lat_ref[...]
    cq = _rms_norm(lat[:, :MLA_Q_LORA], qn_g_ref[...]).astype(BF16)
    ckv = _rms_norm(lat[:, MLA_Q_LORA:MLA_Q_LORA + MLA_KV_LORA], kvn_g_ref[...]).astype(BF16)
    kr = lat[:, MLA_Q_LORA + MLA_KV_LORA:]

    lane = lax.broadcasted_iota(jnp.int32, (tm, HEAD_PAD), 1)
    cosv = cos_ref[...]
    sinv = sin_ref[...]
    in_rope = (lane >= MLA_NOPE) & (lane < MLA_NOPE + MLA_ROPE)
    cfac = jnp.where(lane < MLA_NOPE, 1.0, jnp.where(in_rope, cosv, 0.0))
    sfac = jnp.where(in_rope, jnp.where(lane < MLA_NOPE + ROPE_HALF, -sinv, sinv), 0.0)

    def rot(t):
        return t * cfac + pltpu.roll(t, HEAD_PAD - ROPE_HALF, 1) * sfac

    kr_rot = rot(kr)
    qa = _dot(cq, w_uq_ref[...])
    kn = _dot(ckv, w_uk_ref[...])
    for h in range(MLA_HEADS):
        cols = slice(h * HEAD_PAD, (h + 1) * HEAD_PAD)
        q_ref[:, cols] = (rot(qa[:, cols]) * scale).astype(BF16)
        k_ref[:, cols] = (kn[:, cols] + kr_rot).astype(BF16)
    v_ref[...] = (_dot_nt(w_uvv_ref[...], ckv) + b_v_ref[...]).astype(BF16)


def _mixer_in(x2d, cos, sin, p):
    t, d = x2d.shape
    tm = TM_IN
    scale = (MLA_NOPE + MLA_ROPE) ** -0.5 * math.log2(math.e)
    row = lambda n: pl.BlockSpec((tm, n), lambda i: (i, 0))
    weights = [p["w_uv"], p["b_uv"], p["w_gate"], p["b_gate"], p["w_lat"], p["b_lat"],
               p["gm_ln_g"], p["gm_ln_b"], p["ws"], p["bs_full"], p["w_gm_out"],
               p["qn_g"], p["kvn_g"], p["w_uq"], p["w_uk"], p["w_uvv"], p["b_v"]]
    qk_w = MLA_HEADS * HEAD_PAD
    v_w = MLA_HEADS * V_ROWS
    return pl.pallas_call(
        functools.partial(_mixer_in_kernel, tm=tm, scale=scale),
        grid=(t // tm,),
        out_shape=(jax.ShapeDtypeStruct((t, d), BF16), jax.ShapeDtypeStruct((t, d), BF16),
                   jax.ShapeDtypeStruct((t, qk_w), BF16), jax.ShapeDtypeStruct((t, qk_w), BF16),
                   jax.ShapeDtypeStruct((v_w, t), BF16)),
        in_specs=[row(d), row(LANES), row(LANES)] + [_const_spec(w.shape) for w in weights],
        out_specs=(row(d), row(d), row(qk_w), row(qk_w), pl.BlockSpec((v_w, tm), lambda i: (0, i))),
        scratch_shapes=[pltpu.VMEM((tm, GM_WIDTH), F32)],
        compiler_params=pltpu.CompilerParams(dimension_semantics=("arbitrary",),
                                             vmem_limit_bytes=VMEM_LIMIT),
        name="mixer_in",
    )(x2d, cos, sin, *weights)


def _attn_kernel(q_ref, k_ref, vt_ref, o_ref, acc_ref, stat_ref, s0_ref, s1_ref, *, tq, tk):
    i = pl.program_id(2)
    acc_ref[...] = jnp.zeros_like(acc_ref)
    stat_ref[...] = jnp.full(stat_ref.shape, -jnp.inf, F32)

    def scores(j, s_ref):
        kb = k_ref[pl.ds(pl.multiple_of(j * tk, tk), tk), :]
        for a in range(2):
            cols = slice(a * HEAD_PAD, (a + 1) * HEAD_PAD)
            s_ref[a] = _dot_nt(kb[:, cols], q_ref[:, cols])

    def update(j, s_ref, masked):
        start = pl.multiple_of(j * tk, tk)
        vb = vt_ref[:, pl.ds(start, tk)]
        for a in range(2):
            m = stat_ref[a:a + 1, :]
            rows = slice(a * V_ROWS, (a + 1) * V_ROWS)
            s = s_ref[a]
            if masked:
                key = start + lax.broadcasted_iota(jnp.int32, (tk, tq), 0)
                qry = i * tq + lax.broadcasted_iota(jnp.int32, (tk, tq), 1)
                s = jnp.where(key <= qry, s, -jnp.inf)
            m_new = jnp.maximum(m, jnp.max(s, axis=0, keepdims=True))
            alpha = jnp.exp2(m - m_new)
            pr = jnp.exp2(s - m_new)
            stat_ref[a:a + 1, :] = m_new
            acc_ref[rows, :] = acc_ref[rows, :] * alpha + _dot(vb[rows, :], pr.astype(BF16))

    scores(0, s0_ref)

    def main(jj, carry):
        scores(2 * jj + 1, s1_ref)
        update(2 * jj, s0_ref, False)
        scores(2 * jj + 2, s0_ref)
        update(2 * jj + 1, s1_ref, False)
        return carry

    lax.fori_loop(0, i, main, 0)
    scores(2 * i + 1, s1_ref)
    update(2 * i, s0_ref, True)
    update(2 * i + 1, s1_ref, True)
    out = [acc_ref[a * V_ROWS:a * V_ROWS + MLA_V, :] / acc_ref[a * V_ROWS + MLA_V:a * V_ROWS + MLA_V + 1, :]
           for a in range(2)]
    o_ref[...] = jnp.concatenate(out, axis=0).T.astype(BF16)


def _mla_attn(q, k, vt, batch, seq):
    assert TQ == 2 * TK
    nq = seq // TQ
    return pl.pallas_call(
        functools.partial(_attn_kernel, tq=TQ, tk=TK),
        grid=(batch, MLA_HEADS // 2, nq),
        out_shape=jax.ShapeDtypeStruct((batch * seq, MLA_HEADS * MLA_V), BF16),
        in_specs=[pl.BlockSpec((TQ, 2 * HEAD_PAD), lambda b, h, i: (b * nq + i, h)),
                  pl.BlockSpec((seq, 2 * HEAD_PAD), lambda b, h, i: (b, h)),
                  pl.BlockSpec((2 * V_ROWS, seq), lambda b, h, i: (h, b))],
        out_specs=pl.BlockSpec((TQ, 2 * MLA_V), lambda b, h, i: (b * nq + i, h)),
        scratch_shapes=[pltpu.VMEM((2 * V_ROWS, TQ), F32), pltpu.VMEM((8, TQ), F32),
                        pltpu.VMEM((2, TK, TQ), F32), pltpu.VMEM((2, TK, TQ), F32)],
        compiler_params=pltpu.CompilerParams(
            dimension_semantics=("arbitrary", "arbitrary", "arbitrary"), vmem_limit_bytes=VMEM_LIMIT),
        name="mla_attn",
    )(q, k, vt)


def _mem_kv_kernel(mem_ref, wk_ref, wv_ref, k_ref, v_ref):
    mb = mem_ref[...].astype(BF16)
    k_ref[...] = _dot(mb, wk_ref[...]).astype(BF16)
    v_ref[...] = _dot(mb, wv_ref[...]).astype(BF16)


def _mem_kv(mem2d, w_mk, w_mv, mem_len):
    rows, d = mem2d.shape
    blk = pl.BlockSpec((mem_len, d), lambda i: (i, 0))
    return pl.pallas_call(
        _mem_kv_kernel,
        grid=(rows // mem_len,),
        out_shape=(jax.ShapeDtypeStruct((rows, d), BF16),) * 2,
        in_specs=[blk, _const_spec(w_mk.shape), _const_spec(w_mv.shape)],
        out_specs=(blk, blk),
        compiler_params=pltpu.CompilerParams(dimension_semantics=("arbitrary",)),
        name="mem_kv",
    )(mem2d, w_mk, w_mv)


def _post_mixer_kernel(x_ref, o_ref, gm_ref, sg_ref, mk_ref, mv_ref,
                       w_mla_out_ref, w_o_ref, ln1_g_ref, ln1_b_ref,
                       w_mq_ref, w_mo_ref, ln2_g_ref, ln2_b_ref,
                       wr_hi_ref, wr_lo_ref, br_ref,
                       x2_ref, route_ref, counts_ref, run_ref, *, tm, alpha, mem_scale):
    i = pl.program_id(0)

    @pl.when(i == 0)
    def _():
        run_ref[...] = jnp.zeros_like(run_ref)

    y_mla = _dot(o_ref[...], w_mla_out_ref[...])
    merged = gm_ref[...].astype(F32) + sg_ref[...].astype(F32) * y_mla
    x1 = _layer_norm(alpha * x_ref[...] + _dot(merged.astype(BF16), w_o_ref[...]),
                     ln1_g_ref[...], ln1_b_ref[...])

    qm = (_dot(x1.astype(BF16), w_mq_ref[...]) * mem_scale).astype(BF16)
    hd = qm.shape[1] // MEM_HEADS
    heads = []
    for h in range(MEM_HEADS):
        cols = slice(h * hd, (h + 1) * hd)
        s = _dot_nt(qm[:, cols], mk_ref[:, cols])
        pr = jnp.exp(s - jnp.max(s, axis=1, keepdims=True))
        heads.append(_dot(pr.astype(BF16), mv_ref[:, cols]) / jnp.sum(pr, axis=1, keepdims=True))
    om = jnp.concatenate(heads, axis=1).astype(BF16)
    x2 = _layer_norm(alpha * x1 + _dot(om, w_mo_ref[...]), ln2_g_ref[...], ln2_b_ref[...])
    x2_ref[...] = x2

    x_hi = x2.astype(BF16)
    x_lo = (x2 - x_hi.astype(F32)).astype(BF16)
    logits = (_dot(x_hi, wr_hi_ref[...]) + _dot(x_lo, wr_hi_ref[...])
              + _dot(x_hi, wr_lo_ref[...]) + br_ref[...])
    lane = lax.broadcasted_iota(jnp.int32, (tm, LANES), 1).astype(F32)
    big = jnp.float32(1e9)

    def first_argmax(vals, vmax):
        return jnp.min(jnp.where(vals == vmax, lane, big), axis=1, keepdims=True)

    g_mask = (lane >= N_EXPERTS) & (lane < N_EXPERTS + N_GROUPS)
    lg = jnp.where(g_mask, logits, -jnp.inf)
    g_max = jnp.max(lg, axis=1, keepdims=True)
    g_sel = first_argmax(lg, g_max) - N_EXPERTS
    g_w = 1.0 / jnp.sum(jnp.where(g_mask, jnp.exp(logits - g_max), 0.0), axis=1, keepdims=True)
    in_group = jnp.floor(lane * (1.0 / EXPERTS_PER_GROUP)) == g_sel
    le = jnp.where(in_group, logits, -jnp.inf)
    v1 = jnp.max(le, axis=1, keepdims=True)
    e1 = first_argmax(le, v1)
    le2 = jnp.where(lane == e1, -jnp.inf, le)
    v2 = jnp.max(le2, axis=1, keepdims=True)
    e2 = first_argmax(le2, v2)
    t2 = jnp.exp(v2 - v1)
    w1 = (1.0 / (1.0 + t2)) * g_w
    w2 = (t2 / (1.0 + t2)) * g_w

    hit1 = lane == e1
    hit2 = lane == e2
    onehot = jnp.where(hit1 | hit2, 1.0, 0.0)
    r_i = lax.broadcasted_iota(jnp.int32, (tm, tm), 0)
    c_i = lax.broadcasted_iota(jnp.int32, (tm, tm), 1)
    tri = jnp.where(c_i < r_i, 1.0, 0.0).astype(BF16)
    before = _dot(tri, onehot.astype(BF16)) + run_ref[...]
    r1 = jnp.sum(jnp.where(hit1, before, 0.0), axis=1, keepdims=True)
    r2 = jnp.sum(jnp.where(hit2, before, 0.0), axis=1, keepdims=True)
    run_ref[...] = run_ref[...] + jnp.sum(onehot, axis=0, keepdims=True)
    counts_ref[...] = jnp.broadcast_to(run_ref[...], counts_ref.shape)

    packed = jnp.zeros((tm, LANES), F32)
    for idx, val in enumerate((e1, e2, w1, w2, r1, r2)):
        packed = jnp.where(lane == idx, val, packed)
    route_ref[...] = packed


def _post_mixer(x2d, o, gm, sg, mk, mv, p, seq, mem_len):
    t, d = x2d.shape
    tm = TM_POST
    per_batch = seq // tm
    row = lambda n: pl.BlockSpec((tm, n), lambda i: (i, 0))
    memblk = pl.BlockSpec((mem_len, d), lambda i: (i // per_batch, 0))
    weights = [p["w_mla_out"], p["w_o"], p["ln1_g"], p["ln1_b"], p["w_mq"], p["w_mo"],
               p["ln2_g"], p["ln2_b"], p["wr_hi"], p["wr_lo"], p["br"]]
    return pl.pallas_call(
        functools.partial(_post_mixer_kernel, tm=tm, alpha=p["alpha"],
                          mem_scale=(d // MEM_HEADS) ** -0.5),
        grid=(t // tm,),
        out_shape=(jax.ShapeDtypeStruct((t, d), F32), jax.ShapeDtypeStruct((t, LANES), F32),
                   jax.ShapeDtypeStruct((8, LANES), F32)),
        in_specs=[row(d), row(d), row(d), row(d), memblk, memblk]
                 + [_const_spec(w.shape) for w in weights],
        out_specs=(row(d), row(LANES), pl.BlockSpec((8, LANES), lambda i: (0, 0))),
        scratch_shapes=[pltpu.VMEM((1, LANES), F32)],
        compiler_params=pltpu.CompilerParams(dimension_semantics=("arbitrary",),
                                             vmem_limit_bytes=VMEM_LIMIT),
        name="post_mixer",
    )(x2d, o, gm, sg, mk, mv, *weights)


def _gather_rows(idx_ref, src_hbm, dst, sem, n):
    def body(r, carry):
        pltpu.make_async_copy(src_hbm.at[pl.ds(idx_ref[0, 0, r], 1)], dst.at[pl.ds(r, 1)], sem).start()
        return carry
    lax.fori_loop(0, n, body, 0, unroll=8)


def _wait_rows(src_hbm, dst, sem, n):
    pltpu.make_async_copy(src_hbm.at[pl.ds(0, n)], dst, sem).wait()


def _experts_kernel(be_ref, nvalid_ref, src_cur_ref, src_nxt_ref, wg_ref, wu_ref, wd_ref, x_hbm,
                    yd_ref, xbuf, sems, wgu_bf, wd_bf, *, rows, nblocks):
    i = pl.program_id(0)
    slot = i % 2
    nvalid = nvalid_ref[0]

    @pl.when(i == 0)
    def _():
        _gather_rows(src_cur_ref, x_hbm, xbuf.at[0], sems.at[0], rows)

    @pl.when(i + 1 < nvalid)
    def _():
        _gather_rows(src_nxt_ref, x_hbm, xbuf.at[1 - slot], sems.at[1 - slot], rows)

    changed = jnp.logical_or(i == 0, be_ref[i] != be_ref[jnp.maximum(i - 1, 0)])

    @pl.when(changed)
    def _():
        wgu_bf[:, :D_EXPERT] = wg_ref[...].astype(BF16)
        wgu_bf[:, D_EXPERT:] = wu_ref[...].astype(BF16)
        wd_bf[...] = wd_ref[...].astype(BF16)

    @pl.when(jnp.logical_or(i < nvalid, i == 0))
    def _():
        _wait_rows(x_hbm, xbuf.at[slot], sems.at[slot], rows)

    @pl.when(i < nvalid)
    def _():
        xb = xbuf[slot].astype(BF16)
        gu = _dot(xb, wgu_bf[...])
        gate = gu[:, :D_EXPERT]
        hidden = (gate * jax.nn.sigmoid(gate) * gu[:, D_EXPERT:]).astype(BF16)
        yd_ref[...] = _dot(hidden, wd_bf[...])

    @pl.when(i >= nvalid)
    def _():
        yd_ref[...] = jnp.zeros_like(yd_ref)


def _moe_experts(x2, block_e, nvalid, row_src, w_gate, w_up, w_down):
    t, d = x2.shape
    nblocks = block_e.shape[0]
    rows = MOE_ROWS
    src3 = row_src.reshape(nblocks, 1, rows)
    smem_blk = lambda f: pl.BlockSpec((1, 1, rows), f, memory_space=pltpu.SMEM)
    grid_spec = pltpu.PrefetchScalarGridSpec(
        num_scalar_prefetch=2,
        grid=(nblocks,),
        in_specs=[smem_blk(lambda i, be, nv: (i, 0, 0)),
                  smem_blk(lambda i, be, nv: (jnp.minimum(i + 1, nblocks - 1), 0, 0)),
                  pl.BlockSpec((None, d, D_EXPERT), lambda i, be, nv: (be[i], 0, 0)),
                  pl.BlockSpec((None, d, D_EXPERT), lambda i, be, nv: (be[i], 0, 0)),
                  pl.BlockSpec((None, D_EXPERT, d), lambda i, be, nv: (be[i], 0, 0)),
                  pl.BlockSpec(memory_space=pl.ANY)],
        out_specs=pl.BlockSpec((rows, d), lambda i, be, nv: (i, 0)),
        scratch_shapes=[pltpu.VMEM((2, rows, d), F32), pltpu.SemaphoreType.DMA((2,)),
                        pltpu.VMEM((d, 2 * D_EXPERT), BF16), pltpu.VMEM((D_EXPERT, d), BF16)],
    )
    return pl.pallas_call(
        functools.partial(_experts_kernel, rows=rows, nblocks=nblocks),
        grid_spec=grid_spec,
        out_shape=jax.ShapeDtypeStruct((nblocks * rows, d), F32),
        compiler_params=pltpu.CompilerParams(dimension_semantics=("arbitrary",),
                                             vmem_limit_bytes=VMEM_LIMIT),
        name="moe_experts",
    )(block_e, nvalid, src3, src3, w_gate, w_up, w_down, x2)


def _combine_kernel(dst_cur_ref, dst_nxt_ref, x2_ref, route_ref, g_ref, b_ref, yd_hbm,
                    out_ref, ybuf, sems, *, tm, nsteps, alpha):
    i = pl.program_id(0)
    slot = i % 2

    @pl.when(i == 0)
    def _():
        _gather_rows(dst_cur_ref, yd_hbm, ybuf.at[0], sems.at[0], 2 * tm)

    @pl.when(i + 1 < nsteps)
    def _():
        _gather_rows(dst_nxt_ref, yd_hbm, ybuf.at[1 - slot], sems.at[1 - slot], 2 * tm)

    _wait_rows(yd_hbm, ybuf.at[slot], sems.at[slot], 2 * tm)
    route = route_ref[...]
    y = ybuf[slot, :tm, :] * route[:, 2:3] + ybuf[slot, tm:, :] * route[:, 3:4]
    out_ref[...] = _layer_norm(alpha * x2_ref[...] + y, g_ref[...], b_ref[...])


def _moe_combine(x2, route, dest, yd, ln_g, ln_b, alpha):
    t, d = x2.shape
    tm = TM_OUT
    nsteps = t // tm
    dst3 = dest.reshape(nsteps, tm, 2).transpose(0, 2, 1).reshape(nsteps, 1, 2 * tm)
    smem_blk = lambda f: pl.BlockSpec((1, 1, 2 * tm), f, memory_space=pltpu.SMEM)
    row = lambda n: pl.BlockSpec((tm, n), lambda i: (i, 0))
    return pl.pallas_call(
        functools.partial(_combine_kernel, tm=tm, nsteps=nsteps, alpha=alpha),
        grid=(nsteps,),
        out_shape=jax.ShapeDtypeStruct((t, d), F32),
        in_specs=[smem_blk(lambda i: (i, 0, 0)),
                  smem_blk(lambda i: (jnp.minimum(i + 1, nsteps - 1), 0, 0)),
                  row(d), row(LANES), _const_spec(ln_g.shape), _const_spec(ln_b.shape),
                  pl.BlockSpec(memory_space=pl.ANY)],
        out_specs=row(d),
        scratch_shapes=[pltpu.VMEM((2, 2 * tm, d), F32), pltpu.SemaphoreType.DMA((2,))],
        compiler_params=pltpu.CompilerParams(dimension_semantics=("arbitrary",),
                                             vmem_limit_bytes=VMEM_LIMIT),
        name="moe_combine",
    )(dst3, dst3, x2, route, ln_g, ln_b, yd)


def _prep_layer(w_in, b_in, gm_ln_g, gm_ln_b, gm_w_s, gm_b_s, w_gm_out, mla_q_norm_g, mla_kv_norm_g,
                w_uq, w_uk, w_uv, w_mla_out, w_o, ln1_g, ln1_b, w_mq, w_mk, w_mv, w_mo, ln2_g, ln2_b,
                w_group_router, b_group_router, w_expert_router, b_expert_router, depth):
    d = w_in.shape[0]
    s_v = 2 * GM_WIDTH
    s_q = s_v + MLA_Q_LORA
    s_kv = s_q + MLA_KV_LORA
    s_r = s_kv + MLA_ROPE
    rowv = lambda a: a.reshape(1, -1).astype(F32)

    def lat_cols(a):
        z = jnp.zeros(a.shape[:-1] + (MLA_NOPE,), a.dtype)
        kr = a[..., s_kv:s_r]
        return jnp.concatenate([a[..., s_v:s_kv], z, kr, kr], axis=-1)

    wq3 = w_uq.reshape(MLA_Q_LORA, MLA_HEADS, MLA_NOPE + MLA_ROPE)
    wq_pad = jnp.concatenate([wq3, wq3[..., MLA_NOPE:]], axis=-1).reshape(MLA_Q_LORA, MLA_HEADS * HEAD_PAD)
    wk3 = w_uk.reshape(MLA_KV_LORA, MLA_HEADS, MLA_NOPE)
    wk_pad = jnp.pad(wk3, ((0, 0), (0, 0), (0, HEAD_PAD - MLA_NOPE))).reshape(MLA_KV_LORA, MLA_HEADS * HEAD_PAD)
    wv3 = w_uv.T.reshape(MLA_HEADS, MLA_V, MLA_KV_LORA)
    wv_pad = jnp.pad(wv3, ((0, 0), (0, V_ROWS - MLA_V), (0, 0))).reshape(MLA_HEADS * V_ROWS, MLA_KV_LORA)
    b_v = jnp.zeros((MLA_HEADS, V_ROWS, 1), F32).at[:, MLA_V].set(1.0).reshape(MLA_HEADS * V_ROWS, 1)
    w_r = jnp.zeros((d, LANES), F32)
    w_r = w_r.at[:, :N_EXPERTS].set(w_expert_router).at[:, N_EXPERTS:N_EXPERTS + N_GROUPS].set(w_group_router)
    b_r = jnp.zeros((LANES,), F32)
    b_r = b_r.at[:N_EXPERTS].set(b_expert_router).at[N_EXPERTS:N_EXPERTS + N_GROUPS].set(b_group_router)
    wr_hi = w_r.astype(BF16)
    return dict(
        w_uv=w_in[:, :s_v].astype(BF16), b_uv=rowv(b_in[:s_v]),
        w_gate=w_in[:, s_r:].astype(BF16), b_gate=rowv(b_in[s_r:]),
        w_lat=lat_cols(w_in).astype(BF16), b_lat=rowv(lat_cols(b_in)),
        gm_ln_g=rowv(gm_ln_g), gm_ln_b=rowv(gm_ln_b),
        ws=gm_w_s.astype(BF16), bs_full=jnp.repeat(gm_b_s.T, GM_CHUNK, axis=1).astype(F32),
        w_gm_out=w_gm_out.astype(BF16), qn_g=rowv(mla_q_norm_g), kvn_g=rowv(mla_kv_norm_g),
        w_uq=wq_pad.astype(BF16), w_uk=wk_pad.astype(BF16), w_uvv=wv_pad.astype(BF16), b_v=b_v,
        w_mla_out=w_mla_out.astype(BF16), w_o=w_o.astype(BF16), ln1_g=rowv(ln1_g), ln1_b=rowv(ln1_b),
        w_mq=w_mq.astype(BF16), w_mk=w_mk.astype(BF16), w_mv=w_mv.astype(BF16), w_mo=w_mo.astype(BF16),
        ln2_g=rowv(ln2_g), ln2_b=rowv(ln2_b),
        wr_hi=wr_hi, wr_lo=(w_r - wr_hi.astype(F32)).astype(BF16), br=rowv(b_r),
        alpha=(2 * depth) ** 0.25,
    )


def _dispatch_tables(route, counts, t):
    rows = MOE_ROWS
    e = route[:, 0:2].astype(jnp.int32)
    rank = route[:, 4:6].astype(jnp.int32)
    cnt = counts[0, :N_EXPERTS].astype(jnp.int32)
    padded = (cnt + rows - 1) // rows * rows
    ends = jnp.cumsum(padded)
    starts = ends - padded
    dest = starts[e] + rank
    total = 2 * t + N_EXPERTS * rows
    nblocks = total // rows
    tok = jnp.broadcast_to(jnp.arange(t, dtype=jnp.int32)[:, None], (t, 2))
    row_src = jnp.zeros((total,), jnp.int32).at[dest.reshape(-1)].set(tok.reshape(-1))
    block_start = jnp.arange(nblocks, dtype=jnp.int32) * rows
    block_e = jnp.minimum(jnp.sum(ends[None, :] <= block_start[:, None], axis=1), N_EXPERTS - 1)
    nvalid = (ends[-1] // rows).reshape(1).astype(jnp.int32)
    return dest, row_src, block_e.astype(jnp.int32), nvalid


def _layer(x, mem, positions, depth, w_in, b_in, gm_ln_g, gm_ln_b, gm_w_s, gm_b_s, w_gm_out,
           mla_q_norm_g, mla_kv_norm_g, w_uq, w_uk, w_uv, w_mla_out, w_o, ln1_g, ln1_b,
           w_mq, w_mk, w_mv, w_mo, ln2_g, ln2_b,
           w_group_router, b_group_router, w_expert_router, b_expert_router,
           w_exp_gate, w_exp_up, w_exp_down, ln3_g, ln3_b):
    b, s, d = x.shape
    t = b * s
    mem_len = mem.shape[1]
    p = _prep_layer(w_in, b_in, gm_ln_g, gm_ln_b, gm_w_s, gm_b_s, w_gm_out, mla_q_norm_g, mla_kv_norm_g,
                    w_uq, w_uk, w_uv, w_mla_out, w_o, ln1_g, ln1_b, w_mq, w_mk, w_mv, w_mo, ln2_g, ln2_b,
                    w_group_router, b_group_router, w_expert_router, b_expert_router, depth)
    x2d = x.reshape(t, d)
    inv_freq = ROPE_THETA ** (-jnp.arange(ROPE_HALF, dtype=F32) / ROPE_HALF)
    cos, sin = _rope_tables(positions, inv_freq)
    gm, sg, q, k, v = _mixer_in(x2d, cos, sin, p)
    o = _mla_attn(q, k, v, b, s)
    mk, mv = _mem_kv(mem.reshape(b * mem_len, d), p["w_mk"], p["w_mv"], mem_len)
    x2, route, counts = _post_mixer(x2d, o, gm, sg, mk, mv, p, s, mem_len)
    dest, row_src, block_e, nvalid = _dispatch_tables(route, counts, t)
    yd = _moe_experts(x2, block_e, nvalid, row_src, w_exp_gate, w_exp_up, w_exp_down)
    out = _moe_combine(x2, route, dest, yd, ln3_g.reshape(1, d), ln3_b.reshape(1, d), p["alpha"])
    return out.reshape(b, s, d)


def kernel(x, mem, positions, w_in, b_in, gm_ln_g, gm_ln_b, gm_w_s, gm_b_s, w_gm_out, mla_q_norm_g, mla_kv_norm_g, w_uq, w_uk, w_uv, w_mla_out, w_o, ln1_g, ln1_b, w_mq, w_mk, w_mv, w_mo, ln2_g, ln2_b, w_group_router, b_group_router, w_expert_router, b_expert_router, w_exp_gate, w_exp_up, w_exp_down, ln3_g, ln3_b):
    depth = w_in.shape[0]
    per_layer = (w_in, b_in, gm_ln_g, gm_ln_b, gm_w_s, gm_b_s, w_gm_out, mla_q_norm_g, mla_kv_norm_g,
                 w_uq, w_uk, w_uv, w_mla_out, w_o, ln1_g, ln1_b, w_mq, w_mk, w_mv, w_mo, ln2_g, ln2_b,
                 w_group_router, b_group_router, w_expert_router, b_expert_router,
                 w_exp_gate, w_exp_up, w_exp_down, ln3_g, ln3_b)
    h = x
    for l in range(depth):
        h = _layer(h, mem, positions, depth, *[w[l] for w in per_layer])
    return h
```

```python
import functools
import math

import jax
import jax.numpy as jnp
from jax import lax
from jax.experimental import pallas as pl
from jax.experimental.pallas import tpu as pltpu

F32 = jnp.float32
BF16 = jnp.bfloat16

GM_WIDTH = 1024
GM_GROUPS = 8
GM_CHUNK = 128
MLA_HEADS = 16
MLA_NOPE = 64
MLA_ROPE = 32
MLA_V = 64
MLA_Q_LORA = 384
MLA_KV_LORA = 256
ROPE_THETA = 10000.0
MEM_HEADS = 4
N_GROUPS = 8
EXPERTS_PER_GROUP = 8
N_EXPERTS = N_GROUPS * EXPERTS_PER_GROUP
D_EXPERT = 256
LN_EPS = 1e-5
RMS_EPS = 1e-6

LANES = 128
HEAD_PAD = 128
V_ROWS = 80
ROPE_HALF = MLA_ROPE // 2

TM_IN = 256
TQ = 512
TK = 256
SM_CHUNK = 16
TM_POST = 256
MOE_ROWS = 256
TM_OUT = 256
VMEM_LIMIT = 56 * 1024 * 1024


def _const_spec(shape):
    nd = len(shape)
    return pl.BlockSpec(shape, lambda *_: (0,) * nd, pipeline_mode=pl.Buffered(1))


def _layer_norm(x, g, b):
    mu = jnp.mean(x, axis=-1, keepdims=True)
    xc = x - mu
    var = jnp.mean(xc * xc, axis=-1, keepdims=True)
    return xc * lax.rsqrt(var + LN_EPS) * g + b


def _rms_norm(x, g):
    return x * lax.rsqrt(jnp.mean(x * x, axis=-1, keepdims=True) + RMS_EPS) * g


def _gelu(x):
    return x * (lax.erf(x * (1.0 / math.sqrt(2.0))) + 1.0) * 0.5


def _dot(a, b):
    return jnp.dot(a, b, preferred_element_type=F32)


def _dot_nt(a, b):
    return lax.dot_general(a, b, (((1,), (1,)), ((), ())), preferred_element_type=F32)


def _rope_kernel(inv_ref, pos_ref, cos_ref, sin_ref):
    p = pos_ref[...].astype(F32)
    for j in range(ROPE_HALF):
        ang = p * inv_ref[j]
        cos_ref[j] = jnp.cos(ang)
        sin_ref[j] = jnp.sin(ang)


def _rope_tables(positions, inv_freq):
    t = positions.size
    rows = t // LANES
    pos2 = positions.reshape(rows, LANES)
    cos, sin = pl.pallas_call(
        _rope_kernel,
        out_shape=(jax.ShapeDtypeStruct((ROPE_HALF, rows, LANES), F32),) * 2,
        in_specs=[pl.BlockSpec(memory_space=pltpu.SMEM),
                  pl.BlockSpec((rows, LANES), lambda: (0, 0))],
        out_specs=(pl.BlockSpec((ROPE_HALF, rows, LANES), lambda: (0, 0, 0)),) * 2,
        name="rope_tables",
    )(inv_freq, pos2)
    cos = jnp.tile(cos.reshape(ROPE_HALF, t).T, (1, LANES // ROPE_HALF))
    sin = jnp.tile(sin.reshape(ROPE_HALF, t).T, (1, LANES // ROPE_HALF))
    return cos, sin


def _mixer_in_kernel(x_ref, cos_ref, sin_ref, w_uv_ref, b_uv_ref, w_gate_ref, b_gate_ref,
                     w_lat_ref, b_lat_ref, ln_g_ref, ln_b_ref, ws_ref, bs_ref, w_gm_out_ref,
                     qn_g_ref, kvn_g_ref, w_uq_ref, w_uk_ref, w_uvv_ref, b_v_ref,
                     gm_ref, sg_ref, q_ref, k_ref, v_ref, mixed_ref, *, tm, scale):
    xb = x_ref[...].astype(BF16)

    uv = _gelu(_dot(xb, w_uv_ref[...]) + b_uv_ref[...])
    u = uv[:, :GM_WIDTH]
    vn = _layer_norm(uv[:, GM_WIDTH:], ln_g_ref[...], ln_b_ref[...]).astype(BF16)
    r = lax.broadcasted_iota(jnp.int32, (GM_CHUNK, GM_CHUNK), 0)
    c = lax.broadcasted_iota(jnp.int32, (GM_CHUNK, GM_CHUNK), 1)
    causal = c <= r
    for g in range(GM_GROUPS):
        w = jnp.where(causal, ws_ref[g], jnp.zeros_like(ws_ref[g]))
        for ch in range(tm // GM_CHUNK):
            rows = slice(ch * GM_CHUNK, (ch + 1) * GM_CHUNK)
            cols = slice(g * GM_CHUNK, (g + 1) * GM_CHUNK)
            mixed_ref[rows, cols] = _dot(w, vn[rows, cols]) + bs_ref[:, cols]
    gated = (u * mixed_ref[...]).astype(BF16)
    y_gm = _dot(gated, w_gm_out_ref[...])
    gates = _dot(xb, w_gate_ref[...]) + b_gate_ref[...]
    gm_ref[...] = (jax.nn.sigmoid(gates[:, :GM_WIDTH]) * y_gm).astype(BF16)
    sg_ref[...] = jax.nn.sigmoid(gates[:, GM_WIDTH:]).astype(BF16)

    lat = _dot(xb, w_lat_ref[...]) + b_lat_ref[...]
    cq = _rms_norm(lat[:, :MLA_Q_LORA], qn_g_ref[...]).astype(BF16)
    ckv = _rms_norm(lat[:, MLA_Q_LORA:MLA_Q_LORA + MLA_KV_LORA], kvn_g_ref[...]).astype(BF16)
    kr = lat[:, MLA_Q_LORA + MLA_KV_LORA:]

    lane = lax.broadcasted_iota(jnp.int32, (tm, HEAD_PAD), 1)
    cosv = cos_ref[...]
    sinv = sin_ref[...]
    in_rope = (lane >= MLA_NOPE) & (lane < MLA_NOPE + MLA_ROPE)
    cfac = jnp.where(lane < MLA_NOPE, 1.0, jnp.where(in_rope, cosv, 0.0))
    sfac = jnp.where(in_rope, jnp.where(lane < MLA_NOPE + ROPE_HALF, -sinv, sinv), 0.0)

    def rot(t):
        return t * cfac + pltpu.roll(t, HEAD_PAD - ROPE_HALF, 1) * sfac

    kr_rot = rot(kr)
    qa = _dot(cq, w_uq_ref[...])
    kn = _dot(ckv, w_uk_ref[...])
    for h in range(MLA_HEADS):
        cols = slice(h * HEAD_PAD, (h + 1) * HEAD_PAD)
        q_ref[:, cols] = (rot(qa[:, cols]) * scale).astype(BF16)
        k_ref[:, cols] = (kn[:, cols] + kr_rot).astype(BF16)
    v_ref[...] = (_dot_nt(w_uvv_ref[...], ckv) + b_v_ref[...]).astype(BF16)


def _mixer_in(x2d, cos, sin, p):
    t, d = x2d.shape
    tm = TM_IN
    scale = (MLA_NOPE + MLA_ROPE) ** -0.5 * math.log2(math.e)
    row = lambda n: pl.BlockSpec((tm, n), lambda i: (i, 0))
    weights = [p["w_uv"], p["b_uv"], p["w_gate"], p["b_gate"], p["w_lat"], p["b_lat"],
               p["gm_ln_g"], p["gm_ln_b"], p["ws"], p["bs_full"], p["w_gm_out"],
               p["qn_g"], p["kvn_g"], p["w_uq"], p["w_uk"], p["w_uvv"], p["b_v"]]
    qk_w = MLA_HEADS * HEAD_PAD
    v_w = MLA_HEADS * V_ROWS
    return pl.pallas_call(
        functools.partial(_mixer_in_kernel, tm=tm, scale=scale),
        grid=(t // tm,),
        out_shape=(jax.ShapeDtypeStruct((t, d), BF16), jax.ShapeDtypeStruct((t, d), BF16),
                   jax.ShapeDtypeStruct((t, qk_w), BF16), jax.ShapeDtypeStruct((t, qk_w), BF16),
                   jax.ShapeDtypeStruct((v_w, t), BF16)),
        in_specs=[row(d), row(LANES), row(LANES)] + [_const_spec(w.shape) for w in weights],
        out_specs=(row(d), row(d), row(qk_w), row(qk_w), pl.BlockSpec((v_w, tm), lambda i: (0, i))),
        scratch_shapes=[pltpu.VMEM((tm, GM_WIDTH), F32)],
        compiler_params=pltpu.CompilerParams(dimension_semantics=("arbitrary",),
                                             vmem_limit_bytes=VMEM_LIMIT),
        name="mixer_in",
    )(x2d, cos, sin, *weights)


def _attn_kernel(q_ref, k_ref, vt_ref, o_ref, acc_ref, m_ref, r0_ref, r1_ref, s0_ref, s1_ref, p0_ref, p1_ref,
                 *, tq, tk):
    i = pl.program_id(2)
    acc_ref[...] = jnp.zeros_like(acc_ref)
    m_ref[...] = jnp.full(m_ref.shape, -jnp.inf, F32)

    def scores(j, s_ref):
        kb = k_ref[pl.ds(pl.multiple_of(j * tk, tk), tk), :]
        for a in range(2):
            cols = slice(a * HEAD_PAD, (a + 1) * HEAD_PAD)
            s_ref[a] = _dot_nt(kb[:, cols], q_ref[:, cols])

    def softmax(j, s_ref, p_ref, r_ref, masked):
        nch = tk // SM_CHUNK
        if masked:
            qry = i * tq + lax.broadcasted_iota(jnp.int32, (SM_CHUNK, tq), 1)
            key0 = j * tk + lax.broadcasted_iota(jnp.int32, (SM_CHUNK, tq), 0)

        def chunk(a, c):
            s = s_ref[a, c * SM_CHUNK:(c + 1) * SM_CHUNK, :]
            if masked:
                s = jnp.where(key0 + c * SM_CHUNK <= qry, s, -jnp.inf)
            return s

        for a in range(2):
            m = m_ref[a:a + 1, :]
            cm = chunk(a, 0)
            for c in range(1, nch):
                cm = jnp.maximum(cm, chunk(a, c))
            m_new = jnp.maximum(m, jnp.max(cm, axis=0, keepdims=True))
            r_ref[a:a + 1, :] = jnp.exp2(m - m_new)
            m_ref[a:a + 1, :] = m_new
            mb = jnp.broadcast_to(m_new, (SM_CHUNK, tq))
            for c in range(nch):
                p_ref[a, c * SM_CHUNK:(c + 1) * SM_CHUNK, :] = jnp.exp2(chunk(a, c) - mb).astype(BF16)

    def accumulate(j, p_ref, r_ref):
        vb = vt_ref[:, pl.ds(pl.multiple_of(j * tk, tk), tk)]
        for a in range(2):
            rows = slice(a * V_ROWS, (a + 1) * V_ROWS)
            alpha = r_ref[a:a + 1, :]
            acc_ref[rows, :] = acc_ref[rows, :] * alpha + _dot(vb[rows, :], p_ref[a])

    def pair(base):
        scores(base + 1, s1_ref)
        softmax(base, s0_ref, p0_ref, r0_ref, False)
        accumulate(base, p0_ref, r0_ref)
        scores(base + 2, s0_ref)
        softmax(base + 1, s1_ref, p1_ref, r1_ref, False)
        accumulate(base + 1, p1_ref, r1_ref)

    def main(t, carry):
        pair(4 * t)
        pair(4 * t + 2)
        return carry

    scores(0, s0_ref)
    lax.fori_loop(0, i // 2, main, 0)

    @pl.when(i % 2 == 1)
    def _():
        pair(2 * i - 2)

    scores(2 * i + 1, s1_ref)
    softmax(2 * i, s0_ref, p0_ref, r0_ref, True)
    accumulate(2 * i, p0_ref, r0_ref)
    softmax(2 * i + 1, s1_ref, p1_ref, r1_ref, True)
    accumulate(2 * i + 1, p1_ref, r1_ref)
    out = [acc_ref[a * V_ROWS:a * V_ROWS + MLA_V, :] / acc_ref[a * V_ROWS + MLA_V:a * V_ROWS + MLA_V + 1, :]
           for a in range(2)]
    o_ref[...] = jnp.concatenate(out, axis=0).T.astype(BF16)


def _mla_attn(q, k, vt, batch, seq):
    assert TQ == 2 * TK
    nq = seq // TQ
    return pl.pallas_call(
        functools.partial(_attn_kernel, tq=TQ, tk=TK),
        grid=(batch, MLA_HEADS // 2, nq),
        out_shape=jax.ShapeDtypeStruct((batch * seq, MLA_HEADS * MLA_V), BF16),
        in_specs=[pl.BlockSpec((TQ, 2 * HEAD_PAD), lambda b, h, i: (b * nq + i, h)),
                  pl.BlockSpec((seq, 2 * HEAD_PAD), lambda b, h, i: (b, h)),
                  pl.BlockSpec((2 * V_ROWS, seq), lambda b, h, i: (h, b))],
        out_specs=pl.BlockSpec((TQ, 2 * MLA_V), lambda b, h, i: (b * nq + i, h)),
        scratch_shapes=[pltpu.VMEM((2 * V_ROWS, TQ), F32)] + [pltpu.VMEM((8, TQ), F32)] * 3 + [
                        pltpu.VMEM((2, TK, TQ), F32), pltpu.VMEM((2, TK, TQ), F32),
                        pltpu.VMEM((2, TK, TQ), BF16), pltpu.VMEM((2, TK, TQ), BF16)],
        compiler_params=pltpu.CompilerParams(
            dimension_semantics=("arbitrary", "arbitrary", "arbitrary"), vmem_limit_bytes=VMEM_LIMIT),
        name="mla_attn",
    )(q, k, vt)


def _mem_kv_kernel(mem_ref, wk_ref, wv_ref, k_ref, v_ref):
    mb = mem_ref[...].astype(BF16)
    k_ref[...] = _dot(mb, wk_ref[...]).astype(BF16)
    v_ref[...] = _dot(mb, wv_ref[...]).astype(BF16)


def _mem_kv(mem2d, w_mk, w_mv, mem_len):
    rows, d = mem2d.shape
    blk = pl.BlockSpec((mem_len, d), lambda i: (i, 0))
    return pl.pallas_call(
        _mem_kv_kernel,
        grid=(rows // mem_len,),
        out_shape=(jax.ShapeDtypeStruct((rows, d), BF16),) * 2,
        in_specs=[blk, _const_spec(w_mk.shape), _const_spec(w_mv.shape)],
        out_specs=(blk, blk),
        compiler_params=pltpu.CompilerParams(dimension_semantics=("arbitrary",)),
        name="mem_kv",
    )(mem2d, w_mk, w_mv)


def _post_mixer_kernel(x_ref, o_ref, gm_ref, sg_ref, mk_ref, mv_ref,
                       w_mla_out_ref, w_o_ref, ln1_g_ref, ln1_b_ref,
                       w_mq_ref, w_mo_ref, ln2_g_ref, ln2_b_ref,
                       wr_hi_ref, wr_lo_ref, br_ref,
                       x2_ref, route_ref, idx_ref, counts_ref, run_ref, *, tm, alpha, mem_scale):
    i = pl.program_id(0)

    @pl.when(i == 0)
    def _():
        run_ref[...] = jnp.zeros_like(run_ref)

    y_mla = _dot(o_ref[...], w_mla_out_ref[...])
    merged = gm_ref[...].astype(F32) + sg_ref[...].astype(F32) * y_mla
    x1 = _layer_norm(alpha * x_ref[...] + _dot(merged.astype(BF16), w_o_ref[...]),
                     ln1_g_ref[...], ln1_b_ref[...])

    qm = (_dot(x1.astype(BF16), w_mq_ref[...]) * mem_scale).astype(BF16)
    hd = qm.shape[1] // MEM_HEADS
    heads = []
    for h in range(MEM_HEADS):
        cols = slice(h * hd, (h + 1) * hd)
        s = _dot_nt(qm[:, cols], mk_ref[:, cols])
        pr = jnp.exp(s - jnp.max(s, axis=1, keepdims=True))
        heads.append(_dot(pr.astype(BF16), mv_ref[:, cols]) / jnp.sum(pr, axis=1, keepdims=True))
    om = jnp.concatenate(heads, axis=1).astype(BF16)
    x2 = _layer_norm(alpha * x1 + _dot(om, w_mo_ref[...]), ln2_g_ref[...], ln2_b_ref[...])
    x2_ref[...] = x2

    x_hi = x2.astype(BF16)
    x_lo = (x2 - x_hi.astype(F32)).astype(BF16)
    logits = (_dot(x_hi, wr_hi_ref[...]) + _dot(x_lo, wr_hi_ref[...])
              + _dot(x_hi, wr_lo_ref[...]) + br_ref[...])
    lane = lax.broadcasted_iota(jnp.int32, (tm, LANES), 1).astype(F32)
    big = jnp.float32(1e9)

    def first_argmax(vals, vmax):
        return jnp.min(jnp.where(vals == vmax, lane, big), axis=1, keepdims=True)

    g_mask = (lane >= N_EXPERTS) & (lane < N_EXPERTS + N_GROUPS)
    lg = jnp.where(g_mask, logits, -jnp.inf)
    g_max = jnp.max(lg, axis=1, keepdims=True)
    g_sel = first_argmax(lg, g_max) - N_EXPERTS
    g_w = 1.0 / jnp.sum(jnp.where(g_mask, jnp.exp(logits - g_max), 0.0), axis=1, keepdims=True)
    in_group = jnp.floor(lane * (1.0 / EXPERTS_PER_GROUP)) == g_sel
    le = jnp.where(in_group, logits, -jnp.inf)
    v1 = jnp.max(le, axis=1, keepdims=True)
    e1 = first_argmax(le, v1)
    le2 = jnp.where(lane == e1, -jnp.inf, le)
    v2 = jnp.max(le2, axis=1, keepdims=True)
    e2 = first_argmax(le2, v2)
    t2 = jnp.exp(v2 - v1)
    w1 = (1.0 / (1.0 + t2)) * g_w
    w2 = (t2 / (1.0 + t2)) * g_w

    hit1 = lane == e1
    hit2 = lane == e2
    onehot = jnp.where(hit1 | hit2, 1.0, 0.0)
    r_i = lax.broadcasted_iota(jnp.int32, (tm, tm), 0)
    c_i = lax.broadcasted_iota(jnp.int32, (tm, tm), 1)
    tri = jnp.where(c_i < r_i, 1.0, 0.0).astype(BF16)
    before = _dot(tri, onehot.astype(BF16)) + run_ref[...]
    r1 = jnp.sum(jnp.where(hit1, before, 0.0), axis=1, keepdims=True)
    r2 = jnp.sum(jnp.where(hit2, before, 0.0), axis=1, keepdims=True)
    run_ref[...] = run_ref[...] + jnp.sum(onehot, axis=0, keepdims=True)
    counts_ref[...] = jnp.broadcast_to(run_ref[...], counts_ref.shape)

    packed = jnp.zeros((tm, LANES), F32)
    for idx, val in enumerate((e1, e2, w1, w2, r1, r2)):
        packed = jnp.where(lane == idx, val, packed)
    route_ref[...] = packed
    idx_ref[...] = packed.T[:8, :]


def _post_mixer(x2d, o, gm, sg, mk, mv, p, seq, mem_len):
    t, d = x2d.shape
    tm = TM_POST
    per_batch = seq // tm
    row = lambda n: pl.BlockSpec((tm, n), lambda i: (i, 0))
    memblk = pl.BlockSpec((mem_len, d), lambda i: (i // per_batch, 0))
    weights = [p["w_mla_out"], p["w_o"], p["ln1_g"], p["ln1_b"], p["w_mq"], p["w_mo"],
               p["ln2_g"], p["ln2_b"], p["wr_hi"], p["wr_lo"], p["br"]]
    return pl.pallas_call(
        functools.partial(_post_mixer_kernel, tm=tm, alpha=p["alpha"],
                          mem_scale=(d // MEM_HEADS) ** -0.5),
        grid=(t // tm,),
        out_shape=(jax.ShapeDtypeStruct((t, d), F32), jax.ShapeDtypeStruct((t, LANES), F32),
                   jax.ShapeDtypeStruct((t // tm, 8, tm), F32), jax.ShapeDtypeStruct((8, LANES), F32)),
        in_specs=[row(d), row(d), row(d), row(d), memblk, memblk]
                 + [_const_spec(w.shape) for w in weights],
        out_specs=(row(d), row(LANES), pl.BlockSpec((None, 8, tm), lambda i: (i, 0, 0)),
                   pl.BlockSpec((8, LANES), lambda i: (0, 0))),
        scratch_shapes=[pltpu.VMEM((1, LANES), F32)],
        compiler_params=pltpu.CompilerParams(dimension_semantics=("arbitrary",),
                                             vmem_limit_bytes=VMEM_LIMIT),
        name="post_mixer",
    )(x2d, o, gm, sg, mk, mv, *weights)


def _dispatch_kernel(last_ref, has_ref, dst_ref, x_hbm, xd_hbm, zbuf, zsem, sems, *, tm, nsteps, rows):
    i = pl.program_id(0)
    slot = i % 2

    @pl.when(i == 0)
    def _():
        zbuf[...] = jnp.zeros_like(zbuf)

        def zero_copy(e):
            start = pl.multiple_of(last_ref[e], rows)
            return pltpu.make_async_copy(zbuf, xd_hbm.at[pl.ds(start, rows)], zsem)

        for e in range(N_EXPERTS):
            @pl.when(has_ref[e] > 0)
            def _():
                zero_copy(e).start()

        for e in range(N_EXPERTS):
            @pl.when(has_ref[e] > 0)
            def _():
                zero_copy(e).wait()

    base = i * tm
    for k in range(2):
        for r in range(tm):
            pltpu.make_async_copy(x_hbm.at[pl.ds(base + r, 1)],
                                  xd_hbm.at[pl.ds(dst_ref[0, 0, k * tm + r], 1)], sems.at[slot]).start()

    def drain(s):
        pltpu.make_async_copy(x_hbm.at[pl.ds(0, 2 * tm)], xd_hbm.at[pl.ds(0, 2 * tm)], sems.at[s]).wait()

    @pl.when(i > 0)
    def _():
        drain(1 - slot)

    @pl.when(i == nsteps - 1)
    def _():
        drain(slot)


def _moe_dispatch(x2, dst3, last_start, has_rows, total_rows):
    t, d = x2.shape
    tm = TM_OUT
    nsteps = t // tm
    assert dst3.shape == (nsteps, 1, 2 * tm), "post_mixer and dispatch/combine tiles must coincide"
    grid_spec = pltpu.PrefetchScalarGridSpec(
        num_scalar_prefetch=2,
        grid=(nsteps,),
        in_specs=[pl.BlockSpec((1, 1, 2 * tm), lambda i, la, ha: (i, 0, 0), memory_space=pltpu.SMEM),
                  pl.BlockSpec(memory_space=pl.ANY)],
        out_specs=pl.BlockSpec(memory_space=pl.ANY),
        scratch_shapes=[pltpu.VMEM((MOE_ROWS, d), F32), pltpu.SemaphoreType.DMA(()),
                        pltpu.SemaphoreType.DMA((2,))],
    )
    return pl.pallas_call(
        functools.partial(_dispatch_kernel, tm=tm, nsteps=nsteps, rows=MOE_ROWS),
        grid_spec=grid_spec,
        out_shape=jax.ShapeDtypeStruct((total_rows, d), F32),
        compiler_params=pltpu.CompilerParams(dimension_semantics=("arbitrary",), has_side_effects=True),
        name="moe_dispatch",
    )(last_start, has_rows, dst3, x2)


def _experts_kernel(be_ref, nvalid_ref, xd_ref, wg_ref, wu_ref, wd_ref, yd_ref, wgu_bf, wd_bf):
    i = pl.program_id(0)
    changed = jnp.logical_or(i == 0, be_ref[i] != be_ref[jnp.maximum(i - 1, 0)])

    @pl.when(changed)
    def _():
        wgu_bf[:, :D_EXPERT] = wg_ref[...].astype(BF16)
        wgu_bf[:, D_EXPERT:] = wu_ref[...].astype(BF16)
        wd_bf[...] = wd_ref[...].astype(BF16)

    @pl.when(i < nvalid_ref[0])
    def _():
        gu = _dot(xd_ref[...].astype(BF16), wgu_bf[...])
        gate = gu[:, :D_EXPERT]
        hidden = (gate * jax.nn.sigmoid(gate) * gu[:, D_EXPERT:]).astype(BF16)
        yd_ref[...] = _dot(hidden, wd_bf[...])

    @pl.when(i >= nvalid_ref[0])
    def _():
        yd_ref[...] = jnp.zeros_like(yd_ref)


def _moe_experts(xd, block_e, nvalid, w_gate, w_up, w_down):
    total, d = xd.shape
    rows = MOE_ROWS
    nblocks = total // rows
    grid_spec = pltpu.PrefetchScalarGridSpec(
        num_scalar_prefetch=2,
        grid=(nblocks,),
        in_specs=[pl.BlockSpec((rows, d), lambda i, be, nv: (jnp.minimum(i, nv[0] - 1), 0)),
                  pl.BlockSpec((None, d, D_EXPERT), lambda i, be, nv: (be[i], 0, 0)),
                  pl.BlockSpec((None, d, D_EXPERT), lambda i, be, nv: (be[i], 0, 0)),
                  pl.BlockSpec((None, D_EXPERT, d), lambda i, be, nv: (be[i], 0, 0))],
        out_specs=pl.BlockSpec((rows, d), lambda i, be, nv: (i, 0)),
        scratch_shapes=[pltpu.VMEM((d, 2 * D_EXPERT), BF16), pltpu.VMEM((D_EXPERT, d), BF16)],
    )
    return pl.pallas_call(
        _experts_kernel,
        grid_spec=grid_spec,
        out_shape=jax.ShapeDtypeStruct((total, d), F32),
        compiler_params=pltpu.CompilerParams(dimension_semantics=("arbitrary",),
                                             vmem_limit_bytes=VMEM_LIMIT),
        name="moe_experts",
    )(block_e, nvalid, xd, w_gate, w_up, w_down)


def _combine_kernel(dst_cur_ref, dst_nxt_ref, x2_ref, route_ref, g_ref, b_ref, yd_hbm,
                    out_ref, ybuf, sems, *, tm, nsteps, alpha):
    i = pl.program_id(0)

    def issue(idx_ref, s):
        for r in range(2 * tm):
            pltpu.make_async_copy(yd_hbm.at[pl.ds(idx_ref[0, 0, r], 1)], ybuf.at[s, pl.ds(r, 1)],
                                  sems.at[s]).start()

    def wait(s):
        pltpu.make_async_copy(yd_hbm.at[pl.ds(0, 2 * tm)], ybuf.at[s], sems.at[s]).wait()

    @pl.when(i == 0)
    def _():
        issue(dst_cur_ref, 0)

    for s in range(2):
        @pl.when(i % 2 == s)
        def _():
            wait(s)
            issue(dst_nxt_ref, 1 - s)
            route = route_ref[...]
            y = ybuf[s, :tm, :] * route[:, 2:3] + ybuf[s, tm:, :] * route[:, 3:4]
            out_ref[...] = _layer_norm(alpha * x2_ref[...] + y, g_ref[...], b_ref[...])

            @pl.when(i == nsteps - 1)
            def _():
                wait(1 - s)


def _moe_combine(x2, route, dst3, yd, ln_g, ln_b, alpha):
    t, d = x2.shape
    tm = TM_OUT
    nsteps = t // tm
    smem_blk = lambda f: pl.BlockSpec((1, 1, 2 * tm), f, memory_space=pltpu.SMEM)
    row = lambda n: pl.BlockSpec((tm, n), lambda i: (i, 0))
    return pl.pallas_call(
        functools.partial(_combine_kernel, tm=tm, nsteps=nsteps, alpha=alpha),
        grid=(nsteps,),
        out_shape=jax.ShapeDtypeStruct((t, d), F32),
        in_specs=[smem_blk(lambda i: (i, 0, 0)),
                  smem_blk(lambda i: (jnp.minimum(i + 1, nsteps - 1), 0, 0)),
                  row(d), row(LANES), _const_spec(ln_g.shape), _const_spec(ln_b.shape),
                  pl.BlockSpec(memory_space=pl.ANY)],
        out_specs=row(d),
        scratch_shapes=[pltpu.VMEM((2, 2 * tm, d), F32), pltpu.SemaphoreType.DMA((2,))],
        compiler_params=pltpu.CompilerParams(dimension_semantics=("arbitrary",),
                                             vmem_limit_bytes=VMEM_LIMIT),
        name="moe_combine",
    )(dst3, dst3, x2, route, ln_g, ln_b, yd)


def _prep_layer(w_in, b_in, gm_ln_g, gm_ln_b, gm_w_s, gm_b_s, w_gm_out, mla_q_norm_g, mla_kv_norm_g,
                w_uq, w_uk, w_uv, w_mla_out, w_o, ln1_g, ln1_b, w_mq, w_mk, w_mv, w_mo, ln2_g, ln2_b,
                w_group_router, b_group_router, w_expert_router, b_expert_router, depth):
    d = w_in.shape[0]
    s_v = 2 * GM_WIDTH
    s_q = s_v + MLA_Q_LORA
    s_kv = s_q + MLA_KV_LORA
    s_r = s_kv + MLA_ROPE
    rowv = lambda a: a.reshape(1, -1).astype(F32)

    def lat_cols(a):
        z = jnp.zeros(a.shape[:-1] + (MLA_NOPE,), a.dtype)
        kr = a[..., s_kv:s_r]
        return jnp.concatenate([a[..., s_v:s_kv], z, kr, kr], axis=-1)

    wq3 = w_uq.reshape(MLA_Q_LORA, MLA_HEADS, MLA_NOPE + MLA_ROPE)
    wq_pad = jnp.concatenate([wq3, wq3[..., MLA_NOPE:]], axis=-1).reshape(MLA_Q_LORA, MLA_HEADS * HEAD_PAD)
    wk3 = w_uk.reshape(MLA_KV_LORA, MLA_HEADS, MLA_NOPE)
    wk_pad = jnp.pad(wk3, ((0, 0), (0, 0), (0, HEAD_PAD - MLA_NOPE))).reshape(MLA_KV_LORA, MLA_HEADS * HEAD_PAD)
    wv3 = w_uv.T.reshape(MLA_HEADS, MLA_V, MLA_KV_LORA)
    wv_pad = jnp.pad(wv3, ((0, 0), (0, V_ROWS - MLA_V), (0, 0))).reshape(MLA_HEADS * V_ROWS, MLA_KV_LORA)
    b_v = jnp.zeros((MLA_HEADS, V_ROWS, 1), F32).at[:, MLA_V].set(1.0).reshape(MLA_HEADS * V_ROWS, 1)
    w_r = jnp.zeros((d, LANES), F32)
    w_r = w_r.at[:, :N_EXPERTS].set(w_expert_router).at[:, N_EXPERTS:N_EXPERTS + N_GROUPS].set(w_group_router)
    b_r = jnp.zeros((LANES,), F32)
    b_r = b_r.at[:N_EXPERTS].set(b_expert_router).at[N_EXPERTS:N_EXPERTS + N_GROUPS].set(b_group_router)
    wr_hi = w_r.astype(BF16)
    return dict(
        w_uv=w_in[:, :s_v].astype(BF16), b_uv=rowv(b_in[:s_v]),
        w_gate=w_in[:, s_r:].astype(BF16), b_gate=rowv(b_in[s_r:]),
        w_lat=lat_cols(w_in).astype(BF16), b_lat=rowv(lat_cols(b_in)),
        gm_ln_g=rowv(gm_ln_g), gm_ln_b=rowv(gm_ln_b),
        ws=gm_w_s.astype(BF16), bs_full=jnp.repeat(gm_b_s.T, GM_CHUNK, axis=1).astype(F32),
        w_gm_out=w_gm_out.astype(BF16), qn_g=rowv(mla_q_norm_g), kvn_g=rowv(mla_kv_norm_g),
        w_uq=wq_pad.astype(BF16), w_uk=wk_pad.astype(BF16), w_uvv=wv_pad.astype(BF16), b_v=b_v,
        w_mla_out=w_mla_out.astype(BF16), w_o=w_o.astype(BF16), ln1_g=rowv(ln1_g), ln1_b=rowv(ln1_b),
        w_mq=w_mq.astype(BF16), w_mk=w_mk.astype(BF16), w_mv=w_mv.astype(BF16), w_mo=w_mo.astype(BF16),
        ln2_g=rowv(ln2_g), ln2_b=rowv(ln2_b),
        wr_hi=wr_hi, wr_lo=(w_r - wr_hi.astype(F32)).astype(BF16), br=rowv(b_r),
        alpha=(2 * depth) ** 0.25,
    )


def _dispatch_tables(idx, counts, t):
    rows = MOE_ROWS
    tiles, _, tm = idx.shape
    e = idx[:, 0:2, :].astype(jnp.int32)
    rank = idx[:, 4:6, :].astype(jnp.int32)
    cnt = counts[0, :N_EXPERTS].astype(jnp.int32)
    padded = (cnt + rows - 1) // rows * rows
    ends = jnp.cumsum(padded)
    starts = ends - padded
    ids = jnp.arange(N_EXPERTS, dtype=jnp.int32)[:, None, None, None]
    start_of = jnp.sum(jnp.where(e[None] == ids, starts[:, None, None, None], 0), axis=0)
    dst3 = (start_of + rank).reshape(tiles, 1, 2 * tm)
    total = 2 * t + N_EXPERTS * rows
    block_start = jnp.arange(total // rows, dtype=jnp.int32) * rows
    block_e = jnp.minimum(jnp.sum(ends[None, :] <= block_start[:, None], axis=1), N_EXPERTS - 1)
    nvalid = (ends[-1] // rows).reshape(1).astype(jnp.int32)
    last_start = jnp.maximum(ends - rows, 0).astype(jnp.int32)
    has_rows = (cnt > 0).astype(jnp.int32)
    return dst3, block_e.astype(jnp.int32), nvalid, last_start, has_rows, total


def _layer(x, mem, positions, depth, w_in, b_in, gm_ln_g, gm_ln_b, gm_w_s, gm_b_s, w_gm_out,
           mla_q_norm_g, mla_kv_norm_g, w_uq, w_uk, w_uv, w_mla_out, w_o, ln1_g, ln1_b,
           w_mq, w_mk, w_mv, w_mo, ln2_g, ln2_b,
           w_group_router, b_group_router, w_expert_router, b_expert_router,
           w_exp_gate, w_exp_up, w_exp_down, ln3_g, ln3_b):
    b, s, d = x.shape
    t = b * s
    mem_len = mem.shape[1]
    p = _prep_layer(w_in, b_in, gm_ln_g, gm_ln_b, gm_w_s, gm_b_s, w_gm_out, mla_q_norm_g, mla_kv_norm_g,
                    w_uq, w_uk, w_uv, w_mla_out, w_o, ln1_g, ln1_b, w_mq, w_mk, w_mv, w_mo, ln2_g, ln2_b,
                    w_group_router, b_group_router, w_expert_router, b_expert_router, depth)
    x2d = x.reshape(t, d)
    inv_freq = ROPE_THETA ** (-jnp.arange(ROPE_HALF, dtype=F32) / ROPE_HALF)
    cos, sin = _rope_tables(positions, inv_freq)
    gm, sg, q, k, v = _mixer_in(x2d, cos, sin, p)
    o = _mla_attn(q, k, v, b, s)
    mk, mv = _mem_kv(mem.reshape(b * mem_len, d), p["w_mk"], p["w_mv"], mem_len)
    x2, route, idx, counts = _post_mixer(x2d, o, gm, sg, mk, mv, p, s, mem_len)
    dst3, block_e, nvalid, last_start, has_rows, total = _dispatch_tables(idx, counts, t)
    xd = _moe_dispatch(x2, dst3, last_start, has_rows, total)
    yd = _moe_experts(xd, block_e, nvalid, w_exp_gate, w_exp_up, w_exp_down)
    out = _moe_combine(x2, route, dst3, yd, ln3_g.reshape(1, d), ln3_b.reshape(1, d), p["alpha"])
    return out.reshape(b, s, d)


def kernel(x, mem, positions, w_in, b_in, gm_ln_g, gm_ln_b, gm_w_s, gm_b_s, w_gm_out, mla_q_norm_g, mla_kv_norm_g, w_uq, w_uk, w_uv, w_mla_out, w_o, ln1_g, ln1_b, w_mq, w_mk, w_mv, w_mo, ln2_g, ln2_b, w_group_router, b_group_router, w_expert_router, b_expert_router, w_exp_gate, w_exp_up, w_exp_down, ln3_g, ln3_b):
    depth = w_in.shape[0]
    per_layer = (w_in, b_in, gm_ln_g, gm_ln_b, gm_w_s, gm_b_s, w_gm_out, mla_q_norm_g, mla_kv_norm_g,
                 w_uq, w_uk, w_uv, w_mla_out, w_o, ln1_g, ln1_b, w_mq, w_mk, w_mv, w_mo, ln2_g, ln2_b,
                 w_group_router, b_group_router, w_expert_router, b_expert_router,
                 w_exp_gate, w_exp_up, w_exp_down, ln3_g, ln3_b)
    h = x
    for l in range(depth):
        h = _layer(h, mem, positions, depth, *[w[l] for w in per_layer])
    return h
```

```python
import functools
import math

import jax
import jax.numpy as jnp
from jax import lax
from jax.experimental import pallas as pl
from jax.experimental.pallas import tpu as pltpu

F32 = jnp.float32
BF16 = jnp.bfloat16

GM_WIDTH = 1024
GM_GROUPS = 8
GM_CHUNK = 128
MLA_HEADS = 16
MLA_NOPE = 64
MLA_ROPE = 32
MLA_V = 64
MLA_Q_LORA = 384
MLA_KV_LORA = 256
ROPE_THETA = 10000.0
MEM_HEADS = 4
N_GROUPS = 8
EXPERTS_PER_GROUP = 8
N_EXPERTS = N_GROUPS * EXPERTS_PER_GROUP
D_EXPERT = 256
LN_EPS = 1e-5
RMS_EPS = 1e-6

LANES = 128
HEAD_PAD = 128
V_ROWS = 80
ROPE_HALF = MLA_ROPE // 2

TM_IN = 256
TQ = 512
TK = 256
SM_CHUNK = 16
TM_POST = 256
MOE_ROWS = 256
TM_OUT = 256
VMEM_LIMIT = 56 * 1024 * 1024


def _const_spec(shape):
    nd = len(shape)
    return pl.BlockSpec(shape, lambda *_: (0,) * nd, pipeline_mode=pl.Buffered(1))


def _layer_norm(x, g, b):
    mu = jnp.mean(x, axis=-1, keepdims=True)
    xc = x - mu
    var = jnp.mean(xc * xc, axis=-1, keepdims=True)
    return xc * lax.rsqrt(var + LN_EPS) * g + b


def _rms_norm(x, g):
    return x * lax.rsqrt(jnp.mean(x * x, axis=-1, keepdims=True) + RMS_EPS) * g


def _gelu(x):
    return x * (lax.erf(x * (1.0 / math.sqrt(2.0))) + 1.0) * 0.5


def _dot(a, b):
    return jnp.dot(a, b, preferred_element_type=F32)


def _dot_nt(a, b):
    return lax.dot_general(a, b, (((1,), (1,)), ((), ())), preferred_element_type=F32)


def _rope_kernel(inv_ref, pos_ref, cos_ref, sin_ref):
    p = pos_ref[...].astype(F32)
    for j in range(ROPE_HALF):
        ang = p * inv_ref[j]
        cos_ref[j] = jnp.cos(ang)
        sin_ref[j] = jnp.sin(ang)


def _rope_tables(positions, inv_freq):
    t = positions.size
    rows = t // LANES
    pos2 = positions.reshape(rows, LANES)
    cos, sin = pl.pallas_call(
        _rope_kernel,
        out_shape=(jax.ShapeDtypeStruct((ROPE_HALF, rows, LANES), F32),) * 2,
        in_specs=[pl.BlockSpec(memory_space=pltpu.SMEM),
                  pl.BlockSpec((rows, LANES), lambda: (0, 0))],
        out_specs=(pl.BlockSpec((ROPE_HALF, rows, LANES), lambda: (0, 0, 0)),) * 2,
        name="rope_tables",
    )(inv_freq, pos2)
    cos = jnp.tile(cos.reshape(ROPE_HALF, t).T, (1, LANES // ROPE_HALF))
    sin = jnp.tile(sin.reshape(ROPE_HALF, t).T, (1, LANES // ROPE_HALF))
    return cos, sin


def _mixer_in_kernel(x_ref, cos_ref, sin_ref, w_uv_ref, b_uv_ref, w_gate_ref, b_gate_ref,
                     w_lat_ref, b_lat_ref, ln_g_ref, ln_b_ref, ws_ref, bs_ref, w_gm_out_ref,
                     qn_g_ref, kvn_g_ref, w_uq_ref, w_uk_ref, w_uvv_ref, b_v_ref,
                     gm_ref, sg_ref, q_ref, k_ref, v_ref, mixed_ref, *, tm, scale):
    xb = x_ref[...].astype(BF16)

    uv = _gelu(_dot(xb, w_uv_ref[...]) + b_uv_ref[...])
    u = uv[:, :GM_WIDTH]
    vn = _layer_norm(uv[:, GM_WIDTH:], ln_g_ref[...], ln_b_ref[...]).astype(BF16)
    r = lax.broadcasted_iota(jnp.int32, (GM_CHUNK, GM_CHUNK), 0)
    c = lax.broadcasted_iota(jnp.int32, (GM_CHUNK, GM_CHUNK), 1)
    causal = c <= r
    for g in range(GM_GROUPS):
        w = jnp.where(causal, ws_ref[g], jnp.zeros_like(ws_ref[g]))
        for ch in range(tm // GM_CHUNK):
            rows = slice(ch * GM_CHUNK, (ch + 1) * GM_CHUNK)
            cols = slice(g * GM_CHUNK, (g + 1) * GM_CHUNK)
            mixed_ref[rows, cols] = _dot(w, vn[rows, cols]) + bs_ref[:, cols]
    gated = (u * mixed_ref[...]).astype(BF16)
    y_gm = _dot(gated, w_gm_out_ref[...])
    gates = _dot(xb, w_gate_ref[...]) + b_gate_ref[...]
    gm_ref[...] = (jax.nn.sigmoid(gates[:, :GM_WIDTH]) * y_gm).astype(BF16)
    sg_ref[...] = jax.nn.sigmoid(gates[:, GM_WIDTH:]).astype(BF16)

    lat = _dot(xb, w_lat_ref[...]) + b_lat_ref[...]
    cq = _rms_norm(lat[:, :MLA_Q_LORA], qn_g_ref[...]).astype(BF16)
    ckv = _rms_norm(lat[:, MLA_Q_LORA:MLA_Q_LORA + MLA_KV_LORA], kvn_g_ref[...]).astype(BF16)
    kr = lat[:, MLA_Q_LORA + MLA_KV_LORA:]

    lane = lax.broadcasted_iota(jnp.int32, (tm, HEAD_PAD), 1)
    cosv = cos_ref[...]
    sinv = sin_ref[...]
    in_rope = (lane >= MLA_NOPE) & (lane < MLA_NOPE + MLA_ROPE)
    cfac = jnp.where(lane < MLA_NOPE, 1.0, jnp.where(in_rope, cosv, 0.0))
    sfac = jnp.where(in_rope, jnp.where(lane < MLA_NOPE + ROPE_HALF, -sinv, sinv), 0.0)

    def rot(t):
        return t * cfac + pltpu.roll(t, HEAD_PAD - ROPE_HALF, 1) * sfac

    kr_rot = rot(kr)
    qa = _dot(cq, w_uq_ref[...])
    kn = _dot(ckv, w_uk_ref[...])
    for h in range(MLA_HEADS):
        cols = slice(h * HEAD_PAD, (h + 1) * HEAD_PAD)
        q_ref[:, cols] = (rot(qa[:, cols]) * scale).astype(BF16)
        k_ref[:, cols] = (kn[:, cols] + kr_rot).astype(BF16)
    v_ref[...] = (_dot_nt(w_uvv_ref[...], ckv) + b_v_ref[...]).astype(BF16)


def _mixer_in(x2d, cos, sin, p):
    t, d = x2d.shape
    tm = TM_IN
    scale = (MLA_NOPE + MLA_ROPE) ** -0.5 * math.log2(math.e)
    row = lambda n: pl.BlockSpec((tm, n), lambda i: (i, 0))
    weights = [p["w_uv"], p["b_uv"], p["w_gate"], p["b_gate"], p["w_lat"], p["b_lat"],
               p["gm_ln_g"], p["gm_ln_b"], p["ws"], p["bs_full"], p["w_gm_out"],
               p["qn_g"], p["kvn_g"], p["w_uq"], p["w_uk"], p["w_uvv"], p["b_v"]]
    qk_w = MLA_HEADS * HEAD_PAD
    v_w = MLA_HEADS * V_ROWS
    return pl.pallas_call(
        functools.partial(_mixer_in_kernel, tm=tm, scale=scale),
        grid=(t // tm,),
        out_shape=(jax.ShapeDtypeStruct((t, d), BF16), jax.ShapeDtypeStruct((t, d), BF16),
                   jax.ShapeDtypeStruct((t, qk_w), BF16), jax.ShapeDtypeStruct((t, qk_w), BF16),
                   jax.ShapeDtypeStruct((v_w, t), BF16)),
        in_specs=[row(d), row(LANES), row(LANES)] + [_const_spec(w.shape) for w in weights],
        out_specs=(row(d), row(d), row(qk_w), row(qk_w), pl.BlockSpec((v_w, tm), lambda i: (0, i))),
        scratch_shapes=[pltpu.VMEM((tm, GM_WIDTH), F32)],
        compiler_params=pltpu.CompilerParams(dimension_semantics=("arbitrary",),
                                             vmem_limit_bytes=VMEM_LIMIT),
        name="mixer_in",
    )(x2d, cos, sin, *weights)


def _attn_kernel(q_ref, k_ref, vt_ref, o_ref, acc_ref, m_ref, r0_ref, r1_ref, s0_ref, s1_ref, p0_ref, p1_ref,
                 *, tq, tk):
    i = pl.program_id(2)
    acc_ref[...] = jnp.zeros_like(acc_ref)
    m_ref[...] = jnp.full(m_ref.shape, -jnp.inf, F32)

    def scores(j, s_ref):
        kb = k_ref[pl.ds(pl.multiple_of(j * tk, tk), tk), :]
        for a in range(2):
            cols = slice(a * HEAD_PAD, (a + 1) * HEAD_PAD)
            s_ref[a] = _dot_nt(kb[:, cols], q_ref[:, cols])

    def softmax(j, s_ref, p_ref, r_ref, masked):
        nch = tk // SM_CHUNK
        if masked:
            qry = i * tq + lax.broadcasted_iota(jnp.int32, (SM_CHUNK, tq), 1)
            key0 = j * tk + lax.broadcasted_iota(jnp.int32, (SM_CHUNK, tq), 0)

        def chunk(a, c):
            s = s_ref[a, c * SM_CHUNK:(c + 1) * SM_CHUNK, :]
            if masked:
                s = jnp.where(key0 + c * SM_CHUNK <= qry, s, -jnp.inf)
            return s

        for a in range(2):
            m = m_ref[a:a + 1, :]
            cm = chunk(a, 0)
            for c in range(1, nch):
                cm = jnp.maximum(cm, chunk(a, c))
            m_new = jnp.maximum(m, jnp.max(cm, axis=0, keepdims=True))
            r_ref[a:a + 1, :] = jnp.exp2(m - m_new)
            m_ref[a:a + 1, :] = m_new
            mb = jnp.broadcast_to(m_new, (SM_CHUNK, tq))
            for c in range(nch):
                p_ref[a, c * SM_CHUNK:(c + 1) * SM_CHUNK, :] = jnp.exp2(chunk(a, c) - mb).astype(BF16)

    def accumulate(j, p_ref, r_ref):
        vb = vt_ref[:, pl.ds(pl.multiple_of(j * tk, tk), tk)]
        for a in range(2):
            rows = slice(a * V_ROWS, (a + 1) * V_ROWS)
            alpha = r_ref[a:a + 1, :]
            acc_ref[rows, :] = acc_ref[rows, :] * alpha + _dot(vb[rows, :], p_ref[a])

    def pair(base):
        scores(base + 1, s1_ref)
        softmax(base, s0_ref, p0_ref, r0_ref, False)
        accumulate(base, p0_ref, r0_ref)
        scores(base + 2, s0_ref)
        softmax(base + 1, s1_ref, p1_ref, r1_ref, False)
        accumulate(base + 1, p1_ref, r1_ref)

    def main(t, carry):
        pair(4 * t)
        pair(4 * t + 2)
        return carry

    scores(0, s0_ref)
    lax.fori_loop(0, i // 2, main, 0)

    @pl.when(i % 2 == 1)
    def _():
        pair(2 * i - 2)

    scores(2 * i + 1, s1_ref)
    softmax(2 * i, s0_ref, p0_ref, r0_ref, True)
    accumulate(2 * i, p0_ref, r0_ref)
    softmax(2 * i + 1, s1_ref, p1_ref, r1_ref, True)
    accumulate(2 * i + 1, p1_ref, r1_ref)
    out = [acc_ref[a * V_ROWS:a * V_ROWS + MLA_V, :] / acc_ref[a * V_ROWS + MLA_V:a * V_ROWS + MLA_V + 1, :]
           for a in range(2)]
    o_ref[...] = jnp.concatenate(out, axis=0).T.astype(BF16)


def _mla_attn(q, k, vt, batch, seq):
    assert TQ == 2 * TK
    nq = seq // TQ
    return pl.pallas_call(
        functools.partial(_attn_kernel, tq=TQ, tk=TK),
        grid=(batch, MLA_HEADS // 2, nq),
        out_shape=jax.ShapeDtypeStruct((batch * seq, MLA_HEADS * MLA_V), BF16),
        in_specs=[pl.BlockSpec((TQ, 2 * HEAD_PAD), lambda b, h, i: (b * nq + i, h)),
                  pl.BlockSpec((seq, 2 * HEAD_PAD), lambda b, h, i: (b, h)),
                  pl.BlockSpec((2 * V_ROWS, seq), lambda b, h, i: (h, b))],
        out_specs=pl.BlockSpec((TQ, 2 * MLA_V), lambda b, h, i: (b * nq + i, h)),
        scratch_shapes=[pltpu.VMEM((2 * V_ROWS, TQ), F32)] + [pltpu.VMEM((8, TQ), F32)] * 3 + [
                        pltpu.VMEM((2, TK, TQ), F32), pltpu.VMEM((2, TK, TQ), F32),
                        pltpu.VMEM((2, TK, TQ), BF16), pltpu.VMEM((2, TK, TQ), BF16)],
        compiler_params=pltpu.CompilerParams(
            dimension_semantics=("arbitrary", "arbitrary", "arbitrary"), vmem_limit_bytes=VMEM_LIMIT),
        name="mla_attn",
    )(q, k, vt)


def _mem_kv_kernel(mem_ref, wk_ref, wv_ref, k_ref, v_ref):
    mb = mem_ref[...].astype(BF16)
    k_ref[...] = _dot(mb, wk_ref[...]).astype(BF16)
    v_ref[...] = _dot(mb, wv_ref[...]).astype(BF16)


def _mem_kv(mem2d, w_mk, w_mv, mem_len):
    rows, d = mem2d.shape
    blk = pl.BlockSpec((mem_len, d), lambda i: (i, 0))
    return pl.pallas_call(
        _mem_kv_kernel,
        grid=(rows // mem_len,),
        out_shape=(jax.ShapeDtypeStruct((rows, d), BF16),) * 2,
        in_specs=[blk, _const_spec(w_mk.shape), _const_spec(w_mv.shape)],
        out_specs=(blk, blk),
        compiler_params=pltpu.CompilerParams(dimension_semantics=("arbitrary",)),
        name="mem_kv",
    )(mem2d, w_mk, w_mv)


def _post_mixer_kernel(x_ref, o_ref, gm_ref, sg_ref, mk_ref, mv_ref,
                       w_mla_out_ref, w_o_ref, ln1_g_ref, ln1_b_ref,
                       w_mq_ref, w_mo_ref, ln2_g_ref, ln2_b_ref,
                       wr_hi_ref, wr_lo_ref, br_ref,
                       x2_ref, route_ref, idx_ref, counts_ref, run_ref, *, tm, alpha, mem_scale):
    i = pl.program_id(0)

    @pl.when(i == 0)
    def _():
        run_ref[...] = jnp.zeros_like(run_ref)

    y_mla = _dot(o_ref[...], w_mla_out_ref[...])
    merged = gm_ref[...].astype(F32) + sg_ref[...].astype(F32) * y_mla
    x1 = _layer_norm(alpha * x_ref[...] + _dot(merged.astype(BF16), w_o_ref[...]),
                     ln1_g_ref[...], ln1_b_ref[...])

    qm = (_dot(x1.astype(BF16), w_mq_ref[...]) * mem_scale).astype(BF16)
    hd = qm.shape[1] // MEM_HEADS
    heads = []
    for h in range(MEM_HEADS):
        cols = slice(h * hd, (h + 1) * hd)
        s = _dot_nt(qm[:, cols], mk_ref[:, cols])
        pr = jnp.exp(s - jnp.max(s, axis=1, keepdims=True))
        heads.append(_dot(pr.astype(BF16), mv_ref[:, cols]) / jnp.sum(pr, axis=1, keepdims=True))
    om = jnp.concatenate(heads, axis=1).astype(BF16)
    x2 = _layer_norm(alpha * x1 + _dot(om, w_mo_ref[...]), ln2_g_ref[...], ln2_b_ref[...])
    x2_ref[...] = x2

    x_hi = x2.astype(BF16)
    x_lo = (x2 - x_hi.astype(F32)).astype(BF16)
    logits = (_dot(x_hi, wr_hi_ref[...]) + _dot(x_lo, wr_hi_ref[...])
              + _dot(x_hi, wr_lo_ref[...]) + br_ref[...])
    lane = lax.broadcasted_iota(jnp.int32, (tm, LANES), 1).astype(F32)
    big = jnp.float32(1e9)

    def first_argmax(vals, vmax):
        return jnp.min(jnp.where(vals == vmax, lane, big), axis=1, keepdims=True)

    g_mask = (lane >= N_EXPERTS) & (lane < N_EXPERTS + N_GROUPS)
    lg = jnp.where(g_mask, logits, -jnp.inf)
    g_max = jnp.max(lg, axis=1, keepdims=True)
    g_sel = first_argmax(lg, g_max) - N_EXPERTS
    g_w = 1.0 / jnp.sum(jnp.where(g_mask, jnp.exp(logits - g_max), 0.0), axis=1, keepdims=True)
    in_group = jnp.floor(lane * (1.0 / EXPERTS_PER_GROUP)) == g_sel
    le = jnp.where(in_group, logits, -jnp.inf)
    v1 = jnp.max(le, axis=1, keepdims=True)
    e1 = first_argmax(le, v1)
    le2 = jnp.where(lane == e1, -jnp.inf, le)
    v2 = jnp.max(le2, axis=1, keepdims=True)
    e2 = first_argmax(le2, v2)
    t2 = jnp.exp(v2 - v1)
    w1 = (1.0 / (1.0 + t2)) * g_w
    w2 = (t2 / (1.0 + t2)) * g_w

    hit1 = lane == e1
    hit2 = lane == e2
    onehot = jnp.where(hit1 | hit2, 1.0, 0.0)
    r_i = lax.broadcasted_iota(jnp.int32, (tm, tm), 0)
    c_i = lax.broadcasted_iota(jnp.int32, (tm, tm), 1)
    tri = jnp.where(c_i < r_i, 1.0, 0.0).astype(BF16)
    before = _dot(tri, onehot.astype(BF16)) + run_ref[...]
    r1 = jnp.sum(jnp.where(hit1, before, 0.0), axis=1, keepdims=True)
    r2 = jnp.sum(jnp.where(hit2, before, 0.0), axis=1, keepdims=True)
    run_ref[...] = run_ref[...] + jnp.sum(onehot, axis=0, keepdims=True)
    counts_ref[...] = jnp.broadcast_to(run_ref[...], counts_ref.shape)

    packed = jnp.zeros((tm, LANES), F32)
    for idx, val in enumerate((e1, e2, w1, w2, r1, r2)):
        packed = jnp.where(lane == idx, val, packed)
    route_ref[...] = packed
    idx_ref[...] = packed.T[:8, :]


def _post_mixer(x2d, o, gm, sg, mk, mv, p, seq, mem_len):
    t, d = x2d.shape
    tm = TM_POST
    per_batch = seq // tm
    row = lambda n: pl.BlockSpec((tm, n), lambda i: (i, 0))
    memblk = pl.BlockSpec((mem_len, d), lambda i: (i // per_batch, 0))
    weights = [p["w_mla_out"], p["w_o"], p["ln1_g"], p["ln1_b"], p["w_mq"], p["w_mo"],
               p["ln2_g"], p["ln2_b"], p["wr_hi"], p["wr_lo"], p["br"]]
    return pl.pallas_call(
        functools.partial(_post_mixer_kernel, tm=tm, alpha=p["alpha"],
                          mem_scale=(d // MEM_HEADS) ** -0.5),
        grid=(t // tm,),
        out_shape=(jax.ShapeDtypeStruct((t, d), F32), jax.ShapeDtypeStruct((t, LANES), F32),
                   jax.ShapeDtypeStruct((t // tm, 8, tm), F32), jax.ShapeDtypeStruct((8, LANES), F32)),
        in_specs=[row(d), row(d), row(d), row(d), memblk, memblk]
                 + [_const_spec(w.shape) for w in weights],
        out_specs=(row(d), row(LANES), pl.BlockSpec((None, 8, tm), lambda i: (i, 0, 0)),
                   pl.BlockSpec((8, LANES), lambda i: (0, 0))),
        scratch_shapes=[pltpu.VMEM((1, LANES), F32)],
        compiler_params=pltpu.CompilerParams(dimension_semantics=("arbitrary",),
                                             vmem_limit_bytes=VMEM_LIMIT),
        name="post_mixer",
    )(x2d, o, gm, sg, mk, mv, *weights)


def _dispatch_kernel(last_ref, has_ref, nvalid_ref, dst_ref, x_hbm, xd_hbm, zbuf, xbuf, zsem, lsems, ssems,
                     *, tm, nsteps, rows, nblocks):
    i = pl.program_id(0)

    def load(step, s):
        return pltpu.make_async_copy(x_hbm.at[pl.ds(pl.multiple_of(step * tm, tm), tm)], xbuf.at[s], lsems.at[s])

    def drain(s):
        pltpu.make_async_copy(xd_hbm.at[pl.ds(0, 2 * tm)], xd_hbm.at[pl.ds(0, 2 * tm)], ssems.at[s]).wait()

    @pl.when(i == 0)
    def _():
        load(0, 0).start()
        zbuf[...] = jnp.zeros_like(zbuf)

        def zero_copy(start):
            return pltpu.make_async_copy(zbuf, xd_hbm.at[pl.ds(pl.multiple_of(start, rows), rows)], zsem)

        for e in range(N_EXPERTS):
            @pl.when(has_ref[e] > 0)
            def _():
                zero_copy(last_ref[e]).start()

        def start_tail(blk, carry):
            zero_copy(blk * rows).start()
            return carry

        def wait_tail(blk, carry):
            zero_copy(blk * rows).wait()
            return carry

        lax.fori_loop(nvalid_ref[0], nblocks, start_tail, 0)
        for e in range(N_EXPERTS):
            @pl.when(has_ref[e] > 0)
            def _():
                zero_copy(last_ref[e]).wait()

        lax.fori_loop(nvalid_ref[0], nblocks, wait_tail, 0)

    for s in range(3):
        @pl.when(i % 3 == s)
        def _():
            nxt = (s + 1) % 3
            load(i, s).wait()

            @pl.when(i >= 2)
            def _():
                drain(nxt)

            @pl.when(i + 1 < nsteps)
            def _():
                load(i + 1, nxt).start()

            for k in range(2):
                for r in range(tm):
                    pltpu.make_async_copy(xbuf.at[s, pl.ds(r, 1)],
                                          xd_hbm.at[pl.ds(dst_ref[0, 0, k * tm + r], 1)], ssems.at[s]).start()

            @pl.when(i == nsteps - 1)
            def _():
                if nsteps >= 2:
                    drain((s + 2) % 3)
                drain(s)


def _moe_dispatch(x2, dst3, last_start, has_rows, nvalid, total_rows):
    t, d = x2.shape
    tm = TM_OUT
    nsteps = t // tm
    assert dst3.shape == (nsteps, 1, 2 * tm), "post_mixer and dispatch/combine tiles must coincide"
    grid_spec = pltpu.PrefetchScalarGridSpec(
        num_scalar_prefetch=3,
        grid=(nsteps,),
        in_specs=[pl.BlockSpec((1, 1, 2 * tm), lambda i, la, ha, nv: (i, 0, 0), memory_space=pltpu.SMEM),
                  pl.BlockSpec(memory_space=pl.ANY)],
        out_specs=pl.BlockSpec(memory_space=pl.ANY),
        scratch_shapes=[pltpu.VMEM((MOE_ROWS, d), F32), pltpu.VMEM((3, tm, d), F32),
                        pltpu.SemaphoreType.DMA(()), pltpu.SemaphoreType.DMA((3,)),
                        pltpu.SemaphoreType.DMA((3,))],
    )
    return pl.pallas_call(
        functools.partial(_dispatch_kernel, tm=tm, nsteps=nsteps, rows=MOE_ROWS,
                          nblocks=total_rows // MOE_ROWS),
        grid_spec=grid_spec,
        out_shape=jax.ShapeDtypeStruct((total_rows, d), F32),
        compiler_params=pltpu.CompilerParams(dimension_semantics=("arbitrary",), has_side_effects=True),
        name="moe_dispatch",
    )(last_start, has_rows, nvalid, dst3, x2)


def _experts_kernel(be_ref, nvalid_ref, xd_ref, wg_ref, wu_ref, wd_ref, yd_ref, wgu_bf, wd_bf):
    i = pl.program_id(0)
    changed = jnp.logical_or(i == 0, be_ref[i] != be_ref[jnp.maximum(i - 1, 0)])

    @pl.when(changed)
    def _():
        wgu_bf[:, :D_EXPERT] = wg_ref[...].astype(BF16)
        wgu_bf[:, D_EXPERT:] = wu_ref[...].astype(BF16)
        wd_bf[...] = wd_ref[...].astype(BF16)

    @pl.when(i < nvalid_ref[0])
    def _():
        gu = _dot(xd_ref[...].astype(BF16), wgu_bf[...])
        gate = gu[:, :D_EXPERT]
        hidden = (gate * jax.nn.sigmoid(gate) * gu[:, D_EXPERT:]).astype(BF16)
        yd_ref[...] = _dot(hidden, wd_bf[...])

    @pl.when(i >= nvalid_ref[0])
    def _():
        yd_ref[...] = jnp.zeros_like(yd_ref)


def _moe_experts(xd, block_e, nvalid, w_gate, w_up, w_down):
    total, d = xd.shape
    rows = MOE_ROWS
    nblocks = total // rows
    grid_spec = pltpu.PrefetchScalarGridSpec(
        num_scalar_prefetch=2,
        grid=(nblocks,),
        in_specs=[pl.BlockSpec((rows, d), lambda i, be, nv: (jnp.minimum(i, nv[0] - 1), 0)),
                  pl.BlockSpec((None, d, D_EXPERT), lambda i, be, nv: (be[i], 0, 0)),
                  pl.BlockSpec((None, d, D_EXPERT), lambda i, be, nv: (be[i], 0, 0)),
                  pl.BlockSpec((None, D_EXPERT, d), lambda i, be, nv: (be[i], 0, 0))],
        out_specs=pl.BlockSpec((rows, d), lambda i, be, nv: (i, 0)),
        scratch_shapes=[pltpu.VMEM((d, 2 * D_EXPERT), BF16), pltpu.VMEM((D_EXPERT, d), BF16)],
    )
    return pl.pallas_call(
        _experts_kernel,
        grid_spec=grid_spec,
        out_shape=jax.ShapeDtypeStruct((total, d), F32),
        compiler_params=pltpu.CompilerParams(dimension_semantics=("arbitrary",),
                                             vmem_limit_bytes=VMEM_LIMIT),
        name="moe_experts",
    )(block_e, nvalid, xd, w_gate, w_up, w_down)


def _combine_kernel(dst_cur_ref, dst_nxt_ref, x2_ref, route_ref, g_ref, b_ref, yd_hbm,
                    out_ref, ybuf, sems, *, tm, nsteps, alpha):
    i = pl.program_id(0)

    def issue(idx_ref, s):
        for r in range(2 * tm):
            pltpu.make_async_copy(yd_hbm.at[pl.ds(idx_ref[0, 0, r], 1)], ybuf.at[s, pl.ds(r, 1)],
                                  sems.at[s]).start()

    def wait(s):
        pltpu.make_async_copy(yd_hbm.at[pl.ds(0, 2 * tm)], ybuf.at[s], sems.at[s]).wait()

    @pl.when(i == 0)
    def _():
        issue(dst_cur_ref, 0)

    for s in range(2):
        @pl.when(i % 2 == s)
        def _():
            wait(s)
            issue(dst_nxt_ref, 1 - s)
            route = route_ref[...]
            y = ybuf[s, :tm, :] * route[:, 2:3] + ybuf[s, tm:, :] * route[:, 3:4]
            out_ref[...] = _layer_norm(alpha * x2_ref[...] + y, g_ref[...], b_ref[...])

            @pl.when(i == nsteps - 1)
            def _():
                wait(1 - s)


def _moe_combine(x2, route, dst3, yd, ln_g, ln_b, alpha):
    t, d = x2.shape
    tm = TM_OUT
    nsteps = t // tm
    smem_blk = lambda f: pl.BlockSpec((1, 1, 2 * tm), f, memory_space=pltpu.SMEM)
    row = lambda n: pl.BlockSpec((tm, n), lambda i: (i, 0))
    return pl.pallas_call(
        functools.partial(_combine_kernel, tm=tm, nsteps=nsteps, alpha=alpha),
        grid=(nsteps,),
        out_shape=jax.ShapeDtypeStruct((t, d), F32),
        in_specs=[smem_blk(lambda i: (i, 0, 0)),
                  smem_blk(lambda i: (jnp.minimum(i + 1, nsteps - 1), 0, 0)),
                  row(d), row(LANES), _const_spec(ln_g.shape), _const_spec(ln_b.shape),
                  pl.BlockSpec(memory_space=pl.ANY)],
        out_specs=row(d),
        scratch_shapes=[pltpu.VMEM((2, 2 * tm, d), F32), pltpu.SemaphoreType.DMA((2,))],
        compiler_params=pltpu.CompilerParams(dimension_semantics=("arbitrary",),
                                             vmem_limit_bytes=VMEM_LIMIT),
        name="moe_combine",
    )(dst3, dst3, x2, route, ln_g, ln_b, yd)


def _prep_layer(w_in, b_in, gm_ln_g, gm_ln_b, gm_w_s, gm_b_s, w_gm_out, mla_q_norm_g, mla_kv_norm_g,
                w_uq, w_uk, w_uv, w_mla_out, w_o, ln1_g, ln1_b, w_mq, w_mk, w_mv, w_mo, ln2_g, ln2_b,
                w_group_router, b_group_router, w_expert_router, b_expert_router, depth):
    d = w_in.shape[0]
    s_v = 2 * GM_WIDTH
    s_q = s_v + MLA_Q_LORA
    s_kv = s_q + MLA_KV_LORA
    s_r = s_kv + MLA_ROPE
    rowv = lambda a: a.reshape(1, -1).astype(F32)

    def lat_cols(a):
        z = jnp.zeros(a.shape[:-1] + (MLA_NOPE,), a.dtype)
        kr = a[..., s_kv:s_r]
        return jnp.concatenate([a[..., s_v:s_kv], z, kr, kr], axis=-1)

    wq3 = w_uq.reshape(MLA_Q_LORA, MLA_HEADS, MLA_NOPE + MLA_ROPE)
    wq_pad = jnp.concatenate([wq3, wq3[..., MLA_NOPE:]], axis=-1).reshape(MLA_Q_LORA, MLA_HEADS * HEAD_PAD)
    wk3 = w_uk.reshape(MLA_KV_LORA, MLA_HEADS, MLA_NOPE)
    wk_pad = jnp.pad(wk3, ((0, 0), (0, 0), (0, HEAD_PAD - MLA_NOPE))).reshape(MLA_KV_LORA, MLA_HEADS * HEAD_PAD)
    wv3 = w_uv.T.reshape(MLA_HEADS, MLA_V, MLA_KV_LORA)
    wv_pad = jnp.pad(wv3, ((0, 0), (0, V_ROWS - MLA_V), (0, 0))).reshape(MLA_HEADS * V_ROWS, MLA_KV_LORA)
    b_v = jnp.zeros((MLA_HEADS, V_ROWS, 1), F32).at[:, MLA_V].set(1.0).reshape(MLA_HEADS * V_ROWS, 1)
    w_r = jnp.zeros((d, LANES), F32)
    w_r = w_r.at[:, :N_EXPERTS].set(w_expert_router).at[:, N_EXPERTS:N_EXPERTS + N_GROUPS].set(w_group_router)
    b_r = jnp.zeros((LANES,), F32)
    b_r = b_r.at[:N_EXPERTS].set(b_expert_router).at[N_EXPERTS:N_EXPERTS + N_GROUPS].set(b_group_router)
    wr_hi = w_r.astype(BF16)
    return dict(
        w_uv=w_in[:, :s_v].astype(BF16), b_uv=rowv(b_in[:s_v]),
        w_gate=w_in[:, s_r:].astype(BF16), b_gate=rowv(b_in[s_r:]),
        w_lat=lat_cols(w_in).astype(BF16), b_lat=rowv(lat_cols(b_in)),
        gm_ln_g=rowv(gm_ln_g), gm_ln_b=rowv(gm_ln_b),
        ws=gm_w_s.astype(BF16), bs_full=jnp.repeat(gm_b_s.T, GM_CHUNK, axis=1).astype(F32),
        w_gm_out=w_gm_out.astype(BF16), qn_g=rowv(mla_q_norm_g), kvn_g=rowv(mla_kv_norm_g),
        w_uq=wq_pad.astype(BF16), w_uk=wk_pad.astype(BF16), w_uvv=wv_pad.astype(BF16), b_v=b_v,
        w_mla_out=w_mla_out.astype(BF16), w_o=w_o.astype(BF16), ln1_g=rowv(ln1_g), ln1_b=rowv(ln1_b),
        w_mq=w_mq.astype(BF16), w_mk=w_mk.astype(BF16), w_mv=w_mv.astype(BF16), w_mo=w_mo.astype(BF16),
        ln2_g=rowv(ln2_g), ln2_b=rowv(ln2_b),
        wr_hi=wr_hi, wr_lo=(w_r - wr_hi.astype(F32)).astype(BF16), br=rowv(b_r),
        alpha=(2 * depth) ** 0.25,
    )


def _dispatch_tables(idx, counts, t):
    rows = MOE_ROWS
    tiles, _, tm = idx.shape
    e = idx[:, 0:2, :].astype(jnp.int32)
    rank = idx[:, 4:6, :].astype(jnp.int32)
    cnt = counts[0, :N_EXPERTS].astype(jnp.int32)
    padded = (cnt + rows - 1) // rows * rows
    ends = jnp.cumsum(padded)
    starts = ends - padded
    ids = jnp.arange(N_EXPERTS, dtype=jnp.int32)[:, None, None, None]
    start_of = jnp.sum(jnp.where(e[None] == ids, starts[:, None, None, None], 0), axis=0)
    dst3 = (start_of + rank).reshape(tiles, 1, 2 * tm)
    total = 2 * t + N_EXPERTS * rows
    block_start = jnp.arange(total // rows, dtype=jnp.int32) * rows
    block_e = jnp.minimum(jnp.sum(ends[None, :] <= block_start[:, None], axis=1), N_EXPERTS - 1)
    nvalid = (ends[-1] // rows).reshape(1).astype(jnp.int32)
    last_start = jnp.maximum(ends - rows, 0).astype(jnp.int32)
    has_rows = (cnt > 0).astype(jnp.int32)
    return dst3, block_e.astype(jnp.int32), nvalid, last_start, has_rows, total


def _layer(x, mem, positions, depth, w_in, b_in, gm_ln_g, gm_ln_b, gm_w_s, gm_b_s, w_gm_out,
           mla_q_norm_g, mla_kv_norm_g, w_uq, w_uk, w_uv, w_mla_out, w_o, ln1_g, ln1_b,
           w_mq, w_mk, w_mv, w_mo, ln2_g, ln2_b,
           w_group_router, b_group_router, w_expert_router, b_expert_router,
           w_exp_gate, w_exp_up, w_exp_down, ln3_g, ln3_b):
    b, s, d = x.shape
    t = b * s
    mem_len = mem.shape[1]
    p = _prep_layer(w_in, b_in, gm_ln_g, gm_ln_b, gm_w_s, gm_b_s, w_gm_out, mla_q_norm_g, mla_kv_norm_g,
                    w_uq, w_uk, w_uv, w_mla_out, w_o, ln1_g, ln1_b, w_mq, w_mk, w_mv, w_mo, ln2_g, ln2_b,
                    w_group_router, b_group_router, w_expert_router, b_expert_router, depth)
    x2d = x.reshape(t, d)
    inv_freq = ROPE_THETA ** (-jnp.arange(ROPE_HALF, dtype=F32) / ROPE_HALF)
    cos, sin = _rope_tables(positions, inv_freq)
    gm, sg, q, k, v = _mixer_in(x2d, cos, sin, p)
    o = _mla_attn(q, k, v, b, s)
    mk, mv = _mem_kv(mem.reshape(b * mem_len, d), p["w_mk"], p["w_mv"], mem_len)
    x2, route, idx, counts = _post_mixer(x2d, o, gm, sg, mk, mv, p, s, mem_len)
    dst3, block_e, nvalid, last_start, has_rows, total = _dispatch_tables(idx, counts, t)
    xd = _moe_dispatch(x2, dst3, last_start, has_rows, nvalid, total)
    yd = _moe_experts(xd, block_e, nvalid, w_exp_gate, w_exp_up, w_exp_down)
    out = _moe_combine(x2, route, dst3, yd, ln3_g.reshape(1, d), ln3_b.reshape(1, d), p["alpha"])
    return out.reshape(b, s, d)


def kernel(x, mem, positions, w_in, b_in, gm_ln_g, gm_ln_b, gm_w_s, gm_b_s, w_gm_out, mla_q_norm_g, mla_kv_norm_g, w_uq, w_uk, w_uv, w_mla_out, w_o, ln1_g, ln1_b, w_mq, w_mk, w_mv, w_mo, ln2_g, ln2_b, w_group_router, b_group_router, w_expert_router, b_expert_router, w_exp_gate, w_exp_up, w_exp_down, ln3_g, ln3_b):
    depth = w_in.shape[0]
    per_layer = (w_in, b_in, gm_ln_g, gm_ln_b, gm_w_s, gm_b_s, w_gm_out, mla_q_norm_g, mla_kv_norm_g,
                 w_uq, w_uk, w_uv, w_mla_out, w_o, ln1_g, ln1_b, w_mq, w_mk, w_mv, w_mo, ln2_g, ln2_b,
                 w_group_router, b_group_router, w_expert_router, b_expert_router,
                 w_exp_gate, w_exp_up, w_exp_down, ln3_g, ln3_b)
    h = x
    for l in range(depth):
        h = _layer(h, mem, positions, depth, *[w[l] for w in per_layer])
    return h
```

```python
import functools
import math

import jax
import jax.numpy as jnp
from jax import lax
from jax.experimental import pallas as pl
from jax.experimental.pallas import tpu as pltpu

F32 = jnp.float32
BF16 = jnp.bfloat16

GM_WIDTH = 1024
GM_GROUPS = 8
GM_CHUNK = 128
MLA_HEADS = 16
MLA_NOPE = 64
MLA_ROPE = 32
MLA_V = 64
MLA_Q_LORA = 384
MLA_KV_LORA = 256
ROPE_THETA = 10000.0
MEM_HEADS = 4
N_GROUPS = 8
EXPERTS_PER_GROUP = 8
N_EXPERTS = N_GROUPS * EXPERTS_PER_GROUP
D_EXPERT = 256
LN_EPS = 1e-5
RMS_EPS = 1e-6

LANES = 128
HEAD_PAD = 128
V_ROWS = 80
ROPE_HALF = MLA_ROPE // 2

TM_IN = 512
TM_SUB = 256
TQ = 512
TK = 256
SM_CHUNK = 16
TM_POST = 1024
MOE_ROWS = 256
TM_OUT = 256
VMEM_LIMIT = 56 * 1024 * 1024


def _const_spec(shape):
    nd = len(shape)
    return pl.BlockSpec(shape, lambda *_: (0,) * nd, pipeline_mode=pl.Buffered(1))


def _layer_norm(x, g, b):
    mu = jnp.mean(x, axis=-1, keepdims=True)
    xc = x - mu
    var = jnp.mean(xc * xc, axis=-1, keepdims=True)
    return xc * lax.rsqrt(var + LN_EPS) * g + b


def _rms_norm(x, g):
    return x * lax.rsqrt(jnp.mean(x * x, axis=-1, keepdims=True) + RMS_EPS) * g


def _gelu(x):
    return x * (lax.erf(x * (1.0 / math.sqrt(2.0))) + 1.0) * 0.5


def _dot(a, b):
    return jnp.dot(a, b, preferred_element_type=F32)


def _dot_nt(a, b):
    return lax.dot_general(a, b, (((1,), (1,)), ((), ())), preferred_element_type=F32)


def _rope_kernel(inv_ref, pos_ref, cos_ref, sin_ref):
    p = pos_ref[...].astype(F32)
    for j in range(ROPE_HALF):
        ang = p * inv_ref[j]
        cos_ref[j] = jnp.cos(ang)
        sin_ref[j] = jnp.sin(ang)


def _rope_tables(positions, inv_freq):
    t = positions.size
    rows = t // LANES
    pos2 = positions.reshape(rows, LANES)
    cos, sin = pl.pallas_call(
        _rope_kernel,
        out_shape=(jax.ShapeDtypeStruct((ROPE_HALF, rows, LANES), F32),) * 2,
        in_specs=[pl.BlockSpec(memory_space=pltpu.SMEM),
                  pl.BlockSpec((rows, LANES), lambda: (0, 0))],
        out_specs=(pl.BlockSpec((ROPE_HALF, rows, LANES), lambda: (0, 0, 0)),) * 2,
        name="rope_tables",
    )(inv_freq, pos2)
    cos = jnp.tile(cos.reshape(ROPE_HALF, t).T, (1, LANES // ROPE_HALF))
    sin = jnp.tile(sin.reshape(ROPE_HALF, t).T, (1, LANES // ROPE_HALF))
    return cos, sin


def _mixer_in_kernel(x_ref, cos_ref, sin_ref, w_uv_ref, b_uv_ref, w_gate_ref, b_gate_ref,
                     w_lat_ref, b_lat_ref, ln_g_ref, ln_b_ref, ws_ref, bs_ref, w_gm_out_ref,
                     qn_g_ref, kvn_g_ref, w_uq_ref, w_uk_ref, w_uvv_ref, b_v_ref,
                     gm_ref, sg_ref, q_ref, k_ref, v_ref, mixed_ref, *, tm, sub, scale):
    parts = range(tm // sub)
    rows = [slice(part * sub, (part + 1) * sub) for part in parts]
    xb = [x_ref[r, :].astype(BF16) for r in rows]

    uv = [_gelu(_dot(x, w_uv_ref[...]) + b_uv_ref[...]) for x in xb]
    vn = [_layer_norm(t[:, GM_WIDTH:], ln_g_ref[...], ln_b_ref[...]).astype(BF16) for t in uv]
    r_i = lax.broadcasted_iota(jnp.int32, (GM_CHUNK, GM_CHUNK), 0)
    c_i = lax.broadcasted_iota(jnp.int32, (GM_CHUNK, GM_CHUNK), 1)
    causal = c_i <= r_i
    for g in range(GM_GROUPS):
        w = jnp.where(causal, ws_ref[g], jnp.zeros_like(ws_ref[g]))
        cols = slice(g * GM_CHUNK, (g + 1) * GM_CHUNK)
        for part in parts:
            for ch in range(sub // GM_CHUNK):
                local = slice(ch * GM_CHUNK, (ch + 1) * GM_CHUNK)
                dest = slice(part * sub + ch * GM_CHUNK, part * sub + (ch + 1) * GM_CHUNK)
                mixed_ref[dest, cols] = _dot(w, vn[part][local, cols]) + bs_ref[:, cols]
    gated = [(t[:, :GM_WIDTH] * mixed_ref[r, :]).astype(BF16) for t, r in zip(uv, rows)]
    y_gm = [_dot(t, w_gm_out_ref[...]) for t in gated]
    gates = [_dot(x, w_gate_ref[...]) + b_gate_ref[...] for x in xb]
    for r, gt, y in zip(rows, gates, y_gm):
        gm_ref[r, :] = (jax.nn.sigmoid(gt[:, :GM_WIDTH]) * y).astype(BF16)
        sg_ref[r, :] = jax.nn.sigmoid(gt[:, GM_WIDTH:]).astype(BF16)

    lat = [_dot(x, w_lat_ref[...]) + b_lat_ref[...] for x in xb]
    cq = [_rms_norm(t[:, :MLA_Q_LORA], qn_g_ref[...]).astype(BF16) for t in lat]
    ckv = [_rms_norm(t[:, MLA_Q_LORA:MLA_Q_LORA + MLA_KV_LORA], kvn_g_ref[...]).astype(BF16) for t in lat]
    qa = [_dot(t, w_uq_ref[...]) for t in cq]
    kn = [_dot(t, w_uk_ref[...]) for t in ckv]

    lane = lax.broadcasted_iota(jnp.int32, (sub, HEAD_PAD), 1)
    in_rope = (lane >= MLA_NOPE) & (lane < MLA_NOPE + MLA_ROPE)
    for part in parts:
        r = rows[part]
        cosv = cos_ref[r, :]
        sinv = sin_ref[r, :]
        cfac = jnp.where(lane < MLA_NOPE, 1.0, jnp.where(in_rope, cosv, 0.0))
        sfac = jnp.where(in_rope, jnp.where(lane < MLA_NOPE + ROPE_HALF, -sinv, sinv), 0.0)

        def rot(t, cfac=cfac, sfac=sfac):
            return t * cfac + pltpu.roll(t, HEAD_PAD - ROPE_HALF, 1) * sfac

        kr_rot = rot(lat[part][:, MLA_Q_LORA + MLA_KV_LORA:])
        for h in range(MLA_HEADS):
            cols = slice(h * HEAD_PAD, (h + 1) * HEAD_PAD)
            q_ref[r, cols] = (rot(qa[part][:, cols]) * scale).astype(BF16)
            k_ref[r, cols] = (kn[part][:, cols] + kr_rot).astype(BF16)
        v_ref[:, r] = (_dot_nt(w_uvv_ref[...], ckv[part]) + b_v_ref[...]).astype(BF16)


def _mixer_in(x2d, cos, sin, p):
    t, d = x2d.shape
    tm = TM_IN
    scale = (MLA_NOPE + MLA_ROPE) ** -0.5 * math.log2(math.e)
    row = lambda n: pl.BlockSpec((tm, n), lambda i: (i, 0))
    weights = [p["w_uv"], p["b_uv"], p["w_gate"], p["b_gate"], p["w_lat"], p["b_lat"],
               p["gm_ln_g"], p["gm_ln_b"], p["ws"], p["bs_full"], p["w_gm_out"],
               p["qn_g"], p["kvn_g"], p["w_uq"], p["w_uk"], p["w_uvv"], p["b_v"]]
    qk_w = MLA_HEADS * HEAD_PAD
    v_w = MLA_HEADS * V_ROWS
    return pl.pallas_call(
        functools.partial(_mixer_in_kernel, tm=tm, sub=TM_SUB, scale=scale),
        grid=(t // tm,),
        out_shape=(jax.ShapeDtypeStruct((t, d), BF16), jax.ShapeDtypeStruct((t, d), BF16),
                   jax.ShapeDtypeStruct((t, qk_w), BF16), jax.ShapeDtypeStruct((t, qk_w), BF16),
                   jax.ShapeDtypeStruct((v_w, t), BF16)),
        in_specs=[row(d), row(LANES), row(LANES)] + [_const_spec(w.shape) for w in weights],
        out_specs=(row(d), row(d), row(qk_w), row(qk_w), pl.BlockSpec((v_w, tm), lambda i: (0, i))),
        scratch_shapes=[pltpu.VMEM((tm, GM_WIDTH), F32)],
        compiler_params=pltpu.CompilerParams(dimension_semantics=("arbitrary",),
                                             vmem_limit_bytes=VMEM_LIMIT),
        name="mixer_in",
    )(x2d, cos, sin, *weights)


def _attn_kernel(q_ref, k_ref, vt_ref, o_ref, acc_ref, m_ref, r0_ref, r1_ref, s0_ref, s1_ref, p0_ref, p1_ref,
                 *, tq, tk):
    i = pl.program_id(2)
    acc_ref[...] = jnp.zeros_like(acc_ref)
    m_ref[...] = jnp.full(m_ref.shape, -jnp.inf, F32)

    every = slice(0, tq)
    lower = slice(0, tk)
    upper = slice(tk, tq)

    def scores(j, s_ref, qs=every):
        kb = k_ref[pl.ds(pl.multiple_of(j * tk, tk), tk), :]
        for a in range(2):
            cols = slice(a * HEAD_PAD, (a + 1) * HEAD_PAD)
            s_ref[a, :, qs] = _dot_nt(kb[:, cols], q_ref[qs, cols])

    def softmax(s_ref, p_ref, r_ref, qs=every, diagonal=False):
        nch = tk // SM_CHUNK
        width = qs.stop - qs.start
        if diagonal:
            q_loc = lax.broadcasted_iota(jnp.int32, (SM_CHUNK, width), 1)
            k_loc = lax.broadcasted_iota(jnp.int32, (SM_CHUNK, width), 0)

        def chunk(a, c):
            s = s_ref[a, c * SM_CHUNK:(c + 1) * SM_CHUNK, qs]
            if diagonal:
                s = jnp.where(k_loc + c * SM_CHUNK <= q_loc, s, -jnp.inf)
            return s

        for a in range(2):
            m = m_ref[a:a + 1, qs]
            cm = chunk(a, 0)
            for c in range(1, nch):
                cm = jnp.maximum(cm, chunk(a, c))
            m_new = jnp.maximum(m, jnp.max(cm, axis=0, keepdims=True))
            r_ref[a:a + 1, qs] = jnp.exp2(m - m_new)
            m_ref[a:a + 1, qs] = m_new
            mb = jnp.broadcast_to(m_new, (SM_CHUNK, width))
            for c in range(nch):
                p_ref[a, c * SM_CHUNK:(c + 1) * SM_CHUNK, qs] = jnp.exp2(chunk(a, c) - mb).astype(BF16)

    def accumulate(j, p_ref, r_ref, qs=every):
        vb = vt_ref[:, pl.ds(pl.multiple_of(j * tk, tk), tk)]
        for a in range(2):
            rows = slice(a * V_ROWS, (a + 1) * V_ROWS)
            alpha = r_ref[a:a + 1, qs]
            acc_ref[rows, qs] = acc_ref[rows, qs] * alpha + _dot(vb[rows, :], p_ref[a, :, qs])

    def pair(base):
        scores(base + 1, s1_ref)
        softmax(s0_ref, p0_ref, r0_ref)
        accumulate(base, p0_ref, r0_ref)
        scores(base + 2, s0_ref)
        softmax(s1_ref, p1_ref, r1_ref)
        accumulate(base + 1, p1_ref, r1_ref)

    def main(t, carry):
        pair(4 * t)
        pair(4 * t + 2)
        return carry

    scores(0, s0_ref)
    lax.fori_loop(0, i // 2, main, 0)

    @pl.when(i % 2 == 1)
    def _():
        pair(2 * i - 2)

    scores(2 * i + 1, s1_ref, upper)
    softmax(s0_ref, p0_ref, r0_ref, lower, diagonal=True)
    softmax(s0_ref, p0_ref, r0_ref, upper)
    accumulate(2 * i, p0_ref, r0_ref)
    softmax(s1_ref, p1_ref, r1_ref, upper, diagonal=True)
    accumulate(2 * i + 1, p1_ref, r1_ref, upper)
    out = [acc_ref[a * V_ROWS:a * V_ROWS + MLA_V, :] / acc_ref[a * V_ROWS + MLA_V:a * V_ROWS + MLA_V + 1, :]
           for a in range(2)]
    o_ref[...] = jnp.concatenate(out, axis=0).T.astype(BF16)


def _mla_attn(q, k, vt, batch, seq):
    assert TQ == 2 * TK
    nq = seq // TQ
    return pl.pallas_call(
        functools.partial(_attn_kernel, tq=TQ, tk=TK),
        grid=(batch, MLA_HEADS // 2, nq),
        out_shape=jax.ShapeDtypeStruct((batch * seq, MLA_HEADS * MLA_V), BF16),
        in_specs=[pl.BlockSpec((TQ, 2 * HEAD_PAD), lambda b, h, i: (b * nq + i, h)),
                  pl.BlockSpec((seq, 2 * HEAD_PAD), lambda b, h, i: (b, h)),
                  pl.BlockSpec((2 * V_ROWS, seq), lambda b, h, i: (h, b))],
        out_specs=pl.BlockSpec((TQ, 2 * MLA_V), lambda b, h, i: (b * nq + i, h)),
        scratch_shapes=[pltpu.VMEM((2 * V_ROWS, TQ), F32)] + [pltpu.VMEM((8, TQ), F32)] * 3 + [
                        pltpu.VMEM((2, TK, TQ), F32), pltpu.VMEM((2, TK, TQ), F32),
                        pltpu.VMEM((2, TK, TQ), BF16), pltpu.VMEM((2, TK, TQ), BF16)],
        compiler_params=pltpu.CompilerParams(
            dimension_semantics=("arbitrary", "arbitrary", "arbitrary"), vmem_limit_bytes=VMEM_LIMIT),
        name="mla_attn",
    )(q, k, vt)


def _mem_kv_kernel(mem_ref, wk_ref, wv_ref, k_ref, v_ref):
    mb = mem_ref[...].astype(BF16)
    k_ref[...] = _dot(mb, wk_ref[...]).astype(BF16)
    v_ref[...] = _dot(mb, wv_ref[...]).astype(BF16)


def _mem_kv(mem2d, w_mk, w_mv, mem_len):
    rows, d = mem2d.shape
    blk = pl.BlockSpec((mem_len, d), lambda i: (i, 0))
    return pl.pallas_call(
        _mem_kv_kernel,
        grid=(rows // mem_len,),
        out_shape=(jax.ShapeDtypeStruct((rows, d), BF16),) * 2,
        in_specs=[blk, _const_spec(w_mk.shape), _const_spec(w_mv.shape)],
        out_specs=(blk, blk),
        compiler_params=pltpu.CompilerParams(dimension_semantics=("arbitrary",)),
        name="mem_kv",
    )(mem2d, w_mk, w_mv)


def _post_mixer_kernel(x_ref, o_ref, gm_ref, sg_ref, mk_ref, mv_ref,
                       w_mla_out_ref, w_o_ref, ln1_g_ref, ln1_b_ref,
                       w_mq_ref, w_mo_ref, ln2_g_ref, ln2_b_ref,
                       wr_hi_ref, wr_lo_ref, br_ref,
                       x2_ref, route_ref, idx_ref, counts_ref, run_ref, *, tm, sub, alpha, mem_scale):
    i = pl.program_id(0)

    @pl.when(i == 0)
    def _():
        run_ref[...] = jnp.zeros_like(run_ref)

    parts = range(tm // sub)
    rows = [slice(part * sub, (part + 1) * sub) for part in parts]
    y_mla = [_dot(o_ref[r, :], w_mla_out_ref[...]) for r in rows]
    merged = [(gm_ref[r, :].astype(F32) + sg_ref[r, :].astype(F32) * y).astype(BF16) for r, y in zip(rows, y_mla)]
    mixed = [_dot(m, w_o_ref[...]) for m in merged]
    x1 = [_layer_norm(alpha * x_ref[r, :] + h, ln1_g_ref[...], ln1_b_ref[...]) for r, h in zip(rows, mixed)]

    qm = [(_dot(x.astype(BF16), w_mq_ref[...]) * mem_scale).astype(BF16) for x in x1]
    hd = mk_ref.shape[1] // MEM_HEADS
    heads = [[] for _ in parts]
    for h in range(MEM_HEADS):
        cols = slice(h * hd, (h + 1) * hd)
        s = [_dot_nt(q[:, cols], mk_ref[:, cols]) for q in qm]
        pr = [jnp.exp(v - jnp.max(v, axis=1, keepdims=True)) for v in s]
        for part in parts:
            heads[part].append(_dot(pr[part].astype(BF16), mv_ref[:, cols])
                               / jnp.sum(pr[part], axis=1, keepdims=True))
    om = [jnp.concatenate(hs, axis=1).astype(BF16) for hs in heads]
    mem_out = [_dot(v, w_mo_ref[...]) for v in om]
    x2 = [_layer_norm(alpha * x + h, ln2_g_ref[...], ln2_b_ref[...]) for x, h in zip(x1, mem_out)]
    for r, v in zip(rows, x2):
        x2_ref[r, :] = v

    x_hi = [v.astype(BF16) for v in x2]
    x_lo = [(v - hi.astype(F32)).astype(BF16) for v, hi in zip(x2, x_hi)]
    logit = [_dot(hi, wr_hi_ref[...]) + _dot(lo, wr_hi_ref[...]) + _dot(hi, wr_lo_ref[...]) + br_ref[...]
             for hi, lo in zip(x_hi, x_lo)]
    lane = lax.broadcasted_iota(jnp.int32, (sub, LANES), 1).astype(F32)
    big = jnp.float32(1e9)
    r_i = lax.broadcasted_iota(jnp.int32, (sub, sub), 0)
    c_i = lax.broadcasted_iota(jnp.int32, (sub, sub), 1)
    tri = jnp.where(c_i < r_i, 1.0, 0.0).astype(BF16)

    def first_argmax(vals, vmax):
        return jnp.min(jnp.where(vals == vmax, lane, big), axis=1, keepdims=True)

    for part in parts:
        logits = logit[part]
        g_mask = (lane >= N_EXPERTS) & (lane < N_EXPERTS + N_GROUPS)
        lg = jnp.where(g_mask, logits, -jnp.inf)
        g_max = jnp.max(lg, axis=1, keepdims=True)
        g_sel = first_argmax(lg, g_max) - N_EXPERTS
        g_w = 1.0 / jnp.sum(jnp.where(g_mask, jnp.exp(logits - g_max), 0.0), axis=1, keepdims=True)
        in_group = jnp.floor(lane * (1.0 / EXPERTS_PER_GROUP)) == g_sel
        le = jnp.where(in_group, logits, -jnp.inf)
        v1 = jnp.max(le, axis=1, keepdims=True)
        e1 = first_argmax(le, v1)
        le2 = jnp.where(lane == e1, -jnp.inf, le)
        v2 = jnp.max(le2, axis=1, keepdims=True)
        e2 = first_argmax(le2, v2)
        t2 = jnp.exp(v2 - v1)
        w1 = (1.0 / (1.0 + t2)) * g_w
        w2 = (t2 / (1.0 + t2)) * g_w

        hit1 = lane == e1
        hit2 = lane == e2
        onehot = jnp.where(hit1 | hit2, 1.0, 0.0)
        before = _dot(tri, onehot.astype(BF16)) + run_ref[...]
        r1 = jnp.sum(jnp.where(hit1, before, 0.0), axis=1, keepdims=True)
        r2 = jnp.sum(jnp.where(hit2, before, 0.0), axis=1, keepdims=True)
        run_ref[...] = run_ref[...] + jnp.sum(onehot, axis=0, keepdims=True)

        packed = jnp.zeros((sub, LANES), F32)
        for pos, val in enumerate((e1, e2, w1, w2, r1, r2)):
            packed = jnp.where(lane == pos, val, packed)
        route_ref[rows[part], :] = packed
        idx_ref[part] = packed.T[:8, :]
    counts_ref[...] = jnp.broadcast_to(run_ref[...], counts_ref.shape)


def _post_mixer(x2d, o, gm, sg, mk, mv, p, seq, mem_len):
    t, d = x2d.shape
    tm, sub = TM_POST, TM_OUT
    assert seq % tm == 0, "a post_mixer row tile must not straddle two sequences (memory blocks are per sequence)"
    per_batch = seq // tm
    row = lambda n: pl.BlockSpec((tm, n), lambda i: (i, 0))
    memblk = pl.BlockSpec((mem_len, d), lambda i: (i // per_batch, 0))
    weights = [p["w_mla_out"], p["w_o"], p["ln1_g"], p["ln1_b"], p["w_mq"], p["w_mo"],
               p["ln2_g"], p["ln2_b"], p["wr_hi"], p["wr_lo"], p["br"]]
    return pl.pallas_call(
        functools.partial(_post_mixer_kernel, tm=tm, sub=sub, alpha=p["alpha"],
                          mem_scale=(d // MEM_HEADS) ** -0.5),
        grid=(t // tm,),
        out_shape=(jax.ShapeDtypeStruct((t, d), F32), jax.ShapeDtypeStruct((t, LANES), F32),
                   jax.ShapeDtypeStruct((t // sub, 8, sub), F32), jax.ShapeDtypeStruct((8, LANES), F32)),
        in_specs=[row(d), row(d), row(d), row(d), memblk, memblk]
                 + [_const_spec(w.shape) for w in weights],
        out_specs=(row(d), row(LANES), pl.BlockSpec((tm // sub, 8, sub), lambda i: (i, 0, 0)),
                   pl.BlockSpec((8, LANES), lambda i: (0, 0))),
        scratch_shapes=[pltpu.VMEM((1, LANES), F32)],
        compiler_params=pltpu.CompilerParams(dimension_semantics=("arbitrary",),
                                             vmem_limit_bytes=VMEM_LIMIT),
        name="post_mixer",
    )(x2d, o, gm, sg, mk, mv, *weights)


def _dispatch_kernel(last_ref, has_ref, nvalid_ref, dst_ref, x_hbm, xd_hbm, zbuf, xbuf, zsem, lsems, ssems,
                     *, tm, nsteps, rows, nblocks):
    i = pl.program_id(0)

    def load(step, s):
        return pltpu.make_async_copy(x_hbm.at[pl.ds(pl.multiple_of(step * tm, tm), tm)], xbuf.at[s], lsems.at[s])

    def drain(s):
        pltpu.make_async_copy(xd_hbm.at[pl.ds(0, 2 * tm)], xd_hbm.at[pl.ds(0, 2 * tm)], ssems.at[s]).wait()

    @pl.when(i == 0)
    def _():
        load(0, 0).start()
        zbuf[...] = jnp.zeros_like(zbuf)

        def zero_copy(start):
            return pltpu.make_async_copy(zbuf, xd_hbm.at[pl.ds(pl.multiple_of(start, rows), rows)], zsem)

        for e in range(N_EXPERTS):
            @pl.when(has_ref[e] > 0)
            def _():
                zero_copy(last_ref[e]).start()

        def start_tail(blk, carry):
            zero_copy(blk * rows).start()
            return carry

        def wait_tail(blk, carry):
            zero_copy(blk * rows).wait()
            return carry

        lax.fori_loop(nvalid_ref[0], nblocks, start_tail, 0)
        for e in range(N_EXPERTS):
            @pl.when(has_ref[e] > 0)
            def _():
                zero_copy(last_ref[e]).wait()

        lax.fori_loop(nvalid_ref[0], nblocks, wait_tail, 0)

    for s in range(3):
        @pl.when(i % 3 == s)
        def _():
            nxt = (s + 1) % 3
            load(i, s).wait()

            @pl.when(i >= 2)
            def _():
                drain(nxt)

            @pl.when(i + 1 < nsteps)
            def _():
                load(i + 1, nxt).start()

            for k in range(2):
                for r in range(tm):
                    pltpu.make_async_copy(xbuf.at[s, pl.ds(r, 1)],
                                          xd_hbm.at[pl.ds(dst_ref[0, 0, k * tm + r], 1)], ssems.at[s]).start()

            @pl.when(i == nsteps - 1)
            def _():
                if nsteps >= 2:
                    drain((s + 2) % 3)
                drain(s)


def _moe_dispatch(x2, dst3, last_start, has_rows, nvalid, total_rows):
    t, d = x2.shape
    tm = TM_OUT
    nsteps = t // tm
    assert dst3.shape == (nsteps, 1, 2 * tm), "post_mixer and dispatch/combine tiles must coincide"
    grid_spec = pltpu.PrefetchScalarGridSpec(
        num_scalar_prefetch=3,
        grid=(nsteps,),
        in_specs=[pl.BlockSpec((1, 1, 2 * tm), lambda i, la, ha, nv: (i, 0, 0), memory_space=pltpu.SMEM),
                  pl.BlockSpec(memory_space=pl.ANY)],
        out_specs=pl.BlockSpec(memory_space=pl.ANY),
        scratch_shapes=[pltpu.VMEM((MOE_ROWS, d), F32), pltpu.VMEM((3, tm, d), F32),
                        pltpu.SemaphoreType.DMA(()), pltpu.SemaphoreType.DMA((3,)),
                        pltpu.SemaphoreType.DMA((3,))],
    )
    return pl.pallas_call(
        functools.partial(_dispatch_kernel, tm=tm, nsteps=nsteps, rows=MOE_ROWS,
                          nblocks=total_rows // MOE_ROWS),
        grid_spec=grid_spec,
        out_shape=jax.ShapeDtypeStruct((total_rows, d), F32),
        compiler_params=pltpu.CompilerParams(dimension_semantics=("arbitrary",), has_side_effects=True),
        name="moe_dispatch",
    )(last_start, has_rows, nvalid, dst3, x2)


def _experts_kernel(be_ref, nvalid_ref, xd_ref, wg_ref, wu_ref, wd_ref, yd_ref, wgu_bf, wd_bf):
    i = pl.program_id(0)
    changed = jnp.logical_or(i == 0, be_ref[i] != be_ref[jnp.maximum(i - 1, 0)])

    @pl.when(changed)
    def _():
        wgu_bf[:, :D_EXPERT] = wg_ref[...].astype(BF16)
        wgu_bf[:, D_EXPERT:] = wu_ref[...].astype(BF16)
        wd_bf[...] = wd_ref[...].astype(BF16)

    @pl.when(i < nvalid_ref[0])
    def _():
        gu = _dot(xd_ref[...].astype(BF16), wgu_bf[...])
        gate = gu[:, :D_EXPERT]
        hidden = (gate * jax.nn.sigmoid(gate) * gu[:, D_EXPERT:]).astype(BF16)
        yd_ref[...] = _dot(hidden, wd_bf[...])

    @pl.when(i >= nvalid_ref[0])
    def _():
        yd_ref[...] = jnp.zeros_like(yd_ref)


def _moe_experts(xd, block_e, nvalid, w_gate, w_up, w_down):
    total, d = xd.shape
    rows = MOE_ROWS
    nblocks = total // rows
    grid_spec = pltpu.PrefetchScalarGridSpec(
        num_scalar_prefetch=2,
        grid=(nblocks,),
        in_specs=[pl.BlockSpec((rows, d), lambda i, be, nv: (jnp.minimum(i, nv[0] - 1), 0)),
                  pl.BlockSpec((None, d, D_EXPERT), lambda i, be, nv: (be[i], 0, 0)),
                  pl.BlockSpec((None, d, D_EXPERT), lambda i, be, nv: (be[i], 0, 0)),
                  pl.BlockSpec((None, D_EXPERT, d), lambda i, be, nv: (be[i], 0, 0))],
        out_specs=pl.BlockSpec((rows, d), lambda i, be, nv: (i, 0)),
        scratch_shapes=[pltpu.VMEM((d, 2 * D_EXPERT), BF16), pltpu.VMEM((D_EXPERT, d), BF16)],
    )
    return pl.pallas_call(
        _experts_kernel,
        grid_spec=grid_spec,
        out_shape=jax.ShapeDtypeStruct((total, d), F32),
        compiler_params=pltpu.CompilerParams(dimension_semantics=("arbitrary",),
                                             vmem_limit_bytes=VMEM_LIMIT),
        name="moe_experts",
    )(block_e, nvalid, xd, w_gate, w_up, w_down)


def _combine_kernel(dst_cur_ref, dst_nxt_ref, x2_ref, route_ref, g_ref, b_ref, yd_hbm,
                    out_ref, ybuf, sems, *, tm, nsteps, alpha):
    i = pl.program_id(0)

    def issue(idx_ref, s):
        for r in range(2 * tm):
            pltpu.make_async_copy(yd_hbm.at[pl.ds(idx_ref[0, 0, r], 1)], ybuf.at[s, pl.ds(r, 1)],
                                  sems.at[s]).start()

    def wait(s):
        pltpu.make_async_copy(yd_hbm.at[pl.ds(0, 2 * tm)], ybuf.at[s], sems.at[s]).wait()

    @pl.when(i == 0)
    def _():
        issue(dst_cur_ref, 0)

    for s in range(2):
        @pl.when(i % 2 == s)
        def _():
            wait(s)
            issue(dst_nxt_ref, 1 - s)
            route = route_ref[...]
            y = ybuf[s, :tm, :] * route[:, 2:3] + ybuf[s, tm:, :] * route[:, 3:4]
            out_ref[...] = _layer_norm(alpha * x2_ref[...] + y, g_ref[...], b_ref[...])

            @pl.when(i == nsteps - 1)
            def _():
                wait(1 - s)


def _moe_combine(x2, route, dst3, yd, ln_g, ln_b, alpha):
    t, d = x2.shape
    tm = TM_OUT
    nsteps = t // tm
    smem_blk = lambda f: pl.BlockSpec((1, 1, 2 * tm), f, memory_space=pltpu.SMEM)
    row = lambda n: pl.BlockSpec((tm, n), lambda i: (i, 0))
    return pl.pallas_call(
        functools.partial(_combine_kernel, tm=tm, nsteps=nsteps, alpha=alpha),
        grid=(nsteps,),
        out_shape=jax.ShapeDtypeStruct((t, d), F32),
        in_specs=[smem_blk(lambda i: (i, 0, 0)),
                  smem_blk(lambda i: (jnp.minimum(i + 1, nsteps - 1), 0, 0)),
                  row(d), row(LANES), _const_spec(ln_g.shape), _const_spec(ln_b.shape),
                  pl.BlockSpec(memory_space=pl.ANY)],
        out_specs=row(d),
        scratch_shapes=[pltpu.VMEM((2, 2 * tm, d), F32), pltpu.SemaphoreType.DMA((2,))],
        compiler_params=pltpu.CompilerParams(dimension_semantics=("arbitrary",),
                                             vmem_limit_bytes=VMEM_LIMIT),
        name="moe_combine",
    )(dst3, dst3, x2, route, ln_g, ln_b, yd)


def _prep_layer(w_in, b_in, gm_ln_g, gm_ln_b, gm_w_s, gm_b_s, w_gm_out, mla_q_norm_g, mla_kv_norm_g,
                w_uq, w_uk, w_uv, w_mla_out, w_o, ln1_g, ln1_b, w_mq, w_mk, w_mv, w_mo, ln2_g, ln2_b,
                w_group_router, b_group_router, w_expert_router, b_expert_router, depth):
    d = w_in.shape[0]
    s_v = 2 * GM_WIDTH
    s_q = s_v + MLA_Q_LORA
    s_kv = s_q + MLA_KV_LORA
    s_r = s_kv + MLA_ROPE
    rowv = lambda a: a.reshape(1, -1).astype(F32)

    def lat_cols(a):
        z = jnp.zeros(a.shape[:-1] + (MLA_NOPE,), a.dtype)
        kr = a[..., s_kv:s_r]
        return jnp.concatenate([a[..., s_v:s_kv], z, kr, kr], axis=-1)

    wq3 = w_uq.reshape(MLA_Q_LORA, MLA_HEADS, MLA_NOPE + MLA_ROPE)
    wq_pad = jnp.concatenate([wq3, wq3[..., MLA_NOPE:]], axis=-1).reshape(MLA_Q_LORA, MLA_HEADS * HEAD_PAD)
    wk3 = w_uk.reshape(MLA_KV_LORA, MLA_HEADS, MLA_NOPE)
    wk_pad = jnp.pad(wk3, ((0, 0), (0, 0), (0, HEAD_PAD - MLA_NOPE))).reshape(MLA_KV_LORA, MLA_HEADS * HEAD_PAD)
    wv3 = w_uv.T.reshape(MLA_HEADS, MLA_V, MLA_KV_LORA)
    wv_pad = jnp.pad(wv3, ((0, 0), (0, V_ROWS - MLA_V), (0, 0))).reshape(MLA_HEADS * V_ROWS, MLA_KV_LORA)
    b_v = jnp.zeros((MLA_HEADS, V_ROWS, 1), F32).at[:, MLA_V].set(1.0).reshape(MLA_HEADS * V_ROWS, 1)
    w_r = jnp.zeros((d, LANES), F32)
    w_r = w_r.at[:, :N_EXPERTS].set(w_expert_router).at[:, N_EXPERTS:N_EXPERTS + N_GROUPS].set(w_group_router)
    b_r = jnp.zeros((LANES,), F32)
    b_r = b_r.at[:N_EXPERTS].set(b_expert_router).at[N_EXPERTS:N_EXPERTS + N_GROUPS].set(b_group_router)
    wr_hi = w_r.astype(BF16)
    return dict(
        w_uv=w_in[:, :s_v].astype(BF16), b_uv=rowv(b_in[:s_v]),
        w_gate=w_in[:, s_r:].astype(BF16), b_gate=rowv(b_in[s_r:]),
        w_lat=lat_cols(w_in).astype(BF16), b_lat=rowv(lat_cols(b_in)),
        gm_ln_g=rowv(gm_ln_g), gm_ln_b=rowv(gm_ln_b),
        ws=gm_w_s.astype(BF16), bs_full=jnp.repeat(gm_b_s.T, GM_CHUNK, axis=1).astype(F32),
        w_gm_out=w_gm_out.astype(BF16), qn_g=rowv(mla_q_norm_g), kvn_g=rowv(mla_kv_norm_g),
        w_uq=wq_pad.astype(BF16), w_uk=wk_pad.astype(BF16), w_uvv=wv_pad.astype(BF16), b_v=b_v,
        w_mla_out=w_mla_out.astype(BF16), w_o=w_o.astype(BF16), ln1_g=rowv(ln1_g), ln1_b=rowv(ln1_b),
        w_mq=w_mq.astype(BF16), w_mk=w_mk.astype(BF16), w_mv=w_mv.astype(BF16), w_mo=w_mo.astype(BF16),
        ln2_g=rowv(ln2_g), ln2_b=rowv(ln2_b),
        wr_hi=wr_hi, wr_lo=(w_r - wr_hi.astype(F32)).astype(BF16), br=rowv(b_r),
        alpha=(2 * depth) ** 0.25,
    )


def _dispatch_tables(idx, counts, t):
    rows = MOE_ROWS
    tiles, _, tm = idx.shape
    e = idx[:, 0:2, :].astype(jnp.int32)
    rank = idx[:, 4:6, :].astype(jnp.int32)
    cnt = counts[0, :N_EXPERTS].astype(jnp.int32)
    padded = (cnt + rows - 1) // rows * rows
    ends = jnp.cumsum(padded)
    starts = ends - padded
    ids = jnp.arange(N_EXPERTS, dtype=jnp.int32)[:, None, None, None]
    start_of = jnp.sum(jnp.where(e[None] == ids, starts[:, None, None, None], 0), axis=0)
    dst3 = (start_of + rank).reshape(tiles, 1, 2 * tm)
    total = 2 * t + N_EXPERTS * rows
    block_start = jnp.arange(total // rows, dtype=jnp.int32) * rows
    block_e = jnp.minimum(jnp.sum(ends[None, :] <= block_start[:, None], axis=1), N_EXPERTS - 1)
    nvalid = (ends[-1] // rows).reshape(1).astype(jnp.int32)
    last_start = jnp.maximum(ends - rows, 0).astype(jnp.int32)
    has_rows = (cnt > 0).astype(jnp.int32)
    return dst3, block_e.astype(jnp.int32), nvalid, last_start, has_rows, total


def _layer(x, mem, positions, depth, w_in, b_in, gm_ln_g, gm_ln_b, gm_w_s, gm_b_s, w_gm_out,
           mla_q_norm_g, mla_kv_norm_g, w_uq, w_uk, w_uv, w_mla_out, w_o, ln1_g, ln1_b,
           w_mq, w_mk, w_mv, w_mo, ln2_g, ln2_b,
           w_group_router, b_group_router, w_expert_router, b_expert_router,
           w_exp_gate, w_exp_up, w_exp_down, ln3_g, ln3_b):
    b, s, d = x.shape
    t = b * s
    mem_len = mem.shape[1]
    p = _prep_layer(w_in, b_in, gm_ln_g, gm_ln_b, gm_w_s, gm_b_s, w_gm_out, mla_q_norm_g, mla_kv_norm_g,
                    w_uq, w_uk, w_uv, w_mla_out, w_o, ln1_g, ln1_b, w_mq, w_mk, w_mv, w_mo, ln2_g, ln2_b,
                    w_group_router, b_group_router, w_expert_router, b_expert_router, depth)
    x2d = x.reshape(t, d)
    inv_freq = ROPE_THETA ** (-jnp.arange(ROPE_HALF, dtype=F32) / ROPE_HALF)
    cos, sin = _rope_tables(positions, inv_freq)
    gm, sg, q, k, v = _mixer_in(x2d, cos, sin, p)
    o = _mla_attn(q, k, v, b, s)
    mk, mv = _mem_kv(mem.reshape(b * mem_len, d), p["w_mk"], p["w_mv"], mem_len)
    x2, route, idx, counts = _post_mixer(x2d, o, gm, sg, mk, mv, p, s, mem_len)
    dst3, block_e, nvalid, last_start, has_rows, total = _dispatch_tables(idx, counts, t)
    xd = _moe_dispatch(x2, dst3, last_start, has_rows, nvalid, total)
    yd = _moe_experts(xd, block_e, nvalid, w_exp_gate, w_exp_up, w_exp_down)
    out = _moe_combine(x2, route, dst3, yd, ln3_g.reshape(1, d), ln3_b.reshape(1, d), p["alpha"])
    return out.reshape(b, s, d)


def kernel(x, mem, positions, w_in, b_in, gm_ln_g, gm_ln_b, gm_w_s, gm_b_s, w_gm_out, mla_q_norm_g, mla_kv_norm_g, w_uq, w_uk, w_uv, w_mla_out, w_o, ln1_g, ln1_b, w_mq, w_mk, w_mv, w_mo, ln2_g, ln2_b, w_group_router, b_group_router, w_expert_router, b_expert_router, w_exp_gate, w_exp_up, w_exp_down, ln3_g, ln3_b):
    depth = w_in.shape[0]
    per_layer = (w_in, b_in, gm_ln_g, gm_ln_b, gm_w_s, gm_b_s, w_gm_out, mla_q_norm_g, mla_kv_norm_g,
                 w_uq, w_uk, w_uv, w_mla_out, w_o, ln1_g, ln1_b, w_mq, w_mk, w_mv, w_mo, ln2_g, ln2_b,
                 w_group_router, b_group_router, w_expert_router, b_expert_router,
                 w_exp_gate, w_exp_up, w_exp_down, ln3_g, ln3_b)
    h = x
    for l in range(depth):
        h = _layer(h, mem, positions, depth, *[w[l] for w in per_layer])
    return h
```

```python
import functools
import math

import jax
import jax.numpy as jnp
from jax import lax
from jax.experimental import pallas as pl
from jax.experimental.pallas import tpu as pltpu

F32 = jnp.float32
BF16 = jnp.bfloat16

GM_WIDTH = 1024
GM_GROUPS = 8
GM_CHUNK = 128
MLA_HEADS = 16
MLA_NOPE = 64
MLA_ROPE = 32
MLA_V = 64
MLA_Q_LORA = 384
MLA_KV_LORA = 256
ROPE_THETA = 10000.0
MEM_HEADS = 4
N_GROUPS = 8
EXPERTS_PER_GROUP = 8
N_EXPERTS = N_GROUPS * EXPERTS_PER_GROUP
D_EXPERT = 256
LN_EPS = 1e-5
RMS_EPS = 1e-6

LANES = 128
HEAD_PAD = 128
V_ROWS = 80
ROPE_HALF = MLA_ROPE // 2

TM_IN = 512
TM_SUB = 256
TQ = 512
TK = 256
ATTN_HEADS = 2
SM_CHUNK = 16
TM_POST = 1024
MOE_ROWS = 256
TM_OUT = 256
VMEM_LIMIT = 56 * 1024 * 1024


def _const_spec(shape):
    nd = len(shape)
    return pl.BlockSpec(shape, lambda *_: (0,) * nd, pipeline_mode=pl.Buffered(1))


def _layer_norm(x, g, b):
    mu = jnp.mean(x, axis=-1, keepdims=True)
    xc = x - mu
    var = jnp.mean(xc * xc, axis=-1, keepdims=True)
    return xc * lax.rsqrt(var + LN_EPS) * g + b


def _rms_norm(x, g):
    return x * lax.rsqrt(jnp.mean(x * x, axis=-1, keepdims=True) + RMS_EPS) * g


def _gelu(x):
    return x * (lax.erf(x * (1.0 / math.sqrt(2.0))) + 1.0) * 0.5


def _dot(a, b):
    return jnp.dot(a, b, preferred_element_type=F32)


def _dot_nt(a, b):
    return lax.dot_general(a, b, (((1,), (1,)), ((), ())), preferred_element_type=F32)


def _rope_kernel(inv_ref, pos_ref, cos_ref, sin_ref):
    p = pos_ref[...].astype(F32)
    for j in range(ROPE_HALF):
        ang = p * inv_ref[j]
        cos_ref[j] = jnp.cos(ang)
        sin_ref[j] = jnp.sin(ang)


def _rope_tables(positions, inv_freq):
    t = positions.size
    rows = t // LANES
    pos2 = positions.reshape(rows, LANES)
    cos, sin = pl.pallas_call(
        _rope_kernel,
        out_shape=(jax.ShapeDtypeStruct((ROPE_HALF, rows, LANES), F32),) * 2,
        in_specs=[pl.BlockSpec(memory_space=pltpu.SMEM),
                  pl.BlockSpec((rows, LANES), lambda: (0, 0))],
        out_specs=(pl.BlockSpec((ROPE_HALF, rows, LANES), lambda: (0, 0, 0)),) * 2,
        name="rope_tables",
    )(inv_freq, pos2)
    cos = jnp.tile(cos.reshape(ROPE_HALF, t).T, (1, LANES // ROPE_HALF))
    sin = jnp.tile(sin.reshape(ROPE_HALF, t).T, (1, LANES // ROPE_HALF))
    return cos, sin


def _mixer_in_kernel(x_ref, cos_ref, sin_ref, w_uv_ref, b_uv_ref, w_gate_ref, b_gate_ref,
                     w_lat_ref, b_lat_ref, ln_g_ref, ln_b_ref, ws_ref, bs_ref, w_gm_out_ref,
                     qn_g_ref, kvn_g_ref, w_uq_ref, w_uk_ref, w_uvv_ref, b_v_ref,
                     gm_ref, sg_ref, q_ref, k_ref, v_ref, mixed_ref, *, tm, sub, scale):
    parts = range(tm // sub)
    rows = [slice(part * sub, (part + 1) * sub) for part in parts]
    xb = [x_ref[r, :].astype(BF16) for r in rows]

    uv = [_gelu(_dot(x, w_uv_ref[...]) + b_uv_ref[...]) for x in xb]
    vn = [_layer_norm(t[:, GM_WIDTH:], ln_g_ref[...], ln_b_ref[...]).astype(BF16) for t in uv]
    r_i = lax.broadcasted_iota(jnp.int32, (GM_CHUNK, GM_CHUNK), 0)
    c_i = lax.broadcasted_iota(jnp.int32, (GM_CHUNK, GM_CHUNK), 1)
    causal = c_i <= r_i
    for g in range(GM_GROUPS):
        w = jnp.where(causal, ws_ref[g], jnp.zeros_like(ws_ref[g]))
        cols = slice(g * GM_CHUNK, (g + 1) * GM_CHUNK)
        for part in parts:
            for ch in range(sub // GM_CHUNK):
                local = slice(ch * GM_CHUNK, (ch + 1) * GM_CHUNK)
                dest = slice(part * sub + ch * GM_CHUNK, part * sub + (ch + 1) * GM_CHUNK)
                mixed_ref[dest, cols] = _dot(w, vn[part][local, cols]) + bs_ref[:, cols]
    gated = [(t[:, :GM_WIDTH] * mixed_ref[r, :]).astype(BF16) for t, r in zip(uv, rows)]
    y_gm = [_dot(t, w_gm_out_ref[...]) for t in gated]
    gates = [_dot(x, w_gate_ref[...]) + b_gate_ref[...] for x in xb]
    for r, gt, y in zip(rows, gates, y_gm):
        gm_ref[r, :] = (jax.nn.sigmoid(gt[:, :GM_WIDTH]) * y).astype(BF16)
        sg_ref[r, :] = jax.nn.sigmoid(gt[:, GM_WIDTH:]).astype(BF16)

    lat = [_dot(x, w_lat_ref[...]) + b_lat_ref[...] for x in xb]
    cq = [_rms_norm(t[:, :MLA_Q_LORA], qn_g_ref[...]).astype(BF16) for t in lat]
    ckv = [_rms_norm(t[:, MLA_Q_LORA:MLA_Q_LORA + MLA_KV_LORA], kvn_g_ref[...]).astype(BF16) for t in lat]
    qa = [_dot(t, w_uq_ref[...]) for t in cq]
    kn = [_dot(t, w_uk_ref[...]) for t in ckv]

    lane = lax.broadcasted_iota(jnp.int32, (sub, HEAD_PAD), 1)
    in_rope = (lane >= MLA_NOPE) & (lane < MLA_NOPE + MLA_ROPE)
    for part in parts:
        r = rows[part]
        cosv = cos_ref[r, :]
        sinv = sin_ref[r, :]
        cfac = jnp.where(lane < MLA_NOPE, 1.0, jnp.where(in_rope, cosv, 0.0))
        sfac = jnp.where(in_rope, jnp.where(lane < MLA_NOPE + ROPE_HALF, -sinv, sinv), 0.0)

        def rot(t, cfac=cfac, sfac=sfac):
            return t * cfac + pltpu.roll(t, HEAD_PAD - ROPE_HALF, 1) * sfac

        kr_rot = rot(lat[part][:, MLA_Q_LORA + MLA_KV_LORA:])
        for h in range(MLA_HEADS):
            cols = slice(h * HEAD_PAD, (h + 1) * HEAD_PAD)
            q_ref[r, cols] = (rot(qa[part][:, cols]) * scale).astype(BF16)
            k_ref[r, cols] = (kn[part][:, cols] + kr_rot).astype(BF16)
        v_ref[:, r] = (_dot_nt(w_uvv_ref[...], ckv[part]) + b_v_ref[...]).astype(BF16)


def _mixer_in(x2d, cos, sin, p):
    t, d = x2d.shape
    tm = TM_IN
    scale = (MLA_NOPE + MLA_ROPE) ** -0.5 * math.log2(math.e)
    row = lambda n: pl.BlockSpec((tm, n), lambda i: (i, 0))
    weights = [p["w_uv"], p["b_uv"], p["w_gate"], p["b_gate"], p["w_lat"], p["b_lat"],
               p["gm_ln_g"], p["gm_ln_b"], p["ws"], p["bs_full"], p["w_gm_out"],
               p["qn_g"], p["kvn_g"], p["w_uq"], p["w_uk"], p["w_uvv"], p["b_v"]]
    qk_w = MLA_HEADS * HEAD_PAD
    v_w = MLA_HEADS * V_ROWS
    return pl.pallas_call(
        functools.partial(_mixer_in_kernel, tm=tm, sub=TM_SUB, scale=scale),
        grid=(t // tm,),
        out_shape=(jax.ShapeDtypeStruct((t, d), BF16), jax.ShapeDtypeStruct((t, d), BF16),
                   jax.ShapeDtypeStruct((t, qk_w), BF16), jax.ShapeDtypeStruct((t, qk_w), BF16),
                   jax.ShapeDtypeStruct((v_w, t), BF16)),
        in_specs=[row(d), row(LANES), row(LANES)] + [_const_spec(w.shape) for w in weights],
        out_specs=(row(d), row(d), row(qk_w), row(qk_w), pl.BlockSpec((v_w, tm), lambda i: (0, i))),
        scratch_shapes=[pltpu.VMEM((tm, GM_WIDTH), F32)],
        compiler_params=pltpu.CompilerParams(dimension_semantics=("arbitrary",),
                                             vmem_limit_bytes=VMEM_LIMIT),
        name="mixer_in",
    )(x2d, cos, sin, *weights)


def _attn_kernel(q_ref, k_ref, vt_ref, o_ref, acc_ref, m_ref, r0_ref, r1_ref, s0_ref, s1_ref, p0_ref, p1_ref,
                 *, tq, tk):
    i = pl.program_id(2)
    acc_ref[...] = jnp.zeros_like(acc_ref)
    m_ref[...] = jnp.full(m_ref.shape, -jnp.inf, F32)

    every = slice(0, tq)
    lower = slice(0, tk)
    upper = slice(tk, tq)

    def scores(j, s_ref, qs=every):
        kb = k_ref[pl.ds(pl.multiple_of(j * tk, tk), tk), :]
        for a in range(ATTN_HEADS):
            cols = slice(a * HEAD_PAD, (a + 1) * HEAD_PAD)
            s_ref[a, :, qs] = _dot_nt(kb[:, cols], q_ref[qs, cols])

    def softmax(s_ref, p_ref, r_ref, qs=every, diagonal=False):
        nch = tk // SM_CHUNK
        width = qs.stop - qs.start
        if diagonal:
            q_loc = lax.broadcasted_iota(jnp.int32, (SM_CHUNK, width), 1)
            k_loc = lax.broadcasted_iota(jnp.int32, (SM_CHUNK, width), 0)

        def chunk(a, c):
            s = s_ref[a, c * SM_CHUNK:(c + 1) * SM_CHUNK, qs]
            if diagonal:
                s = jnp.where(k_loc + c * SM_CHUNK <= q_loc, s, -jnp.inf)
            return s

        for a in range(ATTN_HEADS):
            m = m_ref[a:a + 1, qs]
            cm = chunk(a, 0)
            for c in range(1, nch):
                cm = jnp.maximum(cm, chunk(a, c))
            m_new = jnp.maximum(m, jnp.max(cm, axis=0, keepdims=True))
            r_ref[a:a + 1, qs] = jnp.exp2(m - m_new)
            m_ref[a:a + 1, qs] = m_new
            mb = jnp.broadcast_to(m_new, (SM_CHUNK, width))
            for c in range(nch):
                p_ref[a, c * SM_CHUNK:(c + 1) * SM_CHUNK, qs] = jnp.exp2(chunk(a, c) - mb).astype(BF16)

    def accumulate(j, p_ref, r_ref, qs=every):
        vb = vt_ref[:, pl.ds(pl.multiple_of(j * tk, tk), tk)]
        for a in range(ATTN_HEADS):
            rows = slice(a * V_ROWS, (a + 1) * V_ROWS)
            alpha = r_ref[a:a + 1, qs]
            acc_ref[rows, qs] = acc_ref[rows, qs] * alpha + _dot(vb[rows, :], p_ref[a, :, qs])

    def pair(base):
        scores(base + 1, s1_ref)
        softmax(s0_ref, p0_ref, r0_ref)
        accumulate(base, p0_ref, r0_ref)
        scores(base + 2, s0_ref)
        softmax(s1_ref, p1_ref, r1_ref)
        accumulate(base + 1, p1_ref, r1_ref)

    def main(t, carry):
        pair(4 * t)
        pair(4 * t + 2)
        return carry

    scores(0, s0_ref)
    lax.fori_loop(0, i // 2, main, 0)

    @pl.when(i % 2 == 1)
    def _():
        pair(2 * i - 2)

    scores(2 * i + 1, s1_ref, upper)
    softmax(s0_ref, p0_ref, r0_ref, lower, diagonal=True)
    softmax(s0_ref, p0_ref, r0_ref, upper)
    accumulate(2 * i, p0_ref, r0_ref)
    softmax(s1_ref, p1_ref, r1_ref, upper, diagonal=True)
    accumulate(2 * i + 1, p1_ref, r1_ref, upper)
    out = [acc_ref[a * V_ROWS:a * V_ROWS + MLA_V, :] / acc_ref[a * V_ROWS + MLA_V:a * V_ROWS + MLA_V + 1, :]
           for a in range(ATTN_HEADS)]
    o_ref[...] = jnp.concatenate(out, axis=0).T.astype(BF16)


def _mla_attn(q, k, vt, batch, seq):
    assert TQ == 2 * TK
    nq = seq // TQ
    nh = ATTN_HEADS
    return pl.pallas_call(
        functools.partial(_attn_kernel, tq=TQ, tk=TK),
        grid=(batch, MLA_HEADS // nh, nq),
        out_shape=jax.ShapeDtypeStruct((batch * seq, MLA_HEADS * MLA_V), BF16),
        in_specs=[pl.BlockSpec((TQ, nh * HEAD_PAD), lambda b, h, i: (b * nq + i, h)),
                  pl.BlockSpec((seq, nh * HEAD_PAD), lambda b, h, i: (b, h)),
                  pl.BlockSpec((nh * V_ROWS, seq), lambda b, h, i: (h, b))],
        out_specs=pl.BlockSpec((TQ, nh * MLA_V), lambda b, h, i: (b * nq + i, h)),
        scratch_shapes=[pltpu.VMEM((nh * V_ROWS, TQ), F32)] + [pltpu.VMEM((8, TQ), F32)] * 3 + [
                        pltpu.VMEM((nh, TK, TQ), F32), pltpu.VMEM((nh, TK, TQ), F32),
                        pltpu.VMEM((nh, TK, TQ), BF16), pltpu.VMEM((nh, TK, TQ), BF16)],
        compiler_params=pltpu.CompilerParams(
            dimension_semantics=("arbitrary", "arbitrary", "arbitrary"), vmem_limit_bytes=VMEM_LIMIT),
        name="mla_attn",
    )(q, k, vt)


def _mem_kv_kernel(mem_ref, wk_ref, wv_ref, k_ref, v_ref):
    mb = mem_ref[...].astype(BF16)
    k_ref[...] = _dot(mb, wk_ref[...]).astype(BF16)
    v_ref[...] = _dot(mb, wv_ref[...]).astype(BF16)


def _mem_kv(mem2d, w_mk, w_mv, mem_len):
    rows, d = mem2d.shape
    blk = pl.BlockSpec((mem_len, d), lambda i: (i, 0))
    return pl.pallas_call(
        _mem_kv_kernel,
        grid=(rows // mem_len,),
        out_shape=(jax.ShapeDtypeStruct((rows, d), BF16),) * 2,
        in_specs=[blk, _const_spec(w_mk.shape), _const_spec(w_mv.shape)],
        out_specs=(blk, blk),
        compiler_params=pltpu.CompilerParams(dimension_semantics=("arbitrary",)),
        name="mem_kv",
    )(mem2d, w_mk, w_mv)


def _post_mixer_kernel(x_ref, o_ref, gm_ref, sg_ref, mk_ref, mv_ref,
                       w_mla_out_ref, w_o_ref, ln1_g_ref, ln1_b_ref,
                       w_mq_ref, w_mo_ref, ln2_g_ref, ln2_b_ref,
                       wr_hi_ref, wr_lo_ref, br_ref,
                       x2_ref, route_ref, idx_ref, counts_ref, run_ref, *, tm, sub, alpha, mem_scale):
    i = pl.program_id(0)

    @pl.when(i == 0)
    def _():
        run_ref[...] = jnp.zeros_like(run_ref)

    parts = range(tm // sub)
    rows = [slice(part * sub, (part + 1) * sub) for part in parts]
    y_mla = [_dot(o_ref[r, :], w_mla_out_ref[...]) for r in rows]
    merged = [(gm_ref[r, :].astype(F32) + sg_ref[r, :].astype(F32) * y).astype(BF16) for r, y in zip(rows, y_mla)]
    mixed = [_dot(m, w_o_ref[...]) for m in merged]
    x1 = [_layer_norm(alpha * x_ref[r, :] + h, ln1_g_ref[...], ln1_b_ref[...]) for r, h in zip(rows, mixed)]

    qm = [(_dot(x.astype(BF16), w_mq_ref[...]) * mem_scale).astype(BF16) for x in x1]
    hd = mk_ref.shape[1] // MEM_HEADS
    heads = [[] for _ in parts]
    for h in range(MEM_HEADS):
        cols = slice(h * hd, (h + 1) * hd)
        s = [_dot_nt(q[:, cols], mk_ref[:, cols]) for q in qm]
        pr = [jnp.exp(v - jnp.max(v, axis=1, keepdims=True)) for v in s]
        for part in parts:
            heads[part].append(_dot(pr[part].astype(BF16), mv_ref[:, cols])
                               / jnp.sum(pr[part], axis=1, keepdims=True))
    om = [jnp.concatenate(hs, axis=1).astype(BF16) for hs in heads]
    mem_out = [_dot(v, w_mo_ref[...]) for v in om]
    x2 = [_layer_norm(alpha * x + h, ln2_g_ref[...], ln2_b_ref[...]) for x, h in zip(x1, mem_out)]
    for r, v in zip(rows, x2):
        x2_ref[r, :] = v

    x_hi = [v.astype(BF16) for v in x2]
    x_lo = [(v - hi.astype(F32)).astype(BF16) for v, hi in zip(x2, x_hi)]
    logit = [_dot(hi, wr_hi_ref[...]) + _dot(lo, wr_hi_ref[...]) + _dot(hi, wr_lo_ref[...]) + br_ref[...]
             for hi, lo in zip(x_hi, x_lo)]
    lane = lax.broadcasted_iota(jnp.int32, (sub, LANES), 1).astype(F32)
    big = jnp.float32(1e9)
    r_i = lax.broadcasted_iota(jnp.int32, (sub, sub), 0)
    c_i = lax.broadcasted_iota(jnp.int32, (sub, sub), 1)
    tri = jnp.where(c_i < r_i, 1.0, 0.0).astype(BF16)

    def first_argmax(vals, vmax):
        return jnp.min(jnp.where(vals == vmax, lane, big), axis=1, keepdims=True)

    for part in parts:
        logits = logit[part]
        g_mask = (lane >= N_EXPERTS) & (lane < N_EXPERTS + N_GROUPS)
        lg = jnp.where(g_mask, logits, -jnp.inf)
        g_max = jnp.max(lg, axis=1, keepdims=True)
        g_sel = first_argmax(lg, g_max) - N_EXPERTS
        g_w = 1.0 / jnp.sum(jnp.where(g_mask, jnp.exp(logits - g_max), 0.0), axis=1, keepdims=True)
        in_group = jnp.floor(lane * (1.0 / EXPERTS_PER_GROUP)) == g_sel
        le = jnp.where(in_group, logits, -jnp.inf)
        v1 = jnp.max(le, axis=1, keepdims=True)
        e1 = first_argmax(le, v1)
        le2 = jnp.where(lane == e1, -jnp.inf, le)
        v2 = jnp.max(le2, axis=1, keepdims=True)
        e2 = first_argmax(le2, v2)
        t2 = jnp.exp(v2 - v1)
        w1 = (1.0 / (1.0 + t2)) * g_w
        w2 = (t2 / (1.0 + t2)) * g_w

        hit1 = lane == e1
        hit2 = lane == e2
        onehot = jnp.where(hit1 | hit2, 1.0, 0.0)
        before = _dot(tri, onehot.astype(BF16)) + run_ref[...]
        r1 = jnp.sum(jnp.where(hit1, before, 0.0), axis=1, keepdims=True)
        r2 = jnp.sum(jnp.where(hit2, before, 0.0), axis=1, keepdims=True)
        run_ref[...] = run_ref[...] + jnp.sum(onehot, axis=0, keepdims=True)

        packed = jnp.zeros((sub, LANES), F32)
        for pos, val in enumerate((e1, e2, w1, w2, r1, r2)):
            packed = jnp.where(lane == pos, val, packed)
        route_ref[rows[part], :] = packed
        idx_ref[part] = packed.T[:8, :]
    counts_ref[...] = jnp.broadcast_to(run_ref[...], counts_ref.shape)


def _post_mixer(x2d, o, gm, sg, mk, mv, p, seq, mem_len):
    t, d = x2d.shape
    tm, sub = TM_POST, TM_OUT
    assert seq % tm == 0, "a post_mixer row tile must not straddle two sequences (memory blocks are per sequence)"
    per_batch = seq // tm
    row = lambda n: pl.BlockSpec((tm, n), lambda i: (i, 0))
    memblk = pl.BlockSpec((mem_len, d), lambda i: (i // per_batch, 0))
    weights = [p["w_mla_out"], p["w_o"], p["ln1_g"], p["ln1_b"], p["w_mq"], p["w_mo"],
               p["ln2_g"], p["ln2_b"], p["wr_hi"], p["wr_lo"], p["br"]]
    return pl.pallas_call(
        functools.partial(_post_mixer_kernel, tm=tm, sub=sub, alpha=p["alpha"],
                          mem_scale=(d // MEM_HEADS) ** -0.5),
        grid=(t // tm,),
        out_shape=(jax.ShapeDtypeStruct((t, d), F32), jax.ShapeDtypeStruct((t, LANES), F32),
                   jax.ShapeDtypeStruct((t // sub, 8, sub), F32), jax.ShapeDtypeStruct((8, LANES), F32)),
        in_specs=[row(d), row(d), row(d), row(d), memblk, memblk]
                 + [_const_spec(w.shape) for w in weights],
        out_specs=(row(d), row(LANES), pl.BlockSpec((tm // sub, 8, sub), lambda i: (i, 0, 0)),
                   pl.BlockSpec((8, LANES), lambda i: (0, 0))),
        scratch_shapes=[pltpu.VMEM((1, LANES), F32)],
        compiler_params=pltpu.CompilerParams(dimension_semantics=("arbitrary",),
                                             vmem_limit_bytes=VMEM_LIMIT),
        name="post_mixer",
    )(x2d, o, gm, sg, mk, mv, *weights)


def _dispatch_kernel(last_ref, has_ref, nvalid_ref, dst_ref, x_hbm, xd_hbm, zbuf, xbuf, zsem, lsems, ssems,
                     *, tm, nsteps, rows, nblocks):
    i = pl.program_id(0)

    def load(step, s):
        return pltpu.make_async_copy(x_hbm.at[pl.ds(pl.multiple_of(step * tm, tm), tm)], xbuf.at[s], lsems.at[s])

    def drain(s):
        pltpu.make_async_copy(xd_hbm.at[pl.ds(0, 2 * tm)], xd_hbm.at[pl.ds(0, 2 * tm)], ssems.at[s]).wait()

    @pl.when(i == 0)
    def _():
        load(0, 0).start()
        zbuf[...] = jnp.zeros_like(zbuf)

        def zero_copy(start):
            return pltpu.make_async_copy(zbuf, xd_hbm.at[pl.ds(pl.multiple_of(start, rows), rows)], zsem)

        for e in range(N_EXPERTS):
            @pl.when(has_ref[e] > 0)
            def _():
                zero_copy(last_ref[e]).start()

        def start_tail(blk, carry):
            zero_copy(blk * rows).start()
            return carry

        def wait_tail(blk, carry):
            zero_copy(blk * rows).wait()
            return carry

        lax.fori_loop(nvalid_ref[0], nblocks, start_tail, 0)
        for e in range(N_EXPERTS):
            @pl.when(has_ref[e] > 0)
            def _():
                zero_copy(last_ref[e]).wait()

        lax.fori_loop(nvalid_ref[0], nblocks, wait_tail, 0)

    for s in range(3):
        @pl.when(i % 3 == s)
        def _():
            nxt = (s + 1) % 3
            load(i, s).wait()

            @pl.when(i >= 2)
            def _():
                drain(nxt)

            @pl.when(i + 1 < nsteps)
            def _():
                load(i + 1, nxt).start()

            for k in range(2):
                for r in range(tm):
                    pltpu.make_async_copy(xbuf.at[s, pl.ds(r, 1)],
                                          xd_hbm.at[pl.ds(dst_ref[0, 0, k * tm + r], 1)],
                                          ssems.at[s]).start(priority=r % 2)

            @pl.when(i == nsteps - 1)
            def _():
                if nsteps >= 2:
                    drain((s + 2) % 3)
                drain(s)


def _moe_dispatch(x2, dst3, last_start, has_rows, nvalid, total_rows):
    t, d = x2.shape
    tm = TM_OUT
    nsteps = t // tm
    assert dst3.shape == (nsteps, 1, 2 * tm), "post_mixer and dispatch/combine tiles must coincide"
    grid_spec = pltpu.PrefetchScalarGridSpec(
        num_scalar_prefetch=3,
        grid=(nsteps,),
        in_specs=[pl.BlockSpec((1, 1, 2 * tm), lambda i, la, ha, nv: (i, 0, 0), memory_space=pltpu.SMEM),
                  pl.BlockSpec(memory_space=pl.ANY)],
        out_specs=pl.BlockSpec(memory_space=pl.ANY),
        scratch_shapes=[pltpu.VMEM((MOE_ROWS, d), F32), pltpu.VMEM((3, tm, d), F32),
                        pltpu.SemaphoreType.DMA(()), pltpu.SemaphoreType.DMA((3,)),
                        pltpu.SemaphoreType.DMA((3,))],
    )
    return pl.pallas_call(
        functools.partial(_dispatch_kernel, tm=tm, nsteps=nsteps, rows=MOE_ROWS,
                          nblocks=total_rows // MOE_ROWS),
        grid_spec=grid_spec,
        out_shape=jax.ShapeDtypeStruct((total_rows, d), F32),
        compiler_params=pltpu.CompilerParams(dimension_semantics=("arbitrary",), has_side_effects=True),
        name="moe_dispatch",
    )(last_start, has_rows, nvalid, dst3, x2)


def _experts_kernel(be_ref, nvalid_ref, xd_ref, wg_ref, wu_ref, wd_ref, yd_ref, wgu_bf, wd_bf):
    i = pl.program_id(0)
    changed = jnp.logical_or(i == 0, be_ref[i] != be_ref[jnp.maximum(i - 1, 0)])

    @pl.when(changed)
    def _():
        wgu_bf[:, :D_EXPERT] = wg_ref[...].astype(BF16)
        wgu_bf[:, D_EXPERT:] = wu_ref[...].astype(BF16)
        wd_bf[...] = wd_ref[...].astype(BF16)

    @pl.when(i < nvalid_ref[0])
    def _():
        gu = _dot(xd_ref[...].astype(BF16), wgu_bf[...])
        gate = gu[:, :D_EXPERT]
        hidden = (gate * jax.nn.sigmoid(gate) * gu[:, D_EXPERT:]).astype(BF16)
        yd_ref[...] = _dot(hidden, wd_bf[...])

    @pl.when(i >= nvalid_ref[0])
    def _():
        yd_ref[...] = jnp.zeros_like(yd_ref)


def _moe_experts(xd, block_e, nvalid, w_gate, w_up, w_down):
    total, d = xd.shape
    rows = MOE_ROWS
    nblocks = total // rows
    grid_spec = pltpu.PrefetchScalarGridSpec(
        num_scalar_prefetch=2,
        grid=(nblocks,),
        in_specs=[pl.BlockSpec((rows, d), lambda i, be, nv: (jnp.minimum(i, nv[0] - 1), 0)),
                  pl.BlockSpec((None, d, D_EXPERT), lambda i, be, nv: (be[i], 0, 0)),
                  pl.BlockSpec((None, d, D_EXPERT), lambda i, be, nv: (be[i], 0, 0)),
                  pl.BlockSpec((None, D_EXPERT, d), lambda i, be, nv: (be[i], 0, 0))],
        out_specs=pl.BlockSpec((rows, d), lambda i, be, nv: (i, 0)),
        scratch_shapes=[pltpu.VMEM((d, 2 * D_EXPERT), BF16), pltpu.VMEM((D_EXPERT, d), BF16)],
    )
    return pl.pallas_call(
        _experts_kernel,
        grid_spec=grid_spec,
        out_shape=jax.ShapeDtypeStruct((total, d), F32),
        compiler_params=pltpu.CompilerParams(dimension_semantics=("arbitrary",),
                                             vmem_limit_bytes=VMEM_LIMIT),
        name="moe_experts",
    )(block_e, nvalid, xd, w_gate, w_up, w_down)


def _combine_kernel(dst_cur_ref, dst_nxt_ref, dst_nx2_ref, x2_ref, route_ref, g_ref, b_ref, yd_hbm,
                    out_ref, ybuf, sems, *, tm, nsteps, alpha):
    i = pl.program_id(0)

    def issue(idx_ref, s):
        for r in range(2 * tm):
            pltpu.make_async_copy(yd_hbm.at[pl.ds(idx_ref[0, 0, r], 1)], ybuf.at[s, pl.ds(r, 1)],
                                  sems.at[s]).start(priority=r % 2)

    def wait(s):
        pltpu.make_async_copy(yd_hbm.at[pl.ds(0, 2 * tm)], ybuf.at[s], sems.at[s]).wait()

    @pl.when(i == 0)
    def _():
        issue(dst_cur_ref, 0)
        issue(dst_nxt_ref, 1)

    for s in range(3):
        @pl.when(i % 3 == s)
        def _():
            wait(s)
            issue(dst_nx2_ref, (s + 2) % 3)
            route = route_ref[...]
            y = ybuf[s, :tm, :] * route[:, 2:3] + ybuf[s, tm:, :] * route[:, 3:4]
            out_ref[...] = _layer_norm(alpha * x2_ref[...] + y, g_ref[...], b_ref[...])

            @pl.when(i == nsteps - 1)
            def _():
                wait((s + 1) % 3)
                wait((s + 2) % 3)


def _moe_combine(x2, route, dst3, yd, ln_g, ln_b, alpha):
    t, d = x2.shape
    tm = TM_OUT
    nsteps = t // tm
    smem_blk = lambda f: pl.BlockSpec((1, 1, 2 * tm), f, memory_space=pltpu.SMEM)
    row = lambda n: pl.BlockSpec((tm, n), lambda i: (i, 0))
    return pl.pallas_call(
        functools.partial(_combine_kernel, tm=tm, nsteps=nsteps, alpha=alpha),
        grid=(nsteps,),
        out_shape=jax.ShapeDtypeStruct((t, d), F32),
        in_specs=[smem_blk(lambda i: (i, 0, 0)),
                  smem_blk(lambda i: (jnp.minimum(i + 1, nsteps - 1), 0, 0)),
                  smem_blk(lambda i: (jnp.minimum(i + 2, nsteps - 1), 0, 0)),
                  row(d), row(LANES), _const_spec(ln_g.shape), _const_spec(ln_b.shape),
                  pl.BlockSpec(memory_space=pl.ANY)],
        out_specs=row(d),
        scratch_shapes=[pltpu.VMEM((3, 2 * tm, d), F32), pltpu.SemaphoreType.DMA((3,))],
        compiler_params=pltpu.CompilerParams(dimension_semantics=("arbitrary",),
                                             vmem_limit_bytes=VMEM_LIMIT),
        name="moe_combine",
    )(dst3, dst3, dst3, x2, route, ln_g, ln_b, yd)


def _prep_layer(w_in, b_in, gm_ln_g, gm_ln_b, gm_w_s, gm_b_s, w_gm_out, mla_q_norm_g, mla_kv_norm_g,
                w_uq, w_uk, w_uv, w_mla_out, w_o, ln1_g, ln1_b, w_mq, w_mk, w_mv, w_mo, ln2_g, ln2_b,
                w_group_router, b_group_router, w_expert_router, b_expert_router, depth):
    d = w_in.shape[0]
    s_v = 2 * GM_WIDTH
    s_q = s_v + MLA_Q_LORA
    s_kv = s_q + MLA_KV_LORA
    s_r = s_kv + MLA_ROPE
    rowv = lambda a: a.reshape(1, -1).astype(F32)

    def lat_cols(a):
        z = jnp.zeros(a.shape[:-1] + (MLA_NOPE,), a.dtype)
        kr = a[..., s_kv:s_r]
        return jnp.concatenate([a[..., s_v:s_kv], z, kr, kr], axis=-1)

    wq3 = w_uq.reshape(MLA_Q_LORA, MLA_HEADS, MLA_NOPE + MLA_ROPE)
    wq_pad = jnp.concatenate([wq3, wq3[..., MLA_NOPE:]], axis=-1).reshape(MLA_Q_LORA, MLA_HEADS * HEAD_PAD)
    wk3 = w_uk.reshape(MLA_KV_LORA, MLA_HEADS, MLA_NOPE)
    wk_pad = jnp.pad(wk3, ((0, 0), (0, 0), (0, HEAD_PAD - MLA_NOPE))).reshape(MLA_KV_LORA, MLA_HEADS * HEAD_PAD)
    wv3 = w_uv.T.reshape(MLA_HEADS, MLA_V, MLA_KV_LORA)
    wv_pad = jnp.pad(wv3, ((0, 0), (0, V_ROWS - MLA_V), (0, 0))).reshape(MLA_HEADS * V_ROWS, MLA_KV_LORA)
    b_v = jnp.zeros((MLA_HEADS, V_ROWS, 1), F32).at[:, MLA_V].set(1.0).reshape(MLA_HEADS * V_ROWS, 1)
    w_r = jnp.zeros((d, LANES), F32)
    w_r = w_r.at[:, :N_EXPERTS].set(w_expert_router).at[:, N_EXPERTS:N_EXPERTS + N_GROUPS].set(w_group_router)
    b_r = jnp.zeros((LANES,), F32)
    b_r = b_r.at[:N_EXPERTS].set(b_expert_router).at[N_EXPERTS:N_EXPERTS + N_GROUPS].set(b_group_router)
    wr_hi = w_r.astype(BF16)
    return dict(
        w_uv=w_in[:, :s_v].astype(BF16), b_uv=rowv(b_in[:s_v]),
        w_gate=w_in[:, s_r:].astype(BF16), b_gate=rowv(b_in[s_r:]),
        w_lat=lat_cols(w_in).astype(BF16), b_lat=rowv(lat_cols(b_in)),
        gm_ln_g=rowv(gm_ln_g), gm_ln_b=rowv(gm_ln_b),
        ws=gm_w_s.astype(BF16), bs_full=jnp.repeat(gm_b_s.T, GM_CHUNK, axis=1).astype(F32),
        w_gm_out=w_gm_out.astype(BF16), qn_g=rowv(mla_q_norm_g), kvn_g=rowv(mla_kv_norm_g),
        w_uq=wq_pad.astype(BF16), w_uk=wk_pad.astype(BF16), w_uvv=wv_pad.astype(BF16), b_v=b_v,
        w_mla_out=w_mla_out.astype(BF16), w_o=w_o.astype(BF16), ln1_g=rowv(ln1_g), ln1_b=rowv(ln1_b),
        w_mq=w_mq.astype(BF16), w_mk=w_mk.astype(BF16), w_mv=w_mv.astype(BF16), w_mo=w_mo.astype(BF16),
        ln2_g=rowv(ln2_g), ln2_b=rowv(ln2_b),
        wr_hi=wr_hi, wr_lo=(w_r - wr_hi.astype(F32)).astype(BF16), br=rowv(b_r),
        alpha=(2 * depth) ** 0.25,
    )


def _dispatch_tables(idx, counts, t):
    rows = MOE_ROWS
    tiles, _, tm = idx.shape
    e = idx[:, 0:2, :].astype(jnp.int32)
    rank = idx[:, 4:6, :].astype(jnp.int32)
    cnt = counts[0, :N_EXPERTS].astype(jnp.int32)
    padded = (cnt + rows - 1) // rows * rows
    ends = jnp.cumsum(padded)
    starts = ends - padded
    ids = jnp.arange(N_EXPERTS, dtype=jnp.int32)[:, None, None, None]
    start_of = jnp.sum(jnp.where(e[None] == ids, starts[:, None, None, None], 0), axis=0)
    dst3 = (start_of + rank).reshape(tiles, 1, 2 * tm)
    total = 2 * t + N_EXPERTS * rows
    block_start = jnp.arange(total // rows, dtype=jnp.int32) * rows
    block_e = jnp.minimum(jnp.sum(ends[None, :] <= block_start[:, None], axis=1), N_EXPERTS - 1)
    nvalid = (ends[-1] // rows).reshape(1).astype(jnp.int32)
    last_start = jnp.maximum(ends - rows, 0).astype(jnp.int32)
    has_rows = (cnt > 0).astype(jnp.int32)
    return dst3, block_e.astype(jnp.int32), nvalid, last_start, has_rows, total


def _layer(x, mem, positions, depth, w_in, b_in, gm_ln_g, gm_ln_b, gm_w_s, gm_b_s, w_gm_out,
           mla_q_norm_g, mla_kv_norm_g, w_uq, w_uk, w_uv, w_mla_out, w_o, ln1_g, ln1_b,
           w_mq, w_mk, w_mv, w_mo, ln2_g, ln2_b,
           w_group_router, b_group_router, w_expert_router, b_expert_router,
           w_exp_gate, w_exp_up, w_exp_down, ln3_g, ln3_b):
    b, s, d = x.shape
    t = b * s
    mem_len = mem.shape[1]
    p = _prep_layer(w_in, b_in, gm_ln_g, gm_ln_b, gm_w_s, gm_b_s, w_gm_out, mla_q_norm_g, mla_kv_norm_g,
                    w_uq, w_uk, w_uv, w_mla_out, w_o, ln1_g, ln1_b, w_mq, w_mk, w_mv, w_mo, ln2_g, ln2_b,
                    w_group_router, b_group_router, w_expert_router, b_expert_router, depth)
    x2d = x.reshape(t, d)
    inv_freq = ROPE_THETA ** (-jnp.arange(ROPE_HALF, dtype=F32) / ROPE_HALF)
    cos, sin = _rope_tables(positions, inv_freq)
    gm, sg, q, k, v = _mixer_in(x2d, cos, sin, p)
    o = _mla_attn(q, k, v, b, s)
    mk, mv = _mem_kv(mem.reshape(b * mem_len, d), p["w_mk"], p["w_mv"], mem_len)
    x2, route, idx, counts = _post_mixer(x2d, o, gm, sg, mk, mv, p, s, mem_len)
    dst3, block_e, nvalid, last_start, has_rows, total = _dispatch_tables(idx, counts, t)
    xd = _moe_dispatch(x2, dst3, last_start, has_rows, nvalid, total)
    yd = _moe_experts(xd, block_e, nvalid, w_exp_gate, w_exp_up, w_exp_down)
    out = _moe_combine(x2, route, dst3, yd, ln3_g.reshape(1, d), ln3_b.reshape(1, d), p["alpha"])
    return out.reshape(b, s, d)


def kernel(x, mem, positions, w_in, b_in, gm_ln_g, gm_ln_b, gm_w_s, gm_b_s, w_gm_out, mla_q_norm_g, mla_kv_norm_g, w_uq, w_uk, w_uv, w_mla_out, w_o, ln1_g, ln1_b, w_mq, w_mk, w_mv, w_mo, ln2_g, ln2_b, w_group_router, b_group_router, w_expert_router, b_expert_router, w_exp_gate, w_exp_up, w_exp_down, ln3_g, ln3_b):
    depth = w_in.shape[0]
    per_layer = (w_in, b_in, gm_ln_g, gm_ln_b, gm_w_s, gm_b_s, w_gm_out, mla_q_norm_g, mla_kv_norm_g,
                 w_uq, w_uk, w_uv, w_mla_out, w_o, ln1_g, ln1_b, w_mq, w_mk, w_mv, w_mo, ln2_g, ln2_b,
                 w_group_router, b_group_router, w_expert_router, b_expert_router,
                 w_exp_gate, w_exp_up, w_exp_down, ln3_g, ln3_b)
    h = x
    for l in range(depth):
        h = _layer(h, mem, positions, depth, *[w[l] for w in per_layer])
    return h
```

```python
import functools
import math

import jax
import jax.numpy as jnp
from jax import lax
from jax.experimental import pallas as pl
from jax.experimental.pallas import tpu as pltpu

F32 = jnp.float32
BF16 = jnp.bfloat16

GM_WIDTH = 1024
GM_GROUPS = 8
GM_CHUNK = 128
MLA_HEADS = 16
MLA_NOPE = 64
MLA_ROPE = 32
MLA_V = 64
MLA_Q_LORA = 384
MLA_KV_LORA = 256
ROPE_THETA = 10000.0
MEM_HEADS = 4
N_GROUPS = 8
EXPERTS_PER_GROUP = 8
N_EXPERTS = N_GROUPS * EXPERTS_PER_GROUP
D_EXPERT = 256
LN_EPS = 1e-5
RMS_EPS = 1e-6

LANES = 128
HEAD_PAD = 128
V_ROWS = 80
ROPE_HALF = MLA_ROPE // 2

TM_IN = 512
TM_SUB = 256
TQ = 1024
TK = 256
ATTN_HEADS = 2
SM_CHUNK = 16
TM_POST = 1024
MOE_ROWS = 512
TM_OUT = 256
VMEM_LIMIT = 56 * 1024 * 1024


def _const_spec(shape):
    nd = len(shape)
    return pl.BlockSpec(shape, lambda *_: (0,) * nd, pipeline_mode=pl.Buffered(1))


def _layer_norm(x, g, b):
    mu = jnp.mean(x, axis=-1, keepdims=True)
    xc = x - mu
    var = jnp.mean(xc * xc, axis=-1, keepdims=True)
    return xc * lax.rsqrt(var + LN_EPS) * g + b


def _rms_norm(x, g):
    return x * lax.rsqrt(jnp.mean(x * x, axis=-1, keepdims=True) + RMS_EPS) * g


def _gelu(x):
    return x * (lax.erf(x * (1.0 / math.sqrt(2.0))) + 1.0) * 0.5


def _dot(a, b):
    return jnp.dot(a, b, preferred_element_type=F32)


def _dot_nt(a, b):
    return lax.dot_general(a, b, (((1,), (1,)), ((), ())), preferred_element_type=F32)


def _rope_kernel(inv_ref, pos_ref, cos_ref, sin_ref):
    p = pos_ref[...].astype(F32)
    for j in range(ROPE_HALF):
        ang = p * inv_ref[j]
        cos_ref[j] = jnp.cos(ang)
        sin_ref[j] = jnp.sin(ang)


def _rope_tables(positions, inv_freq):
    t = positions.size
    rows = t // LANES
    pos2 = positions.reshape(rows, LANES)
    cos, sin = pl.pallas_call(
        _rope_kernel,
        out_shape=(jax.ShapeDtypeStruct((ROPE_HALF, rows, LANES), F32),) * 2,
        in_specs=[pl.BlockSpec(memory_space=pltpu.SMEM),
                  pl.BlockSpec((rows, LANES), lambda: (0, 0))],
        out_specs=(pl.BlockSpec((ROPE_HALF, rows, LANES), lambda: (0, 0, 0)),) * 2,
        name="rope_tables",
    )(inv_freq, pos2)
    cos = jnp.tile(cos.reshape(ROPE_HALF, t).T, (1, LANES // ROPE_HALF))
    sin = jnp.tile(sin.reshape(ROPE_HALF, t).T, (1, LANES // ROPE_HALF))
    return cos, sin


def _mixer_in_kernel(x_ref, cos_ref, sin_ref, w_uv_ref, b_uv_ref, w_gate_ref, b_gate_ref,
                     w_lat_ref, b_lat_ref, ln_g_ref, ln_b_ref, ws_ref, bs_ref, w_gm_out_ref,
                     qn_g_ref, kvn_g_ref, w_uq_ref, w_uk_ref, w_uvv_ref, b_v_ref,
                     gm_ref, sg_ref, q_ref, k_ref, v_ref, mixed_ref, *, tm, sub, scale):
    parts = range(tm // sub)
    rows = [slice(part * sub, (part + 1) * sub) for part in parts]
    xb = [x_ref[r, :].astype(BF16) for r in rows]

    uv = [_gelu(_dot(x, w_uv_ref[...]) + b_uv_ref[...]) for x in xb]
    vn = [_layer_norm(t[:, GM_WIDTH:], ln_g_ref[...], ln_b_ref[...]).astype(BF16) for t in uv]
    r_i = lax.broadcasted_iota(jnp.int32, (GM_CHUNK, GM_CHUNK), 0)
    c_i = lax.broadcasted_iota(jnp.int32, (GM_CHUNK, GM_CHUNK), 1)
    causal = c_i <= r_i
    for g in range(GM_GROUPS):
        w = jnp.where(causal, ws_ref[g], jnp.zeros_like(ws_ref[g]))
        cols = slice(g * GM_CHUNK, (g + 1) * GM_CHUNK)
        for part in parts:
            for ch in range(sub // GM_CHUNK):
                local = slice(ch * GM_CHUNK, (ch + 1) * GM_CHUNK)
                dest = slice(part * sub + ch * GM_CHUNK, part * sub + (ch + 1) * GM_CHUNK)
                mixed_ref[dest, cols] = _dot(w, vn[part][local, cols]) + bs_ref[:, cols]
    gated = [(t[:, :GM_WIDTH] * mixed_ref[r, :]).astype(BF16) for t, r in zip(uv, rows)]
    y_gm = [_dot(t, w_gm_out_ref[...]) for t in gated]
    gates = [_dot(x, w_gate_ref[...]) + b_gate_ref[...] for x in xb]
    for r, gt, y in zip(rows, gates, y_gm):
        gm_ref[r, :] = (jax.nn.sigmoid(gt[:, :GM_WIDTH]) * y).astype(BF16)
        sg_ref[r, :] = jax.nn.sigmoid(gt[:, GM_WIDTH:]).astype(BF16)

    lat = [_dot(x, w_lat_ref[...]) + b_lat_ref[...] for x in xb]
    cq = [_rms_norm(t[:, :MLA_Q_LORA], qn_g_ref[...]).astype(BF16) for t in lat]
    ckv = [_rms_norm(t[:, MLA_Q_LORA:MLA_Q_LORA + MLA_KV_LORA], kvn_g_ref[...]).astype(BF16) for t in lat]
    qa = [_dot(t, w_uq_ref[...]) for t in cq]
    kn = [_dot(t, w_uk_ref[...]) for t in ckv]

    lane = lax.broadcasted_iota(jnp.int32, (sub, HEAD_PAD), 1)
    in_rope = (lane >= MLA_NOPE) & (lane < MLA_NOPE + MLA_ROPE)
    for part in parts:
        r = rows[part]
        cosv = cos_ref[r, :]
        sinv = sin_ref[r, :]
        cfac = jnp.where(lane < MLA_NOPE, 1.0, jnp.where(in_rope, cosv, 0.0))
        sfac = jnp.where(in_rope, jnp.where(lane < MLA_NOPE + ROPE_HALF, -sinv, sinv), 0.0)

        def rot(t, cfac=cfac, sfac=sfac):
            return t * cfac + pltpu.roll(t, HEAD_PAD - ROPE_HALF, 1) * sfac

        kr_rot = rot(lat[part][:, MLA_Q_LORA + MLA_KV_LORA:])
        for h in range(MLA_HEADS):
            cols = slice(h * HEAD_PAD, (h + 1) * HEAD_PAD)
            q_ref[r, cols] = (rot(qa[part][:, cols]) * scale).astype(BF16)
            k_ref[r, cols] = (kn[part][:, cols] + kr_rot).astype(BF16)
        v_ref[:, r] = (_dot_nt(w_uvv_ref[...], ckv[part]) + b_v_ref[...]).astype(BF16)


def _mixer_in(x2d, cos, sin, p):
    t, d = x2d.shape
    tm = TM_IN
    scale = (MLA_NOPE + MLA_ROPE) ** -0.5 * math.log2(math.e)
    row = lambda n: pl.BlockSpec((tm, n), lambda i: (i, 0))
    weights = [p["w_uv"], p["b_uv"], p["w_gate"], p["b_gate"], p["w_lat"], p["b_lat"],
               p["gm_ln_g"], p["gm_ln_b"], p["ws"], p["bs_full"], p["w_gm_out"],
               p["qn_g"], p["kvn_g"], p["w_uq"], p["w_uk"], p["w_uvv"], p["b_v"]]
    qk_w = MLA_HEADS * HEAD_PAD
    v_w = MLA_HEADS * V_ROWS
    return pl.pallas_call(
        functools.partial(_mixer_in_kernel, tm=tm, sub=TM_SUB, scale=scale),
        grid=(t // tm,),
        out_shape=(jax.ShapeDtypeStruct((t, d), BF16), jax.ShapeDtypeStruct((t, d), BF16),
                   jax.ShapeDtypeStruct((t, qk_w), BF16), jax.ShapeDtypeStruct((t, qk_w), BF16),
                   jax.ShapeDtypeStruct((v_w, t), BF16)),
        in_specs=[row(d), row(LANES), row(LANES)] + [_const_spec(w.shape) for w in weights],
        out_specs=(row(d), row(d), row(qk_w), row(qk_w), pl.BlockSpec((v_w, tm), lambda i: (0, i))),
        scratch_shapes=[pltpu.VMEM((tm, GM_WIDTH), F32)],
        compiler_params=pltpu.CompilerParams(dimension_semantics=("arbitrary",),
                                             vmem_limit_bytes=VMEM_LIMIT),
        name="mixer_in",
    )(x2d, cos, sin, *weights)


def _attn_kernel(q_ref, k_ref, vt_ref, o_ref, acc_ref, m_ref, r0_ref, r1_ref, s0_ref, s1_ref, p0_ref, p1_ref,
                 *, tq, tk):
    i = pl.program_id(2)
    acc_ref[...] = jnp.zeros_like(acc_ref)
    m_ref[...] = jnp.full(m_ref.shape, -jnp.inf, F32)

    every = slice(0, tq)

    def scores(j, s_ref, qs=every):
        kb = k_ref[pl.ds(pl.multiple_of(j * tk, tk), tk), :]
        for a in range(ATTN_HEADS):
            cols = slice(a * HEAD_PAD, (a + 1) * HEAD_PAD)
            s_ref[a, :, qs] = _dot_nt(kb[:, cols], q_ref[qs, cols])

    def softmax(s_ref, p_ref, r_ref, qs=every, diagonal=False):
        nch = tk // SM_CHUNK
        width = qs.stop - qs.start
        if diagonal:
            q_loc = lax.broadcasted_iota(jnp.int32, (SM_CHUNK, width), 1)
            k_loc = lax.broadcasted_iota(jnp.int32, (SM_CHUNK, width), 0)

        def chunk(a, c):
            s = s_ref[a, c * SM_CHUNK:(c + 1) * SM_CHUNK, qs]
            if diagonal:
                s = jnp.where(k_loc + c * SM_CHUNK <= q_loc, s, -jnp.inf)
            return s

        for a in range(ATTN_HEADS):
            m = m_ref[a:a + 1, qs]
            cm = chunk(a, 0)
            for c in range(1, nch):
                cm = jnp.maximum(cm, chunk(a, c))
            m_new = jnp.maximum(m, jnp.max(cm, axis=0, keepdims=True))
            r_ref[a:a + 1, qs] = jnp.exp2(m - m_new)
            m_ref[a:a + 1, qs] = m_new
            mb = jnp.broadcast_to(m_new, (SM_CHUNK, width))
            for c in range(nch):
                p_ref[a, c * SM_CHUNK:(c + 1) * SM_CHUNK, qs] = jnp.exp2(chunk(a, c) - mb).astype(BF16)

    def accumulate(j, p_ref, r_ref, qs=every):
        vb = vt_ref[:, pl.ds(pl.multiple_of(j * tk, tk), tk)]
        for a in range(ATTN_HEADS):
            rows = slice(a * V_ROWS, (a + 1) * V_ROWS)
            alpha = r_ref[a:a + 1, qs]
            acc_ref[rows, qs] = acc_ref[rows, qs] * alpha + _dot(vb[rows, :], p_ref[a, :, qs])

    def pair(base):
        scores(base + 1, s1_ref)
        softmax(s0_ref, p0_ref, r0_ref)
        accumulate(base, p0_ref, r0_ref)
        scores(base + 2, s0_ref)
        softmax(s1_ref, p1_ref, r1_ref)
        accumulate(base + 1, p1_ref, r1_ref)

    def main(t, carry):
        pair(4 * t)
        pair(4 * t + 2)
        return carry

    nsub = tq // tk
    first = nsub * i
    scores(0, s0_ref)
    lax.fori_loop(0, first // 4, main, 0)
    if nsub % 4:
        @pl.when(first % 4 == 2)
        def _():
            pair(first - 2)

    bufs = ((s0_ref, p0_ref, r0_ref), (s1_ref, p1_ref, r1_ref))
    for j in range(nsub):
        s_ref, p_ref, r_ref = bufs[j % 2]
        if j + 1 < nsub:
            scores(first + j + 1, bufs[(j + 1) % 2][0], slice((j + 1) * tk, tq))
        softmax(s_ref, p_ref, r_ref, slice(j * tk, (j + 1) * tk), diagonal=True)
        if j + 1 < nsub:
            softmax(s_ref, p_ref, r_ref, slice((j + 1) * tk, tq))
        accumulate(first + j, p_ref, r_ref, slice(j * tk, tq))
    out = [acc_ref[a * V_ROWS:a * V_ROWS + MLA_V, :] / acc_ref[a * V_ROWS + MLA_V:a * V_ROWS + MLA_V + 1, :]
           for a in range(ATTN_HEADS)]
    o_ref[...] = jnp.concatenate(out, axis=0).T.astype(BF16)


def _mla_attn(q, k, vt, batch, seq):
    assert TQ % (2 * TK) == 0, "visible key blocks are walked in pairs"
    nq = seq // TQ
    nh = ATTN_HEADS
    return pl.pallas_call(
        functools.partial(_attn_kernel, tq=TQ, tk=TK),
        grid=(batch, MLA_HEADS // nh, nq),
        out_shape=jax.ShapeDtypeStruct((batch * seq, MLA_HEADS * MLA_V), BF16),
        in_specs=[pl.BlockSpec((TQ, nh * HEAD_PAD), lambda b, h, i: (b * nq + i, h)),
                  pl.BlockSpec((seq, nh * HEAD_PAD), lambda b, h, i: (b, h)),
                  pl.BlockSpec((nh * V_ROWS, seq), lambda b, h, i: (h, b))],
        out_specs=pl.BlockSpec((TQ, nh * MLA_V), lambda b, h, i: (b * nq + i, h)),
        scratch_shapes=[pltpu.VMEM((nh * V_ROWS, TQ), F32)] + [pltpu.VMEM((8, TQ), F32)] * 3 + [
                        pltpu.VMEM((nh, TK, TQ), F32), pltpu.VMEM((nh, TK, TQ), F32),
                        pltpu.VMEM((nh, TK, TQ), BF16), pltpu.VMEM((nh, TK, TQ), BF16)],
        compiler_params=pltpu.CompilerParams(
            dimension_semantics=("arbitrary", "arbitrary", "arbitrary"), vmem_limit_bytes=VMEM_LIMIT),
        name="mla_attn",
    )(q, k, vt)


def _mem_kv_kernel(mem_ref, wk_ref, wv_ref, k_ref, v_ref):
    mb = mem_ref[...].astype(BF16)
    k_ref[...] = _dot(mb, wk_ref[...]).astype(BF16)
    v_ref[...] = _dot(mb, wv_ref[...]).astype(BF16)


def _mem_kv(mem2d, w_mk, w_mv, mem_len):
    rows, d = mem2d.shape
    blk = pl.BlockSpec((mem_len, d), lambda i: (i, 0))
    return pl.pallas_call(
        _mem_kv_kernel,
        grid=(rows // mem_len,),
        out_shape=(jax.ShapeDtypeStruct((rows, d), BF16),) * 2,
        in_specs=[blk, _const_spec(w_mk.shape), _const_spec(w_mv.shape)],
        out_specs=(blk, blk),
        compiler_params=pltpu.CompilerParams(dimension_semantics=("arbitrary",)),
        name="mem_kv",
    )(mem2d, w_mk, w_mv)


def _post_mixer_kernel(x_ref, o_ref, gm_ref, sg_ref, mk_ref, mv_ref,
                       w_mla_out_ref, w_o_ref, ln1_g_ref, ln1_b_ref,
                       w_mq_ref, w_mo_ref, ln2_g_ref, ln2_b_ref,
                       wr_hi_ref, wr_lo_ref, br_ref,
                       x2_ref, route_ref, idx_ref, counts_ref, run_ref, *, tm, sub, alpha, mem_scale):
    i = pl.program_id(0)

    @pl.when(i == 0)
    def _():
        run_ref[...] = jnp.zeros_like(run_ref)

    parts = range(tm // sub)
    rows = [slice(part * sub, (part + 1) * sub) for part in parts]
    y_mla = [_dot(o_ref[r, :], w_mla_out_ref[...]) for r in rows]
    merged = [(gm_ref[r, :].astype(F32) + sg_ref[r, :].astype(F32) * y).astype(BF16) for r, y in zip(rows, y_mla)]
    mixed = [_dot(m, w_o_ref[...]) for m in merged]
    x1 = [_layer_norm(alpha * x_ref[r, :] + h, ln1_g_ref[...], ln1_b_ref[...]) for r, h in zip(rows, mixed)]

    qm = [(_dot(x.astype(BF16), w_mq_ref[...]) * mem_scale).astype(BF16) for x in x1]
    hd = mk_ref.shape[1] // MEM_HEADS
    heads = [[] for _ in parts]
    for h in range(MEM_HEADS):
        cols = slice(h * hd, (h + 1) * hd)
        s = [_dot_nt(q[:, cols], mk_ref[:, cols]) for q in qm]
        pr = [jnp.exp(v - jnp.max(v, axis=1, keepdims=True)) for v in s]
        for part in parts:
            heads[part].append(_dot(pr[part].astype(BF16), mv_ref[:, cols])
                               / jnp.sum(pr[part], axis=1, keepdims=True))
    om = [jnp.concatenate(hs, axis=1).astype(BF16) for hs in heads]
    mem_out = [_dot(v, w_mo_ref[...]) for v in om]
    x2 = [_layer_norm(alpha * x + h, ln2_g_ref[...], ln2_b_ref[...]) for x, h in zip(x1, mem_out)]
    for r, v in zip(rows, x2):
        x2_ref[r, :] = v

    x_hi = [v.astype(BF16) for v in x2]
    x_lo = [(v - hi.astype(F32)).astype(BF16) for v, hi in zip(x2, x_hi)]
    logit = [_dot(hi, wr_hi_ref[...]) + _dot(lo, wr_hi_ref[...]) + _dot(hi, wr_lo_ref[...]) + br_ref[...]
             for hi, lo in zip(x_hi, x_lo)]
    lane = lax.broadcasted_iota(jnp.int32, (sub, LANES), 1).astype(F32)
    big = jnp.float32(1e9)
    r_i = lax.broadcasted_iota(jnp.int32, (sub, sub), 0)
    c_i = lax.broadcasted_iota(jnp.int32, (sub, sub), 1)
    tri = jnp.where(c_i < r_i, 1.0, 0.0).astype(BF16)

    def first_argmax(vals, vmax):
        return jnp.min(jnp.where(vals == vmax, lane, big), axis=1, keepdims=True)

    for part in parts:
        logits = logit[part]
        g_mask = (lane >= N_EXPERTS) & (lane < N_EXPERTS + N_GROUPS)
        lg = jnp.where(g_mask, logits, -jnp.inf)
        g_max = jnp.max(lg, axis=1, keepdims=True)
        g_sel = first_argmax(lg, g_max) - N_EXPERTS
        g_w = 1.0 / jnp.sum(jnp.where(g_mask, jnp.exp(logits - g_max), 0.0), axis=1, keepdims=True)
        in_group = jnp.floor(lane * (1.0 / EXPERTS_PER_GROUP)) == g_sel
        le = jnp.where(in_group, logits, -jnp.inf)
        v1 = jnp.max(le, axis=1, keepdims=True)
        e1 = first_argmax(le, v1)
        le2 = jnp.where(lane == e1, -jnp.inf, le)
        v2 = jnp.max(le2, axis=1, keepdims=True)
        e2 = first_argmax(le2, v2)
        t2 = jnp.exp(v2 - v1)
        w1 = (1.0 / (1.0 + t2)) * g_w
        w2 = (t2 / (1.0 + t2)) * g_w

        hit1 = lane == e1
        hit2 = lane == e2
        onehot = jnp.where(hit1 | hit2, 1.0, 0.0)
        before = _dot(tri, onehot.astype(BF16)) + run_ref[...]
        r1 = jnp.sum(jnp.where(hit1, before, 0.0), axis=1, keepdims=True)
        r2 = jnp.sum(jnp.where(hit2, before, 0.0), axis=1, keepdims=True)
        run_ref[...] = run_ref[...] + jnp.sum(onehot, axis=0, keepdims=True)

        packed = jnp.zeros((sub, LANES), F32)
        for pos, val in enumerate((e1, e2, w1, w2, r1, r2)):
            packed = jnp.where(lane == pos, val, packed)
        route_ref[rows[part], :] = packed
        idx_ref[part] = packed.T[:8, :]
    counts_ref[...] = jnp.broadcast_to(run_ref[...], counts_ref.shape)


def _post_mixer(x2d, o, gm, sg, mk, mv, p, seq, mem_len):
    t, d = x2d.shape
    tm, sub = TM_POST, TM_OUT
    assert seq % tm == 0, "a post_mixer row tile must not straddle two sequences (memory blocks are per sequence)"
    per_batch = seq // tm
    row = lambda n: pl.BlockSpec((tm, n), lambda i: (i, 0))
    memblk = pl.BlockSpec((mem_len, d), lambda i: (i // per_batch, 0))
    weights = [p["w_mla_out"], p["w_o"], p["ln1_g"], p["ln1_b"], p["w_mq"], p["w_mo"],
               p["ln2_g"], p["ln2_b"], p["wr_hi"], p["wr_lo"], p["br"]]
    return pl.pallas_call(
        functools.partial(_post_mixer_kernel, tm=tm, sub=sub, alpha=p["alpha"],
                          mem_scale=(d // MEM_HEADS) ** -0.5),
        grid=(t // tm,),
        out_shape=(jax.ShapeDtypeStruct((t, d), F32), jax.ShapeDtypeStruct((t, LANES), F32),
                   jax.ShapeDtypeStruct((t // sub, 8, sub), F32), jax.ShapeDtypeStruct((8, LANES), F32)),
        in_specs=[row(d), row(d), row(d), row(d), memblk, memblk]
                 + [_const_spec(w.shape) for w in weights],
        out_specs=(row(d), row(LANES), pl.BlockSpec((tm // sub, 8, sub), lambda i: (i, 0, 0)),
                   pl.BlockSpec((8, LANES), lambda i: (0, 0))),
        scratch_shapes=[pltpu.VMEM((1, LANES), F32)],
        compiler_params=pltpu.CompilerParams(dimension_semantics=("arbitrary",),
                                             vmem_limit_bytes=VMEM_LIMIT),
        name="post_mixer",
    )(x2d, o, gm, sg, mk, mv, *weights)


def _dispatch_kernel(last_ref, has_ref, nvalid_ref, dst_ref, x_hbm, xd_hbm, zbuf, xbuf, zsem, lsems, ssems,
                     *, tm, nsteps, rows, nblocks):
    i = pl.program_id(0)

    def load(step, s):
        return pltpu.make_async_copy(x_hbm.at[pl.ds(pl.multiple_of(step * tm, tm), tm)], xbuf.at[s], lsems.at[s])

    def drain(s):
        pltpu.make_async_copy(xd_hbm.at[pl.ds(0, 2 * tm)], xd_hbm.at[pl.ds(0, 2 * tm)], ssems.at[s]).wait()

    @pl.when(i == 0)
    def _():
        load(0, 0).start()
        zbuf[...] = jnp.zeros_like(zbuf)

        def zero_copy(start):
            return pltpu.make_async_copy(zbuf, xd_hbm.at[pl.ds(pl.multiple_of(start, rows), rows)], zsem)

        for e in range(N_EXPERTS):
            @pl.when(has_ref[e] > 0)
            def _():
                zero_copy(last_ref[e]).start()

        def start_tail(blk, carry):
            zero_copy(blk * rows).start()
            return carry

        def wait_tail(blk, carry):
            zero_copy(blk * rows).wait()
            return carry

        lax.fori_loop(nvalid_ref[0], nblocks, start_tail, 0)
        for e in range(N_EXPERTS):
            @pl.when(has_ref[e] > 0)
            def _():
                zero_copy(last_ref[e]).wait()

        lax.fori_loop(nvalid_ref[0], nblocks, wait_tail, 0)

    for s in range(3):
        @pl.when(i % 3 == s)
        def _():
            nxt = (s + 1) % 3
            load(i, s).wait()

            @pl.when(i >= 2)
            def _():
                drain(nxt)

            @pl.when(i + 1 < nsteps)
            def _():
                load(i + 1, nxt).start()

            for k in range(2):
                for r in range(tm):
                    pltpu.make_async_copy(xbuf.at[s, pl.ds(r, 1)],
                                          xd_hbm.at[pl.ds(dst_ref[0, 0, k * tm + r], 1)],
                                          ssems.at[s]).start(priority=r % 2)

            @pl.when(i == nsteps - 1)
            def _():
                if nsteps >= 2:
                    drain((s + 2) % 3)
                drain(s)


def _moe_dispatch(x2, dst3, last_start, has_rows, nvalid, total_rows):
    t, d = x2.shape
    tm = TM_OUT
    nsteps = t // tm
    assert dst3.shape == (nsteps, 1, 2 * tm), "post_mixer and dispatch/combine tiles must coincide"
    grid_spec = pltpu.PrefetchScalarGridSpec(
        num_scalar_prefetch=3,
        grid=(nsteps,),
        in_specs=[pl.BlockSpec((1, 1, 2 * tm), lambda i, la, ha, nv: (i, 0, 0), memory_space=pltpu.SMEM),
                  pl.BlockSpec(memory_space=pl.ANY)],
        out_specs=pl.BlockSpec(memory_space=pl.ANY),
        scratch_shapes=[pltpu.VMEM((MOE_ROWS, d), F32), pltpu.VMEM((3, tm, d), F32),
                        pltpu.SemaphoreType.DMA(()), pltpu.SemaphoreType.DMA((3,)),
                        pltpu.SemaphoreType.DMA((3,))],
    )
    return pl.pallas_call(
        functools.partial(_dispatch_kernel, tm=tm, nsteps=nsteps, rows=MOE_ROWS,
                          nblocks=total_rows // MOE_ROWS),
        grid_spec=grid_spec,
        out_shape=jax.ShapeDtypeStruct((total_rows, d), F32),
        compiler_params=pltpu.CompilerParams(dimension_semantics=("arbitrary",), has_side_effects=True),
        name="moe_dispatch",
    )(last_start, has_rows, nvalid, dst3, x2)


def _experts_kernel(be_ref, nvalid_ref, xd_ref, wg_ref, wu_ref, wd_ref, yd_ref, wgu_bf, wd_bf):
    i = pl.program_id(0)
    changed = jnp.logical_or(i == 0, be_ref[i] != be_ref[jnp.maximum(i - 1, 0)])

    @pl.when(changed)
    def _():
        wgu_bf[:, :D_EXPERT] = wg_ref[...].astype(BF16)
        wgu_bf[:, D_EXPERT:] = wu_ref[...].astype(BF16)
        wd_bf[...] = wd_ref[...].astype(BF16)

    @pl.when(i < nvalid_ref[0])
    def _():
        gu = _dot(xd_ref[...].astype(BF16), wgu_bf[...])
        gate = gu[:, :D_EXPERT]
        hidden = (gate * jax.nn.sigmoid(gate) * gu[:, D_EXPERT:]).astype(BF16)
        yd_ref[...] = _dot(hidden, wd_bf[...])

    @pl.when(i >= nvalid_ref[0])
    def _():
        yd_ref[...] = jnp.zeros_like(yd_ref)


def _moe_experts(xd, block_e, nvalid, w_gate, w_up, w_down):
    total, d = xd.shape
    rows = MOE_ROWS
    nblocks = total // rows
    grid_spec = pltpu.PrefetchScalarGridSpec(
        num_scalar_prefetch=2,
        grid=(nblocks,),
        in_specs=[pl.BlockSpec((rows, d), lambda i, be, nv: (jnp.minimum(i, nv[0] - 1), 0)),
                  pl.BlockSpec((None, d, D_EXPERT), lambda i, be, nv: (be[i], 0, 0)),
                  pl.BlockSpec((None, d, D_EXPERT), lambda i, be, nv: (be[i], 0, 0)),
                  pl.BlockSpec((None, D_EXPERT, d), lambda i, be, nv: (be[i], 0, 0))],
        out_specs=pl.BlockSpec((rows, d), lambda i, be, nv: (i, 0)),
        scratch_shapes=[pltpu.VMEM((d, 2 * D_EXPERT), BF16), pltpu.VMEM((D_EXPERT, d), BF16)],
    )
    return pl.pallas_call(
        _experts_kernel,
        grid_spec=grid_spec,
        out_shape=jax.ShapeDtypeStruct((total, d), F32),
        compiler_params=pltpu.CompilerParams(dimension_semantics=("arbitrary",),
                                             vmem_limit_bytes=VMEM_LIMIT),
        name="moe_experts",
    )(block_e, nvalid, xd, w_gate, w_up, w_down)


def _combine_kernel(dst_cur_ref, dst_nxt_ref, dst_nx2_ref, x2_ref, route_ref, g_ref, b_ref, yd_hbm,
                    out_ref, ybuf, sems, *, tm, nsteps, alpha):
    i = pl.program_id(0)

    def issue(idx_ref, s):
        for r in range(2 * tm):
            pltpu.make_async_copy(yd_hbm.at[pl.ds(idx_ref[0, 0, r], 1)], ybuf.at[s, pl.ds(r, 1)],
                                  sems.at[s]).start(priority=r % 2)

    def wait(s):
        pltpu.make_async_copy(yd_hbm.at[pl.ds(0, 2 * tm)], ybuf.at[s], sems.at[s]).wait()

    @pl.when(i == 0)
    def _():
        issue(dst_cur_ref, 0)
        issue(dst_nxt_ref, 1)

    for s in range(3):
        @pl.when(i % 3 == s)
        def _():
            wait(s)
            issue(dst_nx2_ref, (s + 2) % 3)
            route = route_ref[...]
            y = ybuf[s, :tm, :] * route[:, 2:3] + ybuf[s, tm:, :] * route[:, 3:4]
            out_ref[...] = _layer_norm(alpha * x2_ref[...] + y, g_ref[...], b_ref[...])

            @pl.when(i == nsteps - 1)
            def _():
                wait((s + 1) % 3)
                wait((s + 2) % 3)


def _moe_combine(x2, route, dst3, yd, ln_g, ln_b, alpha):
    t, d = x2.shape
    tm = TM_OUT
    nsteps = t // tm
    smem_blk = lambda f: pl.BlockSpec((1, 1, 2 * tm), f, memory_space=pltpu.SMEM)
    row = lambda n: pl.BlockSpec((tm, n), lambda i: (i, 0))
    return pl.pallas_call(
        functools.partial(_combine_kernel, tm=tm, nsteps=nsteps, alpha=alpha),
        grid=(nsteps,),
        out_shape=jax.ShapeDtypeStruct((t, d), F32),
        in_specs=[smem_blk(lambda i: (i, 0, 0)),
                  smem_blk(lambda i: (jnp.minimum(i + 1, nsteps - 1), 0, 0)),
                  smem_blk(lambda i: (jnp.minimum(i + 2, nsteps - 1), 0, 0)),
                  row(d), row(LANES), _const_spec(ln_g.shape), _const_spec(ln_b.shape),
                  pl.BlockSpec(memory_space=pl.ANY)],
        out_specs=row(d),
        scratch_shapes=[pltpu.VMEM((3, 2 * tm, d), F32), pltpu.SemaphoreType.DMA((3,))],
        compiler_params=pltpu.CompilerParams(dimension_semantics=("arbitrary",),
                                             vmem_limit_bytes=VMEM_LIMIT),
        name="moe_combine",
    )(dst3, dst3, dst3, x2, route, ln_g, ln_b, yd)


def _prep_layer(w_in, b_in, gm_ln_g, gm_ln_b, gm_w_s, gm_b_s, w_gm_out, mla_q_norm_g, mla_kv_norm_g,
                w_uq, w_uk, w_uv, w_mla_out, w_o, ln1_g, ln1_b, w_mq, w_mk, w_mv, w_mo, ln2_g, ln2_b,
                w_group_router, b_group_router, w_expert_router, b_expert_router, depth):
    d = w_in.shape[0]
    s_v = 2 * GM_WIDTH
    s_q = s_v + MLA_Q_LORA
    s_kv = s_q + MLA_KV_LORA
    s_r = s_kv + MLA_ROPE
    rowv = lambda a: a.reshape(1, -1).astype(F32)

    def lat_cols(a):
        z = jnp.zeros(a.shape[:-1] + (MLA_NOPE,), a.dtype)
        kr = a[..., s_kv:s_r]
        return jnp.concatenate([a[..., s_v:s_kv], z, kr, kr], axis=-1)

    wq3 = w_uq.reshape(MLA_Q_LORA, MLA_HEADS, MLA_NOPE + MLA_ROPE)
    wq_pad = jnp.concatenate([wq3, wq3[..., MLA_NOPE:]], axis=-1).reshape(MLA_Q_LORA, MLA_HEADS * HEAD_PAD)
    wk3 = w_uk.reshape(MLA_KV_LORA, MLA_HEADS, MLA_NOPE)
    wk_pad = jnp.pad(wk3, ((0, 0), (0, 0), (0, HEAD_PAD - MLA_NOPE))).reshape(MLA_KV_LORA, MLA_HEADS * HEAD_PAD)
    wv3 = w_uv.T.reshape(MLA_HEADS, MLA_V, MLA_KV_LORA)
    wv_pad = jnp.pad(wv3, ((0, 0), (0, V_ROWS - MLA_V), (0, 0))).reshape(MLA_HEADS * V_ROWS, MLA_KV_LORA)
    b_v = jnp.zeros((MLA_HEADS, V_ROWS, 1), F32).at[:, MLA_V].set(1.0).reshape(MLA_HEADS * V_ROWS, 1)
    w_r = jnp.zeros((d, LANES), F32)
    w_r = w_r.at[:, :N_EXPERTS].set(w_expert_router).at[:, N_EXPERTS:N_EXPERTS + N_GROUPS].set(w_group_router)
    b_r = jnp.zeros((LANES,), F32)
    b_r = b_r.at[:N_EXPERTS].set(b_expert_router).at[N_EXPERTS:N_EXPERTS + N_GROUPS].set(b_group_router)
    wr_hi = w_r.astype(BF16)
    return dict(
        w_uv=w_in[:, :s_v].astype(BF16), b_uv=rowv(b_in[:s_v]),
        w_gate=w_in[:, s_r:].astype(BF16), b_gate=rowv(b_in[s_r:]),
        w_lat=lat_cols(w_in).astype(BF16), b_lat=rowv(lat_cols(b_in)),
        gm_ln_g=rowv(gm_ln_g), gm_ln_b=rowv(gm_ln_b),
        ws=gm_w_s.astype(BF16), bs_full=jnp.repeat(gm_b_s.T, GM_CHUNK, axis=1).astype(F32),
        w_gm_out=w_gm_out.astype(BF16), qn_g=rowv(mla_q_norm_g), kvn_g=rowv(mla_kv_norm_g),
        w_uq=wq_pad.astype(BF16), w_uk=wk_pad.astype(BF16), w_uvv=wv_pad.astype(BF16), b_v=b_v,
        w_mla_out=w_mla_out.astype(BF16), w_o=w_o.astype(BF16), ln1_g=rowv(ln1_g), ln1_b=rowv(ln1_b),
        w_mq=w_mq.astype(BF16), w_mk=w_mk.astype(BF16), w_mv=w_mv.astype(BF16), w_mo=w_mo.astype(BF16),
        ln2_g=rowv(ln2_g), ln2_b=rowv(ln2_b),
        wr_hi=wr_hi, wr_lo=(w_r - wr_hi.astype(F32)).astype(BF16), br=rowv(b_r),
        alpha=(2 * depth) ** 0.25,
    )


def _dispatch_tables(idx, counts, t):
    rows = MOE_ROWS
    tiles, _, tm = idx.shape
    e = idx[:, 0:2, :].astype(jnp.int32)
    rank = idx[:, 4:6, :].astype(jnp.int32)
    cnt = counts[0, :N_EXPERTS].astype(jnp.int32)
    padded = (cnt + rows - 1) // rows * rows
    ends = jnp.cumsum(padded)
    starts = ends - padded
    ids = jnp.arange(N_EXPERTS, dtype=jnp.int32)[:, None, None, None]
    start_of = jnp.sum(jnp.where(e[None] == ids, starts[:, None, None, None], 0), axis=0)
    dst3 = (start_of + rank).reshape(tiles, 1, 2 * tm)
    total = 2 * t + N_EXPERTS * rows
    block_start = jnp.arange(total // rows, dtype=jnp.int32) * rows
    block_e = jnp.minimum(jnp.sum(ends[None, :] <= block_start[:, None], axis=1), N_EXPERTS - 1)
    nvalid = (ends[-1] // rows).reshape(1).astype(jnp.int32)
    last_start = jnp.maximum(ends - rows, 0).astype(jnp.int32)
    has_rows = (cnt > 0).astype(jnp.int32)
    return dst3, block_e.astype(jnp.int32), nvalid, last_start, has_rows, total


def _layer(x, mem, positions, depth, w_in, b_in, gm_ln_g, gm_ln_b, gm_w_s, gm_b_s, w_gm_out,
           mla_q_norm_g, mla_kv_norm_g, w_uq, w_uk, w_uv, w_mla_out, w_o, ln1_g, ln1_b,
           w_mq, w_mk, w_mv, w_mo, ln2_g, ln2_b,
           w_group_router, b_group_router, w_expert_router, b_expert_router,
           w_exp_gate, w_exp_up, w_exp_down, ln3_g, ln3_b):
    b, s, d = x.shape
    t = b * s
    mem_len = mem.shape[1]
    p = _prep_layer(w_in, b_in, gm_ln_g, gm_ln_b, gm_w_s, gm_b_s, w_gm_out, mla_q_norm_g, mla_kv_norm_g,
                    w_uq, w_uk, w_uv, w_mla_out, w_o, ln1_g, ln1_b, w_mq, w_mk, w_mv, w_mo, ln2_g, ln2_b,
                    w_group_router, b_group_router, w_expert_router, b_expert_router, depth)
    x2d = x.reshape(t, d)
    inv_freq = ROPE_THETA ** (-jnp.arange(ROPE_HALF, dtype=F32) / ROPE_HALF)
    cos, sin = _rope_tables(positions, inv_freq)
    gm, sg, q, k, v = _mixer_in(x2d, cos, sin, p)
    o = _mla_attn(q, k, v, b, s)
    mk, mv = _mem_kv(mem.reshape(b * mem_len, d), p["w_mk"], p["w_mv"], mem_len)
    x2, route, idx, counts = _post_mixer(x2d, o, gm, sg, mk, mv, p, s, mem_len)
    dst3, block_e, nvalid, last_start, has_rows, total = _dispatch_tables(idx, counts, t)
    xd = _moe_dispatch(x2, dst3, last_start, has_rows, nvalid, total)
    yd = _moe_experts(xd, block_e, nvalid, w_exp_gate, w_exp_up, w_exp_down)
    out = _moe_combine(x2, route, dst3, yd, ln3_g.reshape(1, d), ln3_b.reshape(1, d), p["alpha"])
    return out.reshape(b, s, d)


def kernel(x, mem, positions, w_in, b_in, gm_ln_g, gm_ln_b, gm_w_s, gm_b_s, w_gm_out, mla_q_norm_g, mla_kv_norm_g, w_uq, w_uk, w_uv, w_mla_out, w_o, ln1_g, ln1_b, w_mq, w_mk, w_mv, w_mo, ln2_g, ln2_b, w_group_router, b_group_router, w_expert_router, b_expert_router, w_exp_gate, w_exp_up, w_exp_down, ln3_g, ln3_b):
    depth = w_in.shape[0]
    per_layer = (w_in, b_in, gm_ln_g, gm_ln_b, gm_w_s, gm_b_s, w_gm_out, mla_q_norm_g, mla_kv_norm_g,
                 w_uq, w_uk, w_uv, w_mla_out, w_o, ln1_g, ln1_b, w_mq, w_mk, w_mv, w_mo, ln2_g, ln2_b,
                 w_group_router, b_group_router, w_expert_router, b_expert_router,
                 w_exp_gate, w_exp_up, w_exp_down, ln3_g, ln3_b)
    h = x
    for l in range(depth):
        h = _layer(h, mem, positions, depth, *[w[l] for w in per_layer])
    return h
```

```python
import functools
import math

import jax
import jax.numpy as jnp
from jax import lax
from jax.experimental import pallas as pl
from jax.experimental.pallas import tpu as pltpu

F32 = jnp.float32
BF16 = jnp.bfloat16

GM_WIDTH = 1024
GM_GROUPS = 8
GM_CHUNK = 128
MLA_HEADS = 16
MLA_NOPE = 64
MLA_ROPE = 32
MLA_V = 64
MLA_Q_LORA = 384
MLA_KV_LORA = 256
ROPE_THETA = 10000.0
MEM_HEADS = 4
N_GROUPS = 8
EXPERTS_PER_GROUP = 8
N_EXPERTS = N_GROUPS * EXPERTS_PER_GROUP
D_EXPERT = 256
LN_EPS = 1e-5
RMS_EPS = 1e-6

LANES = 128
HEAD_PAD = 128
V_ROWS = 80
ROPE_HALF = MLA_ROPE // 2

ROPE_BLOCK = 4096
TM_IN = 512
TM_SUB = 256
TQ = 1024
TK = 256
ATTN_HEADS = 2
SM_CHUNK = 16
TM_POST = 1024
MOE_ROWS = 512
TM_OUT = 256
VMEM_LIMIT = 56 * 1024 * 1024


def _const_spec(shape):
    nd = len(shape)
    return pl.BlockSpec(shape, lambda *_: (0,) * nd, pipeline_mode=pl.Buffered(1))


def _layer_norm(x, g, b):
    mu = jnp.mean(x, axis=-1, keepdims=True)
    xc = x - mu
    var = jnp.mean(xc * xc, axis=-1, keepdims=True)
    return xc * lax.rsqrt(var + LN_EPS) * g + b


def _rms_norm(x, g):
    return x * lax.rsqrt(jnp.mean(x * x, axis=-1, keepdims=True) + RMS_EPS) * g


def _gelu(x):
    return x * (lax.erf(x * (1.0 / math.sqrt(2.0))) + 1.0) * 0.5


def _dot(a, b):
    return jnp.dot(a, b, preferred_element_type=F32)


def _dot_nt(a, b):
    return lax.dot_general(a, b, (((1,), (1,)), ((), ())), preferred_element_type=F32)


def _rope_kernel(inv_ref, pos_ref, spread_ref, cos_ref, sin_ref):
    ang = inv_ref[...] * pos_ref[...].astype(F32)
    def to_rows(v):
        out = None
        for _ in range(3):
            piece = v.astype(BF16)
            v = v - piece.astype(F32)
            term = lax.dot_general(piece, spread_ref[...], (((0,), (0,)), ((), ())),
                                   preferred_element_type=F32)
            out = term if out is None else out + term
        return out

    cos_ref[...] = to_rows(jnp.cos(ang))
    sin_ref[...] = to_rows(jnp.sin(ang))


def _rope_tables(positions, inv_freq):
    t = positions.size
    blk = min(t, ROPE_BLOCK)
    spread = (jnp.arange(LANES)[None, :] % ROPE_HALF == jnp.arange(ROPE_HALF)[:, None]).astype(BF16)
    out = pl.BlockSpec((blk, LANES), lambda i: (i, 0))
    return pl.pallas_call(
        _rope_kernel,
        grid=(t // blk,),
        out_shape=(jax.ShapeDtypeStruct((t, LANES), F32),) * 2,
        in_specs=[pl.BlockSpec((ROPE_HALF, 1), lambda i: (0, 0)),
                  pl.BlockSpec((1, blk), lambda i: (0, i)),
                  pl.BlockSpec((ROPE_HALF, LANES), lambda i: (0, 0))],
        out_specs=(out, out),
        compiler_params=pltpu.CompilerParams(dimension_semantics=("arbitrary",)),
        name="rope_tables",
    )(inv_freq.reshape(ROPE_HALF, 1), positions.reshape(1, t), spread)


def _mixer_in_kernel(x_ref, cos_ref, sin_ref, w_uv_ref, b_uv_ref, w_gate_ref, b_gate_ref,
                     w_lat_ref, b_lat_ref, ln_g_ref, ln_b_ref, ws_ref, bs_ref, w_gm_out_ref,
                     qn_g_ref, kvn_g_ref, w_uq_ref, w_uk_ref, w_uvv_ref, b_v_ref,
                     gm_ref, sg_ref, q_ref, k_ref, v_ref, mixed_ref, *, tm, sub, scale):
    parts = range(tm // sub)
    rows = [slice(part * sub, (part + 1) * sub) for part in parts]
    xb = [x_ref[r, :].astype(BF16) for r in rows]

    uv = [_gelu(_dot(x, w_uv_ref[...]) + b_uv_ref[...]) for x in xb]
    vn = [_layer_norm(t[:, GM_WIDTH:], ln_g_ref[...], ln_b_ref[...]).astype(BF16) for t in uv]
    r_i = lax.broadcasted_iota(jnp.int32, (GM_CHUNK, GM_CHUNK), 0)
    c_i = lax.broadcasted_iota(jnp.int32, (GM_CHUNK, GM_CHUNK), 1)
    causal = c_i <= r_i
    for g in range(GM_GROUPS):
        w = jnp.where(causal, ws_ref[g], jnp.zeros_like(ws_ref[g]))
        cols = slice(g * GM_CHUNK, (g + 1) * GM_CHUNK)
        for part in parts:
            for ch in range(sub // GM_CHUNK):
                local = slice(ch * GM_CHUNK, (ch + 1) * GM_CHUNK)
                dest = slice(part * sub + ch * GM_CHUNK, part * sub + (ch + 1) * GM_CHUNK)
                mixed_ref[dest, cols] = _dot(w, vn[part][local, cols]) + bs_ref[:, cols]
    gated = [(t[:, :GM_WIDTH] * mixed_ref[r, :]).astype(BF16) for t, r in zip(uv, rows)]
    y_gm = [_dot(t, w_gm_out_ref[...]) for t in gated]
    gates = [_dot(x, w_gate_ref[...]) + b_gate_ref[...] for x in xb]
    for r, gt, y in zip(rows, gates, y_gm):
        gm_ref[r, :] = (jax.nn.sigmoid(gt[:, :GM_WIDTH]) * y).astype(BF16)
        sg_ref[r, :] = jax.nn.sigmoid(gt[:, GM_WIDTH:]).astype(BF16)

    lat = [_dot(x, w_lat_ref[...]) + b_lat_ref[...] for x in xb]
    cq = [_rms_norm(t[:, :MLA_Q_LORA], qn_g_ref[...]).astype(BF16) for t in lat]
    ckv = [_rms_norm(t[:, MLA_Q_LORA:MLA_Q_LORA + MLA_KV_LORA], kvn_g_ref[...]).astype(BF16) for t in lat]
    qa = [_dot(t, w_uq_ref[...]) for t in cq]
    kn = [_dot(t, w_uk_ref[...]) for t in ckv]

    lane = lax.broadcasted_iota(jnp.int32, (sub, HEAD_PAD), 1)
    in_rope = (lane >= MLA_NOPE) & (lane < MLA_NOPE + MLA_ROPE)
    for part in parts:
        r = rows[part]
        cosv = cos_ref[r, :]
        sinv = sin_ref[r, :]
        cfac = jnp.where(lane < MLA_NOPE, 1.0, jnp.where(in_rope, cosv, 0.0))
        sfac = jnp.where(in_rope, jnp.where(lane < MLA_NOPE + ROPE_HALF, -sinv, sinv), 0.0)

        def rot(t, cfac=cfac, sfac=sfac):
            return t * cfac + pltpu.roll(t, HEAD_PAD - ROPE_HALF, 1) * sfac

        kr_rot = rot(lat[part][:, MLA_Q_LORA + MLA_KV_LORA:])
        for h in range(MLA_HEADS):
            cols = slice(h * HEAD_PAD, (h + 1) * HEAD_PAD)
            q_ref[r, cols] = (rot(qa[part][:, cols]) * scale).astype(BF16)
            k_ref[r, cols] = (kn[part][:, cols] + kr_rot).astype(BF16)
        v_ref[:, r] = (_dot_nt(w_uvv_ref[...], ckv[part]) + b_v_ref[...]).astype(BF16)


def _mixer_in(x2d, cos, sin, p):
    t, d = x2d.shape
    tm = TM_IN
    scale = (MLA_NOPE + MLA_ROPE) ** -0.5 * math.log2(math.e)
    row = lambda n: pl.BlockSpec((tm, n), lambda i: (i, 0))
    weights = [p["w_uv"], p["b_uv"], p["w_gate"], p["b_gate"], p["w_lat"], p["b_lat"],
               p["gm_ln_g"], p["gm_ln_b"], p["ws"], p["bs_full"], p["w_gm_out"],
               p["qn_g"], p["kvn_g"], p["w_uq"], p["w_uk"], p["w_uvv"], p["b_v"]]
    qk_w = MLA_HEADS * HEAD_PAD
    v_w = MLA_HEADS * V_ROWS
    return pl.pallas_call(
        functools.partial(_mixer_in_kernel, tm=tm, sub=TM_SUB, scale=scale),
        grid=(t // tm,),
        out_shape=(jax.ShapeDtypeStruct((t, d), BF16), jax.ShapeDtypeStruct((t, d), BF16),
                   jax.ShapeDtypeStruct((t, qk_w), BF16), jax.ShapeDtypeStruct((t, qk_w), BF16),
                   jax.ShapeDtypeStruct((v_w, t), BF16)),
        in_specs=[row(d), row(LANES), row(LANES)] + [_const_spec(w.shape) for w in weights],
        out_specs=(row(d), row(d), row(qk_w), row(qk_w), pl.BlockSpec((v_w, tm), lambda i: (0, i))),
        scratch_shapes=[pltpu.VMEM((tm, GM_WIDTH), F32)],
        compiler_params=pltpu.CompilerParams(dimension_semantics=("arbitrary",),
                                             vmem_limit_bytes=VMEM_LIMIT),
        name="mixer_in",
    )(x2d, cos, sin, *weights)


def _attn_kernel(q_ref, k_ref, vt_ref, o_ref, acc_ref, m_ref, r0_ref, r1_ref, s0_ref, s1_ref, p0_ref, p1_ref,
                 *, tq, tk):
    i = pl.program_id(2)
    acc_ref[...] = jnp.zeros_like(acc_ref)
    m_ref[...] = jnp.full(m_ref.shape, -jnp.inf, F32)

    every = slice(0, tq)

    def scores(j, s_ref, qs=every):
        kb = k_ref[pl.ds(pl.multiple_of(j * tk, tk), tk), :]
        for a in range(ATTN_HEADS):
            cols = slice(a * HEAD_PAD, (a + 1) * HEAD_PAD)
            s_ref[a, :, qs] = _dot_nt(kb[:, cols], q_ref[qs, cols])

    def softmax(s_ref, p_ref, r_ref, qs=every, diagonal=False):
        nch = tk // SM_CHUNK
        width = qs.stop - qs.start
        if diagonal:
            q_loc = lax.broadcasted_iota(jnp.int32, (SM_CHUNK, width), 1)
            k_loc = lax.broadcasted_iota(jnp.int32, (SM_CHUNK, width), 0)

        def chunk(a, c):
            s = s_ref[a, c * SM_CHUNK:(c + 1) * SM_CHUNK, qs]
            if diagonal:
                s = jnp.where(k_loc + c * SM_CHUNK <= q_loc, s, -jnp.inf)
            return s

        for a in range(ATTN_HEADS):
            m = m_ref[a:a + 1, qs]
            cm = chunk(a, 0)
            for c in range(1, nch):
                cm = jnp.maximum(cm, chunk(a, c))
            m_new = jnp.maximum(m, jnp.max(cm, axis=0, keepdims=True))
            r_ref[a:a + 1, qs] = jnp.exp2(m - m_new)
            m_ref[a:a + 1, qs] = m_new
            mb = jnp.broadcast_to(m_new, (SM_CHUNK, width))
            for c in range(nch):
                p_ref[a, c * SM_CHUNK:(c + 1) * SM_CHUNK, qs] = jnp.exp2(chunk(a, c) - mb).astype(BF16)

    def accumulate(j, p_ref, r_ref, qs=every):
        vb = vt_ref[:, pl.ds(pl.multiple_of(j * tk, tk), tk)]
        for a in range(ATTN_HEADS):
            rows = slice(a * V_ROWS, (a + 1) * V_ROWS)
            alpha = r_ref[a:a + 1, qs]
            acc_ref[rows, qs] = acc_ref[rows, qs] * alpha + _dot(vb[rows, :], p_ref[a, :, qs])

    def pair(base):
        scores(base + 1, s1_ref)
        softmax(s0_ref, p0_ref, r0_ref)
        accumulate(base, p0_ref, r0_ref)
        scores(base + 2, s0_ref)
        softmax(s1_ref, p1_ref, r1_ref)
        accumulate(base + 1, p1_ref, r1_ref)

    def main(t, carry):
        pair(4 * t)
        pair(4 * t + 2)
        return carry

    nsub = tq // tk
    first = nsub * i
    scores(0, s0_ref)
    lax.fori_loop(0, first // 4, main, 0)
    if nsub % 4:
        @pl.when(first % 4 == 2)
        def _():
            pair(first - 2)

    bufs = ((s0_ref, p0_ref, r0_ref), (s1_ref, p1_ref, r1_ref))
    for j in range(nsub):
        s_ref, p_ref, r_ref = bufs[j % 2]
        if j + 1 < nsub:
            scores(first + j + 1, bufs[(j + 1) % 2][0], slice((j + 1) * tk, tq))
        softmax(s_ref, p_ref, r_ref, slice(j * tk, (j + 1) * tk), diagonal=True)
        if j + 1 < nsub:
            softmax(s_ref, p_ref, r_ref, slice((j + 1) * tk, tq))
        accumulate(first + j, p_ref, r_ref, slice(j * tk, tq))
    out = [acc_ref[a * V_ROWS:a * V_ROWS + MLA_V, :] / acc_ref[a * V_ROWS + MLA_V:a * V_ROWS + MLA_V + 1, :]
           for a in range(ATTN_HEADS)]
    o_ref[...] = jnp.concatenate(out, axis=0).T.astype(BF16)


def _mla_attn(q, k, vt, batch, seq):
    assert TQ % (2 * TK) == 0, "visible key blocks are walked in pairs"
    nq = seq // TQ
    nh = ATTN_HEADS
    return pl.pallas_call(
        functools.partial(_attn_kernel, tq=TQ, tk=TK),
        grid=(batch, MLA_HEADS // nh, nq),
        out_shape=jax.ShapeDtypeStruct((batch * seq, MLA_HEADS * MLA_V), BF16),
        in_specs=[pl.BlockSpec((TQ, nh * HEAD_PAD), lambda b, h, i: (b * nq + i, h)),
                  pl.BlockSpec((seq, nh * HEAD_PAD), lambda b, h, i: (b, h)),
                  pl.BlockSpec((nh * V_ROWS, seq), lambda b, h, i: (h, b))],
        out_specs=pl.BlockSpec((TQ, nh * MLA_V), lambda b, h, i: (b * nq + i, h)),
        scratch_shapes=[pltpu.VMEM((nh * V_ROWS, TQ), F32)] + [pltpu.VMEM((8, TQ), F32)] * 3 + [
                        pltpu.VMEM((nh, TK, TQ), F32), pltpu.VMEM((nh, TK, TQ), F32),
                        pltpu.VMEM((nh, TK, TQ), BF16), pltpu.VMEM((nh, TK, TQ), BF16)],
        compiler_params=pltpu.CompilerParams(
            dimension_semantics=("arbitrary", "arbitrary", "arbitrary"), vmem_limit_bytes=VMEM_LIMIT),
        name="mla_attn",
    )(q, k, vt)


def _mem_kv_kernel(mem_ref, wk_ref, wv_ref, k_ref, v_ref):
    mb = mem_ref[...].astype(BF16)
    k_ref[...] = _dot(mb, wk_ref[...]).astype(BF16)
    v_ref[...] = _dot(mb, wv_ref[...]).astype(BF16)


def _mem_kv(mem2d, w_mk, w_mv, mem_len):
    rows, d = mem2d.shape
    blk = pl.BlockSpec((mem_len, d), lambda i: (i, 0))
    return pl.pallas_call(
        _mem_kv_kernel,
        grid=(rows // mem_len,),
        out_shape=(jax.ShapeDtypeStruct((rows, d), BF16),) * 2,
        in_specs=[blk, _const_spec(w_mk.shape), _const_spec(w_mv.shape)],
        out_specs=(blk, blk),
        compiler_params=pltpu.CompilerParams(dimension_semantics=("arbitrary",)),
        name="mem_kv",
    )(mem2d, w_mk, w_mv)


def _post_mixer_kernel(x_ref, o_ref, gm_ref, sg_ref, mk_ref, mv_ref,
                       w_mla_out_ref, w_o_ref, ln1_g_ref, ln1_b_ref,
                       w_mq_ref, w_mo_ref, ln2_g_ref, ln2_b_ref,
                       wr_hi_ref, wr_lo_ref, br_ref,
                       x2_ref, route_ref, idx_ref, counts_ref, run_ref, *, tm, sub, alpha, mem_scale):
    i = pl.program_id(0)

    @pl.when(i == 0)
    def _():
        run_ref[...] = jnp.zeros_like(run_ref)

    parts = range(tm // sub)
    rows = [slice(part * sub, (part + 1) * sub) for part in parts]
    y_mla = [_dot(o_ref[r, :], w_mla_out_ref[...]) for r in rows]
    merged = [(gm_ref[r, :].astype(F32) + sg_ref[r, :].astype(F32) * y).astype(BF16) for r, y in zip(rows, y_mla)]
    mixed = [_dot(m, w_o_ref[...]) for m in merged]
    x1 = [_layer_norm(alpha * x_ref[r, :] + h, ln1_g_ref[...], ln1_b_ref[...]) for r, h in zip(rows, mixed)]

    qm = [(_dot(x.astype(BF16), w_mq_ref[...]) * mem_scale).astype(BF16) for x in x1]
    hd = mk_ref.shape[1] // MEM_HEADS
    heads = [[] for _ in parts]
    for h in range(MEM_HEADS):
        cols = slice(h * hd, (h + 1) * hd)
        s = [_dot_nt(q[:, cols], mk_ref[:, cols]) for q in qm]
        pr = [jnp.exp(v - jnp.max(v, axis=1, keepdims=True)) for v in s]
        for part in parts:
            heads[part].append(_dot(pr[part].astype(BF16), mv_ref[:, cols])
                               / jnp.sum(pr[part], axis=1, keepdims=True))
    om = [jnp.concatenate(hs, axis=1).astype(BF16) for hs in heads]
    mem_out = [_dot(v, w_mo_ref[...]) for v in om]
    x2 = [_layer_norm(alpha * x + h, ln2_g_ref[...], ln2_b_ref[...]) for x, h in zip(x1, mem_out)]
    for r, v in zip(rows, x2):
        x2_ref[r, :] = v

    x_hi = [v.astype(BF16) for v in x2]
    x_lo = [(v - hi.astype(F32)).astype(BF16) for v, hi in zip(x2, x_hi)]
    logit = [_dot(hi, wr_hi_ref[...]) + _dot(lo, wr_hi_ref[...]) + _dot(hi, wr_lo_ref[...]) + br_ref[...]
             for hi, lo in zip(x_hi, x_lo)]
    lane = lax.broadcasted_iota(jnp.int32, (sub, LANES), 1).astype(F32)
    big = jnp.float32(1e9)
    r_i = lax.broadcasted_iota(jnp.int32, (sub, sub), 0)
    c_i = lax.broadcasted_iota(jnp.int32, (sub, sub), 1)
    tri = jnp.where(c_i < r_i, 1.0, 0.0).astype(BF16)

    def first_argmax(vals, vmax):
        return jnp.min(jnp.where(vals == vmax, lane, big), axis=1, keepdims=True)

    for part in parts:
        logits = logit[part]
        g_mask = (lane >= N_EXPERTS) & (lane < N_EXPERTS + N_GROUPS)
        lg = jnp.where(g_mask, logits, -jnp.inf)
        g_max = jnp.max(lg, axis=1, keepdims=True)
        g_sel = first_argmax(lg, g_max) - N_EXPERTS
        g_w = 1.0 / jnp.sum(jnp.where(g_mask, jnp.exp(logits - g_max), 0.0), axis=1, keepdims=True)
        in_group = jnp.floor(lane * (1.0 / EXPERTS_PER_GROUP)) == g_sel
        le = jnp.where(in_group, logits, -jnp.inf)
        v1 = jnp.max(le, axis=1, keepdims=True)
        e1 = first_argmax(le, v1)
        le2 = jnp.where(lane == e1, -jnp.inf, le)
        v2 = jnp.max(le2, axis=1, keepdims=True)
        e2 = first_argmax(le2, v2)
        t2 = jnp.exp(v2 - v1)
        w1 = (1.0 / (1.0 + t2)) * g_w
        w2 = (t2 / (1.0 + t2)) * g_w

        hit1 = lane == e1
        hit2 = lane == e2
        onehot = jnp.where(hit1 | hit2, 1.0, 0.0)
        before = _dot(tri, onehot.astype(BF16)) + run_ref[...]
        r1 = jnp.sum(jnp.where(hit1, before, 0.0), axis=1, keepdims=True)
        r2 = jnp.sum(jnp.where(hit2, before, 0.0), axis=1, keepdims=True)
        run_ref[...] = run_ref[...] + jnp.sum(onehot, axis=0, keepdims=True)

        packed = jnp.zeros((sub, LANES), F32)
        for pos, val in enumerate((e1, e2, w1, w2, r1, r2)):
            packed = jnp.where(lane == pos, val, packed)
        route_ref[rows[part], :] = packed
        idx_ref[part] = packed.T[:8, :]
    counts_ref[...] = jnp.broadcast_to(run_ref[...], counts_ref.shape)


def _post_mixer(x2d, o, gm, sg, mk, mv, p, seq, mem_len):
    t, d = x2d.shape
    tm, sub = TM_POST, TM_OUT
    assert seq % tm == 0, "a post_mixer row tile must not straddle two sequences (memory blocks are per sequence)"
    per_batch = seq // tm
    row = lambda n: pl.BlockSpec((tm, n), lambda i: (i, 0))
    memblk = pl.BlockSpec((mem_len, d), lambda i: (i // per_batch, 0))
    weights = [p["w_mla_out"], p["w_o"], p["ln1_g"], p["ln1_b"], p["w_mq"], p["w_mo"],
               p["ln2_g"], p["ln2_b"], p["wr_hi"], p["wr_lo"], p["br"]]
    return pl.pallas_call(
        functools.partial(_post_mixer_kernel, tm=tm, sub=sub, alpha=p["alpha"],
                          mem_scale=(d // MEM_HEADS) ** -0.5),
        grid=(t // tm,),
        out_shape=(jax.ShapeDtypeStruct((t, d), F32), jax.ShapeDtypeStruct((t, LANES), F32),
                   jax.ShapeDtypeStruct((t // sub, 8, sub), F32), jax.ShapeDtypeStruct((8, LANES), F32)),
        in_specs=[row(d), row(d), row(d), row(d), memblk, memblk]
                 + [_const_spec(w.shape) for w in weights],
        out_specs=(row(d), row(LANES), pl.BlockSpec((tm // sub, 8, sub), lambda i: (i, 0, 0)),
                   pl.BlockSpec((8, LANES), lambda i: (0, 0))),
        scratch_shapes=[pltpu.VMEM((1, LANES), F32)],
        compiler_params=pltpu.CompilerParams(dimension_semantics=("arbitrary",),
                                             vmem_limit_bytes=VMEM_LIMIT),
        name="post_mixer",
    )(x2d, o, gm, sg, mk, mv, *weights)


def _dispatch_kernel(last_ref, has_ref, nvalid_ref, dst_ref, x_hbm, xd_hbm, zbuf, xbuf, zsem, lsems, ssems,
                     *, tm, nsteps, rows, nblocks):
    i = pl.program_id(0)

    def load(step, s):
        return pltpu.make_async_copy(x_hbm.at[pl.ds(pl.multiple_of(step * tm, tm), tm)], xbuf.at[s], lsems.at[s])

    def drain(s):
        pltpu.make_async_copy(xd_hbm.at[pl.ds(0, 2 * tm)], xd_hbm.at[pl.ds(0, 2 * tm)], ssems.at[s]).wait()

    @pl.when(i == 0)
    def _():
        load(0, 0).start()
        zbuf[...] = jnp.zeros_like(zbuf)

        def zero_copy(start):
            return pltpu.make_async_copy(zbuf, xd_hbm.at[pl.ds(pl.multiple_of(start, rows), rows)], zsem)

        for e in range(N_EXPERTS):
            @pl.when(has_ref[e] > 0)
            def _():
                zero_copy(last_ref[e]).start()

        def start_tail(blk, carry):
            zero_copy(blk * rows).start()
            return carry

        def wait_tail(blk, carry):
            zero_copy(blk * rows).wait()
            return carry

        lax.fori_loop(nvalid_ref[0], nblocks, start_tail, 0)
        for e in range(N_EXPERTS):
            @pl.when(has_ref[e] > 0)
            def _():
                zero_copy(last_ref[e]).wait()

        lax.fori_loop(nvalid_ref[0], nblocks, wait_tail, 0)

    for s in range(3):
        @pl.when(i % 3 == s)
        def _():
            nxt = (s + 1) % 3
            load(i, s).wait()

            @pl.when(i >= 2)
            def _():
                drain(nxt)

            @pl.when(i + 1 < nsteps)
            def _():
                load(i + 1, nxt).start()

            for k in range(2):
                for r in range(tm):
                    pltpu.make_async_copy(xbuf.at[s, pl.ds(r, 1)],
                                          xd_hbm.at[pl.ds(dst_ref[0, 0, k * tm + r], 1)],
                                          ssems.at[s]).start(priority=r % 2)

            @pl.when(i == nsteps - 1)
            def _():
                if nsteps >= 2:
                    drain((s + 2) % 3)
                drain(s)


def _moe_dispatch(x2, dst3, last_start, has_rows, nvalid, total_rows):
    t, d = x2.shape
    tm = TM_OUT
    nsteps = t // tm
    assert dst3.shape == (nsteps, 1, 2 * tm), "post_mixer and dispatch/combine tiles must coincide"
    grid_spec = pltpu.PrefetchScalarGridSpec(
        num_scalar_prefetch=3,
        grid=(nsteps,),
        in_specs=[pl.BlockSpec((1, 1, 2 * tm), lambda i, la, ha, nv: (i, 0, 0), memory_space=pltpu.SMEM),
                  pl.BlockSpec(memory_space=pl.ANY)],
        out_specs=pl.BlockSpec(memory_space=pl.ANY),
        scratch_shapes=[pltpu.VMEM((MOE_ROWS, d), F32), pltpu.VMEM((3, tm, d), F32),
                        pltpu.SemaphoreType.DMA(()), pltpu.SemaphoreType.DMA((3,)),
                        pltpu.SemaphoreType.DMA((3,))],
    )
    return pl.pallas_call(
        functools.partial(_dispatch_kernel, tm=tm, nsteps=nsteps, rows=MOE_ROWS,
                          nblocks=total_rows // MOE_ROWS),
        grid_spec=grid_spec,
        out_shape=jax.ShapeDtypeStruct((total_rows, d), F32),
        compiler_params=pltpu.CompilerParams(dimension_semantics=("arbitrary",), has_side_effects=True),
        name="moe_dispatch",
    )(last_start, has_rows, nvalid, dst3, x2)


def _experts_kernel(be_ref, nvalid_ref, xd_ref, wg_ref, wu_ref, wd_ref, yd_ref, wgu_bf, wd_bf):
    i = pl.program_id(0)
    changed = jnp.logical_or(i == 0, be_ref[i] != be_ref[jnp.maximum(i - 1, 0)])

    @pl.when(changed)
    def _():
        wgu_bf[:, :D_EXPERT] = wg_ref[...].astype(BF16)
        wgu_bf[:, D_EXPERT:] = wu_ref[...].astype(BF16)
        wd_bf[...] = wd_ref[...].astype(BF16)

    @pl.when(i < nvalid_ref[0])
    def _():
        gu = _dot(xd_ref[...].astype(BF16), wgu_bf[...])
        gate = gu[:, :D_EXPERT]
        hidden = (gate * jax.nn.sigmoid(gate) * gu[:, D_EXPERT:]).astype(BF16)
        yd_ref[...] = _dot(hidden, wd_bf[...])

    @pl.when(i >= nvalid_ref[0])
    def _():
        yd_ref[...] = jnp.zeros_like(yd_ref)


def _moe_experts(xd, block_e, nvalid, w_gate, w_up, w_down):
    total, d = xd.shape
    rows = MOE_ROWS
    nblocks = total // rows
    grid_spec = pltpu.PrefetchScalarGridSpec(
        num_scalar_prefetch=2,
        grid=(nblocks,),
        in_specs=[pl.BlockSpec((rows, d), lambda i, be, nv: (jnp.minimum(i, nv[0] - 1), 0)),
                  pl.BlockSpec((None, d, D_EXPERT), lambda i, be, nv: (be[i], 0, 0)),
                  pl.BlockSpec((None, d, D_EXPERT), lambda i, be, nv: (be[i], 0, 0)),
                  pl.BlockSpec((None, D_EXPERT, d), lambda i, be, nv: (be[i], 0, 0))],
        out_specs=pl.BlockSpec((rows, d), lambda i, be, nv: (i, 0)),
        scratch_shapes=[pltpu.VMEM((d, 2 * D_EXPERT), BF16), pltpu.VMEM((D_EXPERT, d), BF16)],
    )
    return pl.pallas_call(
        _experts_kernel,
        grid_spec=grid_spec,
        out_shape=jax.ShapeDtypeStruct((total, d), F32),
        compiler_params=pltpu.CompilerParams(dimension_semantics=("arbitrary",),
                                             vmem_limit_bytes=VMEM_LIMIT),
        name="moe_experts",
    )(block_e, nvalid, xd, w_gate, w_up, w_down)


def _combine_kernel(dst_cur_ref, dst_nxt_ref, dst_nx2_ref, x2_ref, route_ref, g_ref, b_ref, yd_hbm,
                    out_ref, ybuf, sems, *, tm, nsteps, alpha):
    i = pl.program_id(0)

    def issue(idx_ref, s):
        for r in range(2 * tm):
            pltpu.make_async_copy(yd_hbm.at[pl.ds(idx_ref[0, 0, r], 1)], ybuf.at[s, pl.ds(r, 1)],
                                  sems.at[s]).start(priority=r % 2)

    def wait(s):
        pltpu.make_async_copy(yd_hbm.at[pl.ds(0, 2 * tm)], ybuf.at[s], sems.at[s]).wait()

    @pl.when(i == 0)
    def _():
        issue(dst_cur_ref, 0)
        issue(dst_nxt_ref, 1)

    for s in range(3):
        @pl.when(i % 3 == s)
        def _():
            wait(s)
            issue(dst_nx2_ref, (s + 2) % 3)
            route = route_ref[...]
            y = ybuf[s, :tm, :] * route[:, 2:3] + ybuf[s, tm:, :] * route[:, 3:4]
            out_ref[...] = _layer_norm(alpha * x2_ref[...] + y, g_ref[...], b_ref[...])

            @pl.when(i == nsteps - 1)
            def _():
                wait((s + 1) % 3)
                wait((s + 2) % 3)


def _moe_combine(x2, route, dst3, yd, ln_g, ln_b, alpha):
    t, d = x2.shape
    tm = TM_OUT
    nsteps = t // tm
    smem_blk = lambda f: pl.BlockSpec((1, 1, 2 * tm), f, memory_space=pltpu.SMEM)
    row = lambda n: pl.BlockSpec((tm, n), lambda i: (i, 0))
    return pl.pallas_call(
        functools.partial(_combine_kernel, tm=tm, nsteps=nsteps, alpha=alpha),
        grid=(nsteps,),
        out_shape=jax.ShapeDtypeStruct((t, d), F32),
        in_specs=[smem_blk(lambda i: (i, 0, 0)),
                  smem_blk(lambda i: (jnp.minimum(i + 1, nsteps - 1), 0, 0)),
                  smem_blk(lambda i: (jnp.minimum(i + 2, nsteps - 1), 0, 0)),
                  row(d), row(LANES), _const_spec(ln_g.shape), _const_spec(ln_b.shape),
                  pl.BlockSpec(memory_space=pl.ANY)],
        out_specs=row(d),
        scratch_shapes=[pltpu.VMEM((3, 2 * tm, d), F32), pltpu.SemaphoreType.DMA((3,))],
        compiler_params=pltpu.CompilerParams(dimension_semantics=("arbitrary",),
                                             vmem_limit_bytes=VMEM_LIMIT),
        name="moe_combine",
    )(dst3, dst3, dst3, x2, route, ln_g, ln_b, yd)


def _prep_layer(w_in, b_in, gm_ln_g, gm_ln_b, gm_w_s, gm_b_s, w_gm_out, mla_q_norm_g, mla_kv_norm_g,
                w_uq, w_uk, w_uv, w_mla_out, w_o, ln1_g, ln1_b, w_mq, w_mk, w_mv, w_mo, ln2_g, ln2_b,
                w_group_router, b_group_router, w_expert_router, b_expert_router, depth):
    d = w_in.shape[0]
    s_v = 2 * GM_WIDTH
    s_q = s_v + MLA_Q_LORA
    s_kv = s_q + MLA_KV_LORA
    s_r = s_kv + MLA_ROPE
    rowv = lambda a: a.reshape(1, -1).astype(F32)

    def lat_cols(a):
        z = jnp.zeros(a.shape[:-1] + (MLA_NOPE,), a.dtype)
        kr = a[..., s_kv:s_r]
        return jnp.concatenate([a[..., s_v:s_kv], z, kr, kr], axis=-1)

    wq3 = w_uq.reshape(MLA_Q_LORA, MLA_HEADS, MLA_NOPE + MLA_ROPE)
    wq_pad = jnp.concatenate([wq3, wq3[..., MLA_NOPE:]], axis=-1).reshape(MLA_Q_LORA, MLA_HEADS * HEAD_PAD)
    wk3 = w_uk.reshape(MLA_KV_LORA, MLA_HEADS, MLA_NOPE)
    wk_pad = jnp.pad(wk3, ((0, 0), (0, 0), (0, HEAD_PAD - MLA_NOPE))).reshape(MLA_KV_LORA, MLA_HEADS * HEAD_PAD)
    wv3 = w_uv.T.reshape(MLA_HEADS, MLA_V, MLA_KV_LORA)
    wv_pad = jnp.pad(wv3, ((0, 0), (0, V_ROWS - MLA_V), (0, 0))).reshape(MLA_HEADS * V_ROWS, MLA_KV_LORA)
    b_v = jnp.zeros((MLA_HEADS, V_ROWS, 1), F32).at[:, MLA_V].set(1.0).reshape(MLA_HEADS * V_ROWS, 1)
    w_r = jnp.zeros((d, LANES), F32)
    w_r = w_r.at[:, :N_EXPERTS].set(w_expert_router).at[:, N_EXPERTS:N_EXPERTS + N_GROUPS].set(w_group_router)
    b_r = jnp.zeros((LANES,), F32)
    b_r = b_r.at[:N_EXPERTS].set(b_expert_router).at[N_EXPERTS:N_EXPERTS + N_GROUPS].set(b_group_router)
    wr_hi = w_r.astype(BF16)
    return dict(
        w_uv=w_in[:, :s_v].astype(BF16), b_uv=rowv(b_in[:s_v]),
        w_gate=w_in[:, s_r:].astype(BF16), b_gate=rowv(b_in[s_r:]),
        w_lat=lat_cols(w_in).astype(BF16), b_lat=rowv(lat_cols(b_in)),
        gm_ln_g=rowv(gm_ln_g), gm_ln_b=rowv(gm_ln_b),
        ws=gm_w_s.astype(BF16), bs_full=jnp.repeat(gm_b_s.T, GM_CHUNK, axis=1).astype(F32),
        w_gm_out=w_gm_out.astype(BF16), qn_g=rowv(mla_q_norm_g), kvn_g=rowv(mla_kv_norm_g),
        w_uq=wq_pad.astype(BF16), w_uk=wk_pad.astype(BF16), w_uvv=wv_pad.astype(BF16), b_v=b_v,
        w_mla_out=w_mla_out.astype(BF16), w_o=w_o.astype(BF16), ln1_g=rowv(ln1_g), ln1_b=rowv(ln1_b),
        w_mq=w_mq.astype(BF16), w_mk=w_mk.astype(BF16), w_mv=w_mv.astype(BF16), w_mo=w_mo.astype(BF16),
        ln2_g=rowv(ln2_g), ln2_b=rowv(ln2_b),
        wr_hi=wr_hi, wr_lo=(w_r - wr_hi.astype(F32)).astype(BF16), br=rowv(b_r),
        alpha=(2 * depth) ** 0.25,
    )


def _dispatch_tables(idx, counts, t):
    rows = MOE_ROWS
    tiles, _, tm = idx.shape
    e = idx[:, 0:2, :].astype(jnp.int32)
    rank = idx[:, 4:6, :].astype(jnp.int32)
    cnt = counts[0, :N_EXPERTS].astype(jnp.int32)
    padded = (cnt + rows - 1) // rows * rows
    ends = jnp.cumsum(padded)
    starts = ends - padded
    ids = jnp.arange(N_EXPERTS, dtype=jnp.int32)[:, None, None, None]
    start_of = jnp.sum(jnp.where(e[None] == ids, starts[:, None, None, None], 0), axis=0)
    dst3 = (start_of + rank).reshape(tiles, 1, 2 * tm)
    total = 2 * t + N_EXPERTS * rows
    block_start = jnp.arange(total // rows, dtype=jnp.int32) * rows
    block_e = jnp.minimum(jnp.sum(ends[None, :] <= block_start[:, None], axis=1), N_EXPERTS - 1)
    nvalid = (ends[-1] // rows).reshape(1).astype(jnp.int32)
    last_start = jnp.maximum(ends - rows, 0).astype(jnp.int32)
    has_rows = (cnt > 0).astype(jnp.int32)
    return dst3, block_e.astype(jnp.int32), nvalid, last_start, has_rows, total


def _layer(x, mem, positions, depth, w_in, b_in, gm_ln_g, gm_ln_b, gm_w_s, gm_b_s, w_gm_out,
           mla_q_norm_g, mla_kv_norm_g, w_uq, w_uk, w_uv, w_mla_out, w_o, ln1_g, ln1_b,
           w_mq, w_mk, w_mv, w_mo, ln2_g, ln2_b,
           w_group_router, b_group_router, w_expert_router, b_expert_router,
           w_exp_gate, w_exp_up, w_exp_down, ln3_g, ln3_b):
    b, s, d = x.shape
    t = b * s
    mem_len = mem.shape[1]
    p = _prep_layer(w_in, b_in, gm_ln_g, gm_ln_b, gm_w_s, gm_b_s, w_gm_out, mla_q_norm_g, mla_kv_norm_g,
                    w_uq, w_uk, w_uv, w_mla_out, w_o, ln1_g, ln1_b, w_mq, w_mk, w_mv, w_mo, ln2_g, ln2_b,
                    w_group_router, b_group_router, w_expert_router, b_expert_router, depth)
    x2d = x.reshape(t, d)
    inv_freq = ROPE_THETA ** (-jnp.arange(ROPE_HALF, dtype=F32) / ROPE_HALF)
    cos, sin = _rope_tables(positions, inv_freq)
    gm, sg, q, k, v = _mixer_in(x2d, cos, sin, p)
    o = _mla_attn(q, k, v, b, s)
    mk, mv = _mem_kv(mem.reshape(b * mem_len, d), p["w_mk"], p["w_mv"], mem_len)
    x2, route, idx, counts = _post_mixer(x2d, o, gm, sg, mk, mv, p, s, mem_len)
    dst3, block_e, nvalid, last_start, has_rows, total = _dispatch_tables(idx, counts, t)
    xd = _moe_dispatch(x2, dst3, last_start, has_rows, nvalid, total)
    yd = _moe_experts(xd, block_e, nvalid, w_exp_gate, w_exp_up, w_exp_down)
    out = _moe_combine(x2, route, dst3, yd, ln3_g.reshape(1, d), ln3_b.reshape(1, d), p["alpha"])
    return out.reshape(b, s, d)


def kernel(x, mem, positions, w_in, b_in, gm_ln_g, gm_ln_b, gm_w_s, gm_b_s, w_gm_out, mla_q_norm_g, mla_kv_norm_g, w_uq, w_uk, w_uv, w_mla_out, w_o, ln1_g, ln1_b, w_mq, w_mk, w_mv, w_mo, ln2_g, ln2_b, w_group_router, b_group_router, w_expert_router, b_expert_router, w_exp_gate, w_exp_up, w_exp_down, ln3_g, ln3_b):
    depth = w_in.shape[0]
    per_layer = (w_in, b_in, gm_ln_g, gm_ln_b, gm_w_s, gm_b_s, w_gm_out, mla_q_norm_g, mla_kv_norm_g,
                 w_uq, w_uk, w_uv, w_mla_out, w_o, ln1_g, ln1_b, w_mq, w_mk, w_mv, w_mo, ln2_g, ln2_b,
                 w_group_router, b_group_router, w_expert_router, b_expert_router,
                 w_exp_gate, w_exp_up, w_exp_down, ln3_g, ln3_b)
    h = x
    for l in range(depth):
        h = _layer(h, mem, positions, depth, *[w[l] for w in per_layer])
    return h
```

```python
import functools
import math

import jax
import jax.numpy as jnp
from jax import lax
from jax.experimental import pallas as pl
from jax.experimental.pallas import tpu as pltpu

F32 = jnp.float32
BF16 = jnp.bfloat16

GM_WIDTH = 1024
GM_GROUPS = 8
GM_CHUNK = 128
MLA_HEADS = 16
MLA_NOPE = 64
MLA_ROPE = 32
MLA_V = 64
MLA_Q_LORA = 384
MLA_KV_LORA = 256
ROPE_THETA = 10000.0
MEM_HEADS = 4
N_GROUPS = 8
EXPERTS_PER_GROUP = 8
N_EXPERTS = N_GROUPS * EXPERTS_PER_GROUP
D_EXPERT = 256
LN_EPS = 1e-5
RMS_EPS = 1e-6

LANES = 128
HEAD_PAD = 128
V_ROWS = 80
ROPE_HALF = MLA_ROPE // 2

ROPE_BLOCK = 4096
TM_IN = 512
TM_SUB = 256
TQ = 1024
TK = 256
ATTN_HEADS = 2
SM_CHUNK = 16
TM_POST = 1024
MOE_ROWS = 512
TM_OUT = 256
VMEM_LIMIT = 56 * 1024 * 1024


def _const_spec(shape):
    nd = len(shape)
    return pl.BlockSpec(shape, lambda *_: (0,) * nd, pipeline_mode=pl.Buffered(1))


def _layer_norm(x, g, b):
    mu = jnp.mean(x, axis=-1, keepdims=True)
    xc = x - mu
    var = jnp.mean(xc * xc, axis=-1, keepdims=True)
    return xc * lax.rsqrt(var + LN_EPS) * g + b


def _rms_norm(x, g):
    return x * lax.rsqrt(jnp.mean(x * x, axis=-1, keepdims=True) + RMS_EPS) * g


def _gelu(x):
    return x * (lax.erf(x * (1.0 / math.sqrt(2.0))) + 1.0) * 0.5


def _dot(a, b):
    return jnp.dot(a, b, preferred_element_type=F32)


def _dot_nt(a, b):
    return lax.dot_general(a, b, (((1,), (1,)), ((), ())), preferred_element_type=F32)


def _rope_kernel(inv_ref, pos_ref, spread_ref, cos_ref, sin_ref):
    ang = inv_ref[...] * pos_ref[...].astype(F32)
    def to_rows(v):
        out = None
        for _ in range(3):
            piece = v.astype(BF16)
            v = v - piece.astype(F32)
            term = lax.dot_general(piece, spread_ref[...], (((0,), (0,)), ((), ())),
                                   preferred_element_type=F32)
            out = term if out is None else out + term
        return out

    cos_ref[...] = to_rows(jnp.cos(ang))
    sin_ref[...] = to_rows(jnp.sin(ang))


def _rope_tables(positions, inv_freq):
    t = positions.size
    blk = min(t, ROPE_BLOCK)
    spread = (jnp.arange(LANES)[None, :] % ROPE_HALF == jnp.arange(ROPE_HALF)[:, None]).astype(BF16)
    out = pl.BlockSpec((blk, LANES), lambda i: (i, 0))
    return pl.pallas_call(
        _rope_kernel,
        grid=(t // blk,),
        out_shape=(jax.ShapeDtypeStruct((t, LANES), F32),) * 2,
        in_specs=[pl.BlockSpec((ROPE_HALF, 1), lambda i: (0, 0)),
                  pl.BlockSpec((1, blk), lambda i: (0, i)),
                  pl.BlockSpec((ROPE_HALF, LANES), lambda i: (0, 0))],
        out_specs=(out, out),
        compiler_params=pltpu.CompilerParams(dimension_semantics=("arbitrary",)),
        name="rope_tables",
    )(inv_freq.reshape(ROPE_HALF, 1), positions.reshape(1, t), spread)


def _mixer_in_kernel(x_ref, cos_ref, sin_ref, w_uv_ref, b_uv_ref, w_gate_ref, b_gate_ref,
                     w_lat_ref, b_lat_ref, ln_g_ref, ln_b_ref, ws_ref, bs_ref, w_gm_out_ref,
                     qn_g_ref, kvn_g_ref, w_uq_ref, w_uk_ref, w_uvv_ref, b_v_ref,
                     gm_ref, sg_ref, q_ref, k_ref, v_ref, mixed_ref, *, tm, sub, scale):
    parts = range(tm // sub)
    rows = [slice(part * sub, (part + 1) * sub) for part in parts]
    xb = [x_ref[r, :].astype(BF16) for r in rows]

    uv = [_gelu(_dot(x, w_uv_ref[...]) + b_uv_ref[...]) for x in xb]
    vn = [_layer_norm(t[:, GM_WIDTH:], ln_g_ref[...], ln_b_ref[...]).astype(BF16) for t in uv]
    r_i = lax.broadcasted_iota(jnp.int32, (GM_CHUNK, GM_CHUNK), 0)
    c_i = lax.broadcasted_iota(jnp.int32, (GM_CHUNK, GM_CHUNK), 1)
    causal = c_i <= r_i
    for g in range(GM_GROUPS):
        w = jnp.where(causal, ws_ref[g], jnp.zeros_like(ws_ref[g]))
        cols = slice(g * GM_CHUNK, (g + 1) * GM_CHUNK)
        for part in parts:
            for ch in range(sub // GM_CHUNK):
                local = slice(ch * GM_CHUNK, (ch + 1) * GM_CHUNK)
                dest = slice(part * sub + ch * GM_CHUNK, part * sub + (ch + 1) * GM_CHUNK)
                mixed_ref[dest, cols] = _dot(w, vn[part][local, cols]) + bs_ref[:, cols]
    gated = [(t[:, :GM_WIDTH] * mixed_ref[r, :]).astype(BF16) for t, r in zip(uv, rows)]
    y_gm = [_dot(t, w_gm_out_ref[...]) for t in gated]
    gates = [_dot(x, w_gate_ref[...]) + b_gate_ref[...] for x in xb]
    for r, gt, y in zip(rows, gates, y_gm):
        gm_ref[r, :] = (jax.nn.sigmoid(gt[:, :GM_WIDTH]) * y).astype(BF16)
        sg_ref[r, :] = jax.nn.sigmoid(gt[:, GM_WIDTH:]).astype(BF16)

    lat = [_dot(x, w_lat_ref[...]) + b_lat_ref[...] for x in xb]
    cq = [_rms_norm(t[:, :MLA_Q_LORA], qn_g_ref[...]).astype(BF16) for t in lat]
    ckv = [_rms_norm(t[:, MLA_Q_LORA:MLA_Q_LORA + MLA_KV_LORA], kvn_g_ref[...]).astype(BF16) for t in lat]
    qa = [_dot(t, w_uq_ref[...]) for t in cq]
    kn = [_dot(t, w_uk_ref[...]) for t in ckv]

    lane = lax.broadcasted_iota(jnp.int32, (sub, HEAD_PAD), 1)
    in_rope = (lane >= MLA_NOPE) & (lane < MLA_NOPE + MLA_ROPE)
    for part in parts:
        r = rows[part]
        cosv = cos_ref[r, :]
        sinv = sin_ref[r, :]
        cfac = jnp.where(lane < MLA_NOPE, 1.0, jnp.where(in_rope, cosv, 0.0))
        sfac = jnp.where(in_rope, jnp.where(lane < MLA_NOPE + ROPE_HALF, -sinv, sinv), 0.0)

        def rot(t, cfac=cfac, sfac=sfac):
            return t * cfac + pltpu.roll(t, HEAD_PAD - ROPE_HALF, 1) * sfac

        kr_rot = rot(lat[part][:, MLA_Q_LORA + MLA_KV_LORA:])
        for h in range(MLA_HEADS):
            cols = slice(h * HEAD_PAD, (h + 1) * HEAD_PAD)
            q_ref[r, cols] = (rot(qa[part][:, cols]) * scale).astype(BF16)
            k_ref[r, cols] = (kn[part][:, cols] + kr_rot).astype(BF16)
        v_ref[:, r] = (_dot_nt(w_uvv_ref[...], ckv[part]) + b_v_ref[...]).astype(BF16)


def _mixer_in(x2d, cos, sin, p):
    t, d = x2d.shape
    tm = TM_IN
    scale = (MLA_NOPE + MLA_ROPE) ** -0.5 * math.log2(math.e)
    row = lambda n: pl.BlockSpec((tm, n), lambda i: (i, 0))
    weights = [p["w_uv"], p["b_uv"], p["w_gate"], p["b_gate"], p["w_lat"], p["b_lat"],
               p["gm_ln_g"], p["gm_ln_b"], p["ws"], p["bs_full"], p["w_gm_out"],
               p["qn_g"], p["kvn_g"], p["w_uq"], p["w_uk"], p["w_uvv"], p["b_v"]]
    qk_w = MLA_HEADS * HEAD_PAD
    v_w = MLA_HEADS * V_ROWS
    return pl.pallas_call(
        functools.partial(_mixer_in_kernel, tm=tm, sub=TM_SUB, scale=scale),
        grid=(t // tm,),
        out_shape=(jax.ShapeDtypeStruct((t, d), BF16), jax.ShapeDtypeStruct((t, d), BF16),
                   jax.ShapeDtypeStruct((t, qk_w), BF16), jax.ShapeDtypeStruct((t, qk_w), BF16),
                   jax.ShapeDtypeStruct((v_w, t), BF16)),
        in_specs=[row(d), row(LANES), row(LANES)] + [_const_spec(w.shape) for w in weights],
        out_specs=(row(d), row(d), row(qk_w), row(qk_w), pl.BlockSpec((v_w, tm), lambda i: (0, i))),
        scratch_shapes=[pltpu.VMEM((tm, GM_WIDTH), F32)],
        compiler_params=pltpu.CompilerParams(dimension_semantics=("arbitrary",),
                                             vmem_limit_bytes=VMEM_LIMIT),
        name="mixer_in",
    )(x2d, cos, sin, *weights)


def _attn_kernel(q_ref, k_ref, vt_ref, o_ref, acc_ref, m_ref, r0_ref, r1_ref, s0_ref, s1_ref, p0_ref, p1_ref,
                 *, tq, tk):
    i = pl.program_id(2)
    acc_ref[...] = jnp.zeros_like(acc_ref)
    m_ref[...] = jnp.full(m_ref.shape, -jnp.inf, F32)

    every = slice(0, tq)

    def scores(j, s_ref, qs=every):
        kb = k_ref[pl.ds(pl.multiple_of(j * tk, tk), tk), :]
        for a in range(ATTN_HEADS):
            cols = slice(a * HEAD_PAD, (a + 1) * HEAD_PAD)
            s_ref[a, :, qs] = _dot_nt(kb[:, cols], q_ref[qs, cols])

    def softmax(s_ref, p_ref, r_ref, qs=every, diagonal=False):
        nch = tk // SM_CHUNK
        width = qs.stop - qs.start
        if diagonal:
            q_loc = lax.broadcasted_iota(jnp.int32, (SM_CHUNK, width), 1)
            k_loc = lax.broadcasted_iota(jnp.int32, (SM_CHUNK, width), 0)

        def chunk(a, c):
            s = s_ref[a, c * SM_CHUNK:(c + 1) * SM_CHUNK, qs]
            if diagonal:
                s = jnp.where(k_loc + c * SM_CHUNK <= q_loc, s, -jnp.inf)
            return s

        for a in range(ATTN_HEADS):
            m = m_ref[a:a + 1, qs]
            cm = chunk(a, 0)
            for c in range(1, nch):
                cm = jnp.maximum(cm, chunk(a, c))
            m_new = jnp.maximum(m, jnp.max(cm, axis=0, keepdims=True))
            r_ref[a:a + 1, qs] = jnp.exp2(m - m_new)
            m_ref[a:a + 1, qs] = m_new
            mb = jnp.broadcast_to(m_new, (SM_CHUNK, width))
            for c in range(nch):
                p_ref[a, c * SM_CHUNK:(c + 1) * SM_CHUNK, qs] = jnp.exp2(chunk(a, c) - mb).astype(BF16)

    def accumulate(j, p_ref, r_ref, qs=every):
        vb = vt_ref[:, pl.ds(pl.multiple_of(j * tk, tk), tk)]
        for a in range(ATTN_HEADS):
            rows = slice(a * V_ROWS, (a + 1) * V_ROWS)
            alpha = r_ref[a:a + 1, qs]
            acc_ref[rows, qs] = acc_ref[rows, qs] * alpha + _dot(vb[rows, :], p_ref[a, :, qs])

    def pair(base):
        scores(base + 1, s1_ref)
        softmax(s0_ref, p0_ref, r0_ref)
        accumulate(base, p0_ref, r0_ref)
        scores(base + 2, s0_ref)
        softmax(s1_ref, p1_ref, r1_ref)
        accumulate(base + 1, p1_ref, r1_ref)

    def main(t, carry):
        pair(4 * t)
        pair(4 * t + 2)
        return carry

    nsub = tq // tk
    first = nsub * i
    scores(0, s0_ref)
    lax.fori_loop(0, first // 4, main, 0)
    if nsub % 4:
        @pl.when(first % 4 == 2)
        def _():
            pair(first - 2)

    bufs = ((s0_ref, p0_ref, r0_ref), (s1_ref, p1_ref, r1_ref))
    for j in range(nsub):
        s_ref, p_ref, r_ref = bufs[j % 2]
        if j + 1 < nsub:
            scores(first + j + 1, bufs[(j + 1) % 2][0], slice((j + 1) * tk, tq))
        softmax(s_ref, p_ref, r_ref, slice(j * tk, (j + 1) * tk), diagonal=True)
        if j + 1 < nsub:
            softmax(s_ref, p_ref, r_ref, slice((j + 1) * tk, tq))
        accumulate(first + j, p_ref, r_ref, slice(j * tk, tq))
    out = [acc_ref[a * V_ROWS:a * V_ROWS + MLA_V, :] / acc_ref[a * V_ROWS + MLA_V:a * V_ROWS + MLA_V + 1, :]
           for a in range(ATTN_HEADS)]
    o_ref[...] = jnp.concatenate(out, axis=0).T.astype(BF16)


def _mla_attn(q, k, vt, batch, seq):
    assert TQ % (2 * TK) == 0, "visible key blocks are walked in pairs"
    nq = seq // TQ
    nh = ATTN_HEADS
    return pl.pallas_call(
        functools.partial(_attn_kernel, tq=TQ, tk=TK),
        grid=(batch, MLA_HEADS // nh, nq),
        out_shape=jax.ShapeDtypeStruct((batch * seq, MLA_HEADS * MLA_V), BF16),
        in_specs=[pl.BlockSpec((TQ, nh * HEAD_PAD), lambda b, h, i: (b * nq + i, h)),
                  pl.BlockSpec((seq, nh * HEAD_PAD), lambda b, h, i: (b, h)),
                  pl.BlockSpec((nh * V_ROWS, seq), lambda b, h, i: (h, b))],
        out_specs=pl.BlockSpec((TQ, nh * MLA_V), lambda b, h, i: (b * nq + i, h)),
        scratch_shapes=[pltpu.VMEM((nh * V_ROWS, TQ), F32)] + [pltpu.VMEM((8, TQ), F32)] * 3 + [
                        pltpu.VMEM((nh, TK, TQ), F32), pltpu.VMEM((nh, TK, TQ), F32),
                        pltpu.VMEM((nh, TK, TQ), BF16), pltpu.VMEM((nh, TK, TQ), BF16)],
        compiler_params=pltpu.CompilerParams(
            dimension_semantics=("arbitrary", "arbitrary", "arbitrary"), vmem_limit_bytes=VMEM_LIMIT),
        name="mla_attn",
    )(q, k, vt)


def _mem_kv_kernel(mem_ref, wk_ref, wv_ref, k_ref, v_ref):
    mb = mem_ref[...].astype(BF16)
    k_ref[...] = _dot(mb, wk_ref[...]).astype(BF16)
    v_ref[...] = _dot(mb, wv_ref[...]).astype(BF16)


def _mem_kv(mem2d, w_mk, w_mv, mem_len):
    rows, d = mem2d.shape
    blk = pl.BlockSpec((mem_len, d), lambda i: (i, 0))
    return pl.pallas_call(
        _mem_kv_kernel,
        grid=(rows // mem_len,),
        out_shape=(jax.ShapeDtypeStruct((rows, d), BF16),) * 2,
        in_specs=[blk, _const_spec(w_mk.shape), _const_spec(w_mv.shape)],
        out_specs=(blk, blk),
        compiler_params=pltpu.CompilerParams(dimension_semantics=("arbitrary",)),
        name="mem_kv",
    )(mem2d, w_mk, w_mv)


def _post_mixer_kernel(x_ref, o_ref, gm_ref, sg_ref, mk_ref, mv_ref,
                       w_mla_out_ref, w_o_ref, ln1_g_ref, ln1_b_ref,
                       w_mq_ref, w_mo_ref, ln2_g_ref, ln2_b_ref,
                       wr_hi_ref, wr_lo_ref, br_ref,
                       x2_ref, route_ref, idx_ref, counts_ref, run_ref, *, tm, sub, alpha, mem_scale):
    i = pl.program_id(0)

    @pl.when(i == 0)
    def _():
        run_ref[...] = jnp.zeros_like(run_ref)

    parts = range(tm // sub)
    rows = [slice(part * sub, (part + 1) * sub) for part in parts]
    y_mla = [_dot(o_ref[r, :], w_mla_out_ref[...]) for r in rows]
    merged = [(gm_ref[r, :].astype(F32) + sg_ref[r, :].astype(F32) * y).astype(BF16) for r, y in zip(rows, y_mla)]
    mixed = [_dot(m, w_o_ref[...]) for m in merged]
    x1 = [_layer_norm(alpha * x_ref[r, :] + h, ln1_g_ref[...], ln1_b_ref[...]) for r, h in zip(rows, mixed)]

    qm = [(_dot(x.astype(BF16), w_mq_ref[...]) * mem_scale).astype(BF16) for x in x1]
    hd = mk_ref.shape[1] // MEM_HEADS
    heads = [[] for _ in parts]
    for h in range(MEM_HEADS):
        cols = slice(h * hd, (h + 1) * hd)
        s = [_dot_nt(q[:, cols], mk_ref[:, cols]) for q in qm]
        pr = [jnp.exp(v - jnp.max(v, axis=1, keepdims=True)) for v in s]
        for part in parts:
            heads[part].append(_dot(pr[part].astype(BF16), mv_ref[:, cols])
                               / jnp.sum(pr[part], axis=1, keepdims=True))
    om = [jnp.concatenate(hs, axis=1).astype(BF16) for hs in heads]
    mem_out = [_dot(v, w_mo_ref[...]) for v in om]
    x2 = [_layer_norm(alpha * x + h, ln2_g_ref[...], ln2_b_ref[...]) for x, h in zip(x1, mem_out)]
    for r, v in zip(rows, x2):
        x2_ref[r, :] = v

    x_hi = [v.astype(BF16) for v in x2]
    x_lo = [(v - hi.astype(F32)).astype(BF16) for v, hi in zip(x2, x_hi)]
    logit = [_dot(hi, wr_hi_ref[...]) + _dot(lo, wr_hi_ref[...]) + _dot(hi, wr_lo_ref[...]) + br_ref[...]
             for hi, lo in zip(x_hi, x_lo)]
    lane = lax.broadcasted_iota(jnp.int32, (sub, LANES), 1).astype(F32)
    big = jnp.float32(1e9)
    r_i = lax.broadcasted_iota(jnp.int32, (sub, sub), 0)
    c_i = lax.broadcasted_iota(jnp.int32, (sub, sub), 1)
    tri = jnp.where(c_i < r_i, 1.0, 0.0).astype(BF16)

    def first_argmax(vals, vmax):
        return jnp.min(jnp.where(vals == vmax, lane, big), axis=1, keepdims=True)

    for part in parts:
        logits = logit[part]
        g_mask = (lane >= N_EXPERTS) & (lane < N_EXPERTS + N_GROUPS)
        lg = jnp.where(g_mask, logits, -jnp.inf)
        g_max = jnp.max(lg, axis=1, keepdims=True)
        g_sel = first_argmax(lg, g_max) - N_EXPERTS
        g_w = 1.0 / jnp.sum(jnp.where(g_mask, jnp.exp(logits - g_max), 0.0), axis=1, keepdims=True)
        in_group = jnp.floor(lane * (1.0 / EXPERTS_PER_GROUP)) == g_sel
        le = jnp.where(in_group, logits, -jnp.inf)
        v1 = jnp.max(le, axis=1, keepdims=True)
        e1 = first_argmax(le, v1)
        le2 = jnp.where(lane == e1, -jnp.inf, le)
        v2 = jnp.max(le2, axis=1, keepdims=True)
        e2 = first_argmax(le2, v2)
        t2 = jnp.exp(v2 - v1)
        w1 = (1.0 / (1.0 + t2)) * g_w
        w2 = (t2 / (1.0 + t2)) * g_w

        hit1 = lane == e1
        hit2 = lane == e2
        onehot = jnp.where(hit1 | hit2, 1.0, 0.0)
        before = _dot(tri, onehot.astype(BF16)) + run_ref[...]
        r1 = jnp.sum(jnp.where(hit1, before, 0.0), axis=1, keepdims=True)
        r2 = jnp.sum(jnp.where(hit2, before, 0.0), axis=1, keepdims=True)
        run_ref[...] = run_ref[...] + jnp.sum(onehot, axis=0, keepdims=True)

        packed = jnp.zeros((sub, LANES), F32)
        for pos, val in enumerate((e1, e2, w1, w2, r1, r2)):
            packed = jnp.where(lane == pos, val, packed)
        route_ref[rows[part], :] = packed
        idx_ref[part] = packed.T[:8, :]
    counts_ref[...] = jnp.broadcast_to(run_ref[...], counts_ref.shape)


def _post_mixer(x2d, o, gm, sg, mk, mv, p, seq, mem_len):
    t, d = x2d.shape
    tm, sub = TM_POST, TM_OUT
    assert seq % tm == 0, "a post_mixer row tile must not straddle two sequences (memory blocks are per sequence)"
    per_batch = seq // tm
    row = lambda n: pl.BlockSpec((tm, n), lambda i: (i, 0))
    memblk = pl.BlockSpec((mem_len, d), lambda i: (i // per_batch, 0))
    weights = [p["w_mla_out"], p["w_o"], p["ln1_g"], p["ln1_b"], p["w_mq"], p["w_mo"],
               p["ln2_g"], p["ln2_b"], p["wr_hi"], p["wr_lo"], p["br"]]
    return pl.pallas_call(
        functools.partial(_post_mixer_kernel, tm=tm, sub=sub, alpha=p["alpha"],
                          mem_scale=(d // MEM_HEADS) ** -0.5),
        grid=(t // tm,),
        out_shape=(jax.ShapeDtypeStruct((t, d), F32), jax.ShapeDtypeStruct((t, LANES), F32),
                   jax.ShapeDtypeStruct((t // sub, 8, sub), F32), jax.ShapeDtypeStruct((8, LANES), F32)),
        in_specs=[row(d), row(d), row(d), row(d), memblk, memblk]
                 + [_const_spec(w.shape) for w in weights],
        out_specs=(row(d), row(LANES), pl.BlockSpec((tm // sub, 8, sub), lambda i: (i, 0, 0)),
                   pl.BlockSpec((8, LANES), lambda i: (0, 0))),
        scratch_shapes=[pltpu.VMEM((1, LANES), F32)],
        compiler_params=pltpu.CompilerParams(dimension_semantics=("arbitrary",),
                                             vmem_limit_bytes=VMEM_LIMIT),
        name="post_mixer",
    )(x2d, o, gm, sg, mk, mv, *weights)


def _dispatch_kernel(last_ref, has_ref, nvalid_ref, dst_ref, x_hbm, xd_hbm, zbuf, xbuf, zsem, tsem, lsems, ssems,
                     *, tm, nsteps, rows, nblocks):
    i = pl.program_id(0)

    def load(step, s):
        return pltpu.make_async_copy(x_hbm.at[pl.ds(pl.multiple_of(step * tm, tm), tm)], xbuf.at[s], lsems.at[s])

    def drain(s):
        pltpu.make_async_copy(xd_hbm.at[pl.ds(0, 2 * tm)], xd_hbm.at[pl.ds(0, 2 * tm)], ssems.at[s]).wait()

    def zero_copy(start, sem):
        return pltpu.make_async_copy(zbuf, xd_hbm.at[pl.ds(pl.multiple_of(start, rows), rows)], sem)

    def unused_blocks(action):
        def body(blk, carry):
            action(zero_copy(blk * rows, tsem))
            return carry
        lax.fori_loop(nvalid_ref[0], nblocks, body, 0)

    @pl.when(i == 0)
    def _():
        load(0, 0).start()
        zbuf[...] = jnp.zeros_like(zbuf)
        for e in range(N_EXPERTS):
            @pl.when(has_ref[e] > 0)
            def _():
                zero_copy(last_ref[e], zsem).start()

        unused_blocks(lambda copy: copy.start())
        for e in range(N_EXPERTS):
            @pl.when(has_ref[e] > 0)
            def _():
                zero_copy(last_ref[e], zsem).wait()

    for s in range(3):
        @pl.when(i % 3 == s)
        def _():
            nxt = (s + 1) % 3
            load(i, s).wait()

            @pl.when(i >= 2)
            def _():
                drain(nxt)

            @pl.when(i + 1 < nsteps)
            def _():
                load(i + 1, nxt).start()

            for k in range(2):
                for r in range(tm):
                    pltpu.make_async_copy(xbuf.at[s, pl.ds(r, 1)],
                                          xd_hbm.at[pl.ds(dst_ref[0, 0, k * tm + r], 1)],
                                          ssems.at[s]).start(priority=r % 2)

            @pl.when(i == nsteps - 1)
            def _():
                if nsteps >= 2:
                    drain((s + 2) % 3)
                drain(s)
                unused_blocks(lambda copy: copy.wait())


def _moe_dispatch(x2, dst3, last_start, has_rows, nvalid, total_rows):
    t, d = x2.shape
    tm = TM_OUT
    nsteps = t // tm
    assert dst3.shape == (nsteps, 1, 2 * tm), "post_mixer and dispatch/combine tiles must coincide"
    grid_spec = pltpu.PrefetchScalarGridSpec(
        num_scalar_prefetch=3,
        grid=(nsteps,),
        in_specs=[pl.BlockSpec((1, 1, 2 * tm), lambda i, la, ha, nv: (i, 0, 0), memory_space=pltpu.SMEM),
                  pl.BlockSpec(memory_space=pl.ANY)],
        out_specs=pl.BlockSpec(memory_space=pl.ANY),
        scratch_shapes=[pltpu.VMEM((MOE_ROWS, d), F32), pltpu.VMEM((3, tm, d), F32),
                        pltpu.SemaphoreType.DMA(()), pltpu.SemaphoreType.DMA(()),
                        pltpu.SemaphoreType.DMA((3,)), pltpu.SemaphoreType.DMA((3,))],
    )
    return pl.pallas_call(
        functools.partial(_dispatch_kernel, tm=tm, nsteps=nsteps, rows=MOE_ROWS,
                          nblocks=total_rows // MOE_ROWS),
        grid_spec=grid_spec,
        out_shape=jax.ShapeDtypeStruct((total_rows, d), F32),
        compiler_params=pltpu.CompilerParams(dimension_semantics=("arbitrary",), has_side_effects=True),
        name="moe_dispatch",
    )(last_start, has_rows, nvalid, dst3, x2)


def _experts_kernel(be_ref, nvalid_ref, xd_ref, wg_ref, wu_ref, wd_ref, yd_ref, wgu_bf, wd_bf):
    i = pl.program_id(0)
    changed = jnp.logical_or(i == 0, be_ref[i] != be_ref[jnp.maximum(i - 1, 0)])

    @pl.when(changed)
    def _():
        wgu_bf[:, :D_EXPERT] = wg_ref[...].astype(BF16)
        wgu_bf[:, D_EXPERT:] = wu_ref[...].astype(BF16)
        wd_bf[...] = wd_ref[...].astype(BF16)

    @pl.when(i < nvalid_ref[0])
    def _():
        gu = _dot(xd_ref[...].astype(BF16), wgu_bf[...])
        gate = gu[:, :D_EXPERT]
        hidden = (gate * jax.nn.sigmoid(gate) * gu[:, D_EXPERT:]).astype(BF16)
        yd_ref[...] = _dot(hidden, wd_bf[...])

    @pl.when(i >= nvalid_ref[0])
    def _():
        yd_ref[...] = jnp.zeros_like(yd_ref)


def _moe_experts(xd, block_e, nvalid, w_gate, w_up, w_down):
    total, d = xd.shape
    rows = MOE_ROWS
    nblocks = total // rows
    grid_spec = pltpu.PrefetchScalarGridSpec(
        num_scalar_prefetch=2,
        grid=(nblocks,),
        in_specs=[pl.BlockSpec((rows, d), lambda i, be, nv: (jnp.minimum(i, nv[0] - 1), 0)),
                  pl.BlockSpec((None, d, D_EXPERT), lambda i, be, nv: (be[i], 0, 0)),
                  pl.BlockSpec((None, d, D_EXPERT), lambda i, be, nv: (be[i], 0, 0)),
                  pl.BlockSpec((None, D_EXPERT, d), lambda i, be, nv: (be[i], 0, 0))],
        out_specs=pl.BlockSpec((rows, d), lambda i, be, nv: (i, 0)),
        scratch_shapes=[pltpu.VMEM((d, 2 * D_EXPERT), BF16), pltpu.VMEM((D_EXPERT, d), BF16)],
    )
    return pl.pallas_call(
        _experts_kernel,
        grid_spec=grid_spec,
        out_shape=jax.ShapeDtypeStruct((total, d), F32),
        compiler_params=pltpu.CompilerParams(dimension_semantics=("arbitrary",),
                                             vmem_limit_bytes=VMEM_LIMIT),
        name="moe_experts",
    )(block_e, nvalid, xd, w_gate, w_up, w_down)


def _combine_kernel(dst_cur_ref, dst_nxt_ref, dst_nx2_ref, x2_ref, route_ref, g_ref, b_ref, yd_hbm,
                    out_ref, ybuf, sems, *, tm, nsteps, alpha):
    i = pl.program_id(0)

    def issue(idx_ref, s):
        for r in range(2 * tm):
            pltpu.make_async_copy(yd_hbm.at[pl.ds(idx_ref[0, 0, r], 1)], ybuf.at[s, pl.ds(r, 1)],
                                  sems.at[s]).start(priority=r % 2)

    def wait(s):
        pltpu.make_async_copy(yd_hbm.at[pl.ds(0, 2 * tm)], ybuf.at[s], sems.at[s]).wait()

    @pl.when(i == 0)
    def _():
        issue(dst_cur_ref, 0)
        issue(dst_nxt_ref, 1)

    for s in range(3):
        @pl.when(i % 3 == s)
        def _():
            wait(s)
            issue(dst_nx2_ref, (s + 2) % 3)
            route = route_ref[...]
            y = ybuf[s, :tm, :] * route[:, 2:3] + ybuf[s, tm:, :] * route[:, 3:4]
            out_ref[...] = _layer_norm(alpha * x2_ref[...] + y, g_ref[...], b_ref[...])

            @pl.when(i == nsteps - 1)
            def _():
                wait((s + 1) % 3)
                wait((s + 2) % 3)


def _moe_combine(x2, route, dst3, yd, ln_g, ln_b, alpha):
    t, d = x2.shape
    tm = TM_OUT
    nsteps = t // tm
    smem_blk = lambda f: pl.BlockSpec((1, 1, 2 * tm), f, memory_space=pltpu.SMEM)
    row = lambda n: pl.BlockSpec((tm, n), lambda i: (i, 0))
    return pl.pallas_call(
        functools.partial(_combine_kernel, tm=tm, nsteps=nsteps, alpha=alpha),
        grid=(nsteps,),
        out_shape=jax.ShapeDtypeStruct((t, d), F32),
        in_specs=[smem_blk(lambda i: (i, 0, 0)),
                  smem_blk(lambda i: (jnp.minimum(i + 1, nsteps - 1), 0, 0)),
                  smem_blk(lambda i: (jnp.minimum(i + 2, nsteps - 1), 0, 0)),
                  row(d), row(LANES), _const_spec(ln_g.shape), _const_spec(ln_b.shape),
                  pl.BlockSpec(memory_space=pl.ANY)],
        out_specs=row(d),
        scratch_shapes=[pltpu.VMEM((3, 2 * tm, d), F32), pltpu.SemaphoreType.DMA((3,))],
        compiler_params=pltpu.CompilerParams(dimension_semantics=("arbitrary",),
                                             vmem_limit_bytes=VMEM_LIMIT),
        name="moe_combine",
    )(dst3, dst3, dst3, x2, route, ln_g, ln_b, yd)


def _prep_layer(w_in, b_in, gm_ln_g, gm_ln_b, gm_w_s, gm_b_s, w_gm_out, mla_q_norm_g, mla_kv_norm_g,
                w_uq, w_uk, w_uv, w_mla_out, w_o, ln1_g, ln1_b, w_mq, w_mk, w_mv, w_mo, ln2_g, ln2_b,
                w_group_router, b_group_router, w_expert_router, b_expert_router, depth):
    d = w_in.shape[0]
    s_v = 2 * GM_WIDTH
    s_q = s_v + MLA_Q_LORA
    s_kv = s_q + MLA_KV_LORA
    s_r = s_kv + MLA_ROPE
    rowv = lambda a: a.reshape(1, -1).astype(F32)

    def lat_cols(a):
        z = jnp.zeros(a.shape[:-1] + (MLA_NOPE,), a.dtype)
        kr = a[..., s_kv:s_r]
        return jnp.concatenate([a[..., s_v:s_kv], z, kr, kr], axis=-1)

    wq3 = w_uq.reshape(MLA_Q_LORA, MLA_HEADS, MLA_NOPE + MLA_ROPE)
    wq_pad = jnp.concatenate([wq3, wq3[..., MLA_NOPE:]], axis=-1).reshape(MLA_Q_LORA, MLA_HEADS * HEAD_PAD)
    wk3 = w_uk.reshape(MLA_KV_LORA, MLA_HEADS, MLA_NOPE)
    wk_pad = jnp.pad(wk3, ((0, 0), (0, 0), (0, HEAD_PAD - MLA_NOPE))).reshape(MLA_KV_LORA, MLA_HEADS * HEAD_PAD)
    wv3 = w_uv.T.reshape(MLA_HEADS, MLA_V, MLA_KV_LORA)
    wv_pad = jnp.pad(wv3, ((0, 0), (0, V_ROWS - MLA_V), (0, 0))).reshape(MLA_HEADS * V_ROWS, MLA_KV_LORA)
    b_v = jnp.zeros((MLA_HEADS, V_ROWS, 1), F32).at[:, MLA_V].set(1.0).reshape(MLA_HEADS * V_ROWS, 1)
    w_r = jnp.zeros((d, LANES), F32)
    w_r = w_r.at[:, :N_EXPERTS].set(w_expert_router).at[:, N_EXPERTS:N_EXPERTS + N_GROUPS].set(w_group_router)
    b_r = jnp.zeros((LANES,), F32)
    b_r = b_r.at[:N_EXPERTS].set(b_expert_router).at[N_EXPERTS:N_EXPERTS + N_GROUPS].set(b_group_router)
    wr_hi = w_r.astype(BF16)
    return dict(
        w_uv=w_in[:, :s_v].astype(BF16), b_uv=rowv(b_in[:s_v]),
        w_gate=w_in[:, s_r:].astype(BF16), b_gate=rowv(b_in[s_r:]),
        w_lat=lat_cols(w_in).astype(BF16), b_lat=rowv(lat_cols(b_in)),
        gm_ln_g=rowv(gm_ln_g), gm_ln_b=rowv(gm_ln_b),
        ws=gm_w_s.astype(BF16), bs_full=jnp.repeat(gm_b_s.T, GM_CHUNK, axis=1).astype(F32),
        w_gm_out=w_gm_out.astype(BF16), qn_g=rowv(mla_q_norm_g), kvn_g=rowv(mla_kv_norm_g),
        w_uq=wq_pad.astype(BF16), w_uk=wk_pad.astype(BF16), w_uvv=wv_pad.astype(BF16), b_v=b_v,
        w_mla_out=w_mla_out.astype(BF16), w_o=w_o.astype(BF16), ln1_g=rowv(ln1_g), ln1_b=rowv(ln1_b),
        w_mq=w_mq.astype(BF16), w_mk=w_mk.astype(BF16), w_mv=w_mv.astype(BF16), w_mo=w_mo.astype(BF16),
        ln2_g=rowv(ln2_g), ln2_b=rowv(ln2_b),
        wr_hi=wr_hi, wr_lo=(w_r - wr_hi.astype(F32)).astype(BF16), br=rowv(b_r),
        alpha=(2 * depth) ** 0.25,
    )


def _dispatch_tables(idx, counts, t):
    rows = MOE_ROWS
    tiles, _, tm = idx.shape
    e = idx[:, 0:2, :].astype(jnp.int32)
    rank = idx[:, 4:6, :].astype(jnp.int32)
    cnt = counts[0, :N_EXPERTS].astype(jnp.int32)
    padded = (cnt + rows - 1) // rows * rows
    ends = jnp.cumsum(padded)
    starts = ends - padded
    ids = jnp.arange(N_EXPERTS, dtype=jnp.int32)[:, None, None, None]
    start_of = jnp.sum(jnp.where(e[None] == ids, starts[:, None, None, None], 0), axis=0)
    dst3 = (start_of + rank).reshape(tiles, 1, 2 * tm)
    total = 2 * t + N_EXPERTS * rows
    block_start = jnp.arange(total // rows, dtype=jnp.int32) * rows
    block_e = jnp.minimum(jnp.sum(ends[None, :] <= block_start[:, None], axis=1), N_EXPERTS - 1)
    nvalid = (ends[-1] // rows).reshape(1).astype(jnp.int32)
    last_start = jnp.maximum(ends - rows, 0).astype(jnp.int32)
    has_rows = (cnt > 0).astype(jnp.int32)
    return dst3, block_e.astype(jnp.int32), nvalid, last_start, has_rows, total


def _layer(x, mem, positions, depth, w_in, b_in, gm_ln_g, gm_ln_b, gm_w_s, gm_b_s, w_gm_out,
           mla_q_norm_g, mla_kv_norm_g, w_uq, w_uk, w_uv, w_mla_out, w_o, ln1_g, ln1_b,
           w_mq, w_mk, w_mv, w_mo, ln2_g, ln2_b,
           w_group_router, b_group_router, w_expert_router, b_expert_router,
           w_exp_gate, w_exp_up, w_exp_down, ln3_g, ln3_b):
    b, s, d = x.shape
    t = b * s
    mem_len = mem.shape[1]
    p = _prep_layer(w_in, b_in, gm_ln_g, gm_ln_b, gm_w_s, gm_b_s, w_gm_out, mla_q_norm_g, mla_kv_norm_g,
                    w_uq, w_uk, w_uv, w_mla_out, w_o, ln1_g, ln1_b, w_mq, w_mk, w_mv, w_mo, ln2_g, ln2_b,
                    w_group_router, b_group_router, w_expert_router, b_expert_router, depth)
    x2d = x.reshape(t, d)
    inv_freq = ROPE_THETA ** (-jnp.arange(ROPE_HALF, dtype=F32) / ROPE_HALF)
    cos, sin = _rope_tables(positions, inv_freq)
    gm, sg, q, k, v = _mixer_in(x2d, cos, sin, p)
    o = _mla_attn(q, k, v, b, s)
    mk, mv = _mem_kv(mem.reshape(b * mem_len, d), p["w_mk"], p["w_mv"], mem_len)
    x2, route, idx, counts = _post_mixer(x2d, o, gm, sg, mk, mv, p, s, mem_len)
    dst3, block_e, nvalid, last_start, has_rows, total = _dispatch_tables(idx, counts, t)
    xd = _moe_dispatch(x2, dst3, last_start, has_rows, nvalid, total)
    yd = _moe_experts(xd, block_e, nvalid, w_exp_gate, w_exp_up, w_exp_down)
    out = _moe_combine(x2, route, dst3, yd, ln3_g.reshape(1, d), ln3_b.reshape(1, d), p["alpha"])
    return out.reshape(b, s, d)


def kernel(x, mem, positions, w_in, b_in, gm_ln_g, gm_ln_b, gm_w_s, gm_b_s, w_gm_out, mla_q_norm_g, mla_kv_norm_g, w_uq, w_uk, w_uv, w_mla_out, w_o, ln1_g, ln1_b, w_mq, w_mk, w_mv, w_mo, ln2_g, ln2_b, w_group_router, b_group_router, w_expert_router, b_expert_router, w_exp_gate, w_exp_up, w_exp_down, ln3_g, ln3_b):
    depth = w_in.shape[0]
    per_layer = (w_in, b_in, gm_ln_g, gm_ln_b, gm_w_s, gm_b_s, w_gm_out, mla_q_norm_g, mla_kv_norm_g,
                 w_uq, w_uk, w_uv, w_mla_out, w_o, ln1_g, ln1_b, w_mq, w_mk, w_mv, w_mo, ln2_g, ln2_b,
                 w_group_router, b_group_router, w_expert_router, b_expert_router,
                 w_exp_gate, w_exp_up, w_exp_down, ln3_g, ln3_b)
    h = x
    for l in range(depth):
        h = _layer(h, mem, positions, depth, *[w[l] for w in per_layer])
    return h
```

```python
import functools
import math

import jax
import jax.numpy as jnp
from jax import lax
from jax.experimental import pallas as pl
from jax.experimental.pallas import tpu as pltpu

F32 = jnp.float32
BF16 = jnp.bfloat16

GM_WIDTH = 1024
GM_GROUPS = 8
GM_CHUNK = 128
MLA_HEADS = 16
MLA_NOPE = 64
MLA_ROPE = 32
MLA_V = 64
MLA_Q_LORA = 384
MLA_KV_LORA = 256
ROPE_THETA = 10000.0
MEM_HEADS = 4
N_GROUPS = 8
EXPERTS_PER_GROUP = 8
N_EXPERTS = N_GROUPS * EXPERTS_PER_GROUP
D_EXPERT = 256
LN_EPS = 1e-5
RMS_EPS = 1e-6

LANES = 128
HEAD_PAD = 128
V_ROWS = 80
ROPE_HALF = MLA_ROPE // 2

ROPE_BLOCK = 4096
TM_IN = 512
TM_SUB = 256
TQ = 1024
TK = 256
ATTN_HEADS = 2
SM_CHUNK = 16
TM_POST = 1024
POST_SUB = 256
MOE_ROWS = 512
TM_OUT = 512
VMEM_LIMIT = 56 * 1024 * 1024


def _const_spec(shape):
    nd = len(shape)
    return pl.BlockSpec(shape, lambda *_: (0,) * nd, pipeline_mode=pl.Buffered(1))


def _layer_norm(x, g, b):
    mu = jnp.mean(x, axis=-1, keepdims=True)
    xc = x - mu
    var = jnp.mean(xc * xc, axis=-1, keepdims=True)
    return xc * lax.rsqrt(var + LN_EPS) * g + b


def _rms_norm(x, g):
    return x * lax.rsqrt(jnp.mean(x * x, axis=-1, keepdims=True) + RMS_EPS) * g


def _gelu(x):
    return x * (lax.erf(x * (1.0 / math.sqrt(2.0))) + 1.0) * 0.5


def _dot(a, b):
    return jnp.dot(a, b, preferred_element_type=F32)


def _dot_nt(a, b):
    return lax.dot_general(a, b, (((1,), (1,)), ((), ())), preferred_element_type=F32)


def _rope_kernel(inv_ref, pos_ref, spread_ref, cos_ref, sin_ref):
    ang = inv_ref[...] * pos_ref[...].astype(F32)
    def to_rows(v):
        out = None
        for _ in range(3):
            piece = v.astype(BF16)
            v = v - piece.astype(F32)
            term = lax.dot_general(piece, spread_ref[...], (((0,), (0,)), ((), ())),
                                   preferred_element_type=F32)
            out = term if out is None else out + term
        return out

    cos_ref[...] = to_rows(jnp.cos(ang))
    sin_ref[...] = to_rows(jnp.sin(ang))


def _rope_tables(positions, inv_freq):
    t = positions.size
    blk = min(t, ROPE_BLOCK)
    spread = (jnp.arange(LANES)[None, :] % ROPE_HALF == jnp.arange(ROPE_HALF)[:, None]).astype(BF16)
    out = pl.BlockSpec((blk, LANES), lambda i: (i, 0))
    return pl.pallas_call(
        _rope_kernel,
        grid=(t // blk,),
        out_shape=(jax.ShapeDtypeStruct((t, LANES), F32),) * 2,
        in_specs=[pl.BlockSpec((ROPE_HALF, 1), lambda i: (0, 0)),
                  pl.BlockSpec((1, blk), lambda i: (0, i)),
                  pl.BlockSpec((ROPE_HALF, LANES), lambda i: (0, 0))],
        out_specs=(out, out),
        compiler_params=pltpu.CompilerParams(dimension_semantics=("arbitrary",)),
        name="rope_tables",
    )(inv_freq.reshape(ROPE_HALF, 1), positions.reshape(1, t), spread)


def _mixer_in_kernel(x_ref, cos_ref, sin_ref, w_uv_ref, b_uv_ref, w_gate_ref, b_gate_ref,
                     w_lat_ref, b_lat_ref, ln_g_ref, ln_b_ref, ws_ref, bs_ref, w_gm_out_ref,
                     qn_g_ref, kvn_g_ref, w_uq_ref, w_uk_ref, w_uvv_ref, b_v_ref,
                     gm_ref, sg_ref, q_ref, k_ref, v_ref, *, tm, sub, scale):
    parts = range(tm // sub)
    rows = [slice(part * sub, (part + 1) * sub) for part in parts]
    xb = [x_ref[r, :].astype(BF16) for r in rows]

    uv = [_gelu(_dot(x, w_uv_ref[...]) + b_uv_ref[...]) for x in xb]
    vn = [_layer_norm(t[:, GM_WIDTH:], ln_g_ref[...], ln_b_ref[...]).astype(BF16) for t in uv]
    r_i = lax.broadcasted_iota(jnp.int32, (GM_CHUNK, GM_CHUNK), 0)
    c_i = lax.broadcasted_iota(jnp.int32, (GM_CHUNK, GM_CHUNK), 1)
    causal = c_i <= r_i
    nchunk = sub // GM_CHUNK
    blocks = [[[None] * GM_GROUPS for _ in range(nchunk)] for _ in parts]
    for g in range(GM_GROUPS):
        w = jnp.where(causal, ws_ref[g], jnp.zeros_like(ws_ref[g]))
        cols = slice(g * GM_CHUNK, (g + 1) * GM_CHUNK)
        for part in parts:
            for ch in range(nchunk):
                local = slice(ch * GM_CHUNK, (ch + 1) * GM_CHUNK)
                mixed = _dot(w, vn[part][local, cols]) + bs_ref[:, cols]
                blocks[part][ch][g] = (uv[part][local, cols] * mixed).astype(BF16)
    gated = [jnp.concatenate([jnp.concatenate(chunk, axis=1) for chunk in blocks[part]], axis=0)
             for part in parts]
    y_gm = [_dot(t, w_gm_out_ref[...]) for t in gated]
    gates = [_dot(x, w_gate_ref[...]) + b_gate_ref[...] for x in xb]
    for r, gt, y in zip(rows, gates, y_gm):
        gm_ref[r, :] = (jax.nn.sigmoid(gt[:, :GM_WIDTH]) * y).astype(BF16)
        sg_ref[r, :] = jax.nn.sigmoid(gt[:, GM_WIDTH:]).astype(BF16)

    lat = [_dot(x, w_lat_ref[...]) + b_lat_ref[...] for x in xb]
    cq = [_rms_norm(t[:, :MLA_Q_LORA], qn_g_ref[...]).astype(BF16) for t in lat]
    ckv = [_rms_norm(t[:, MLA_Q_LORA:MLA_Q_LORA + MLA_KV_LORA], kvn_g_ref[...]).astype(BF16) for t in lat]
    qa = [_dot(t, w_uq_ref[...]) for t in cq]
    kn = [_dot(t, w_uk_ref[...]) for t in ckv]

    lane = lax.broadcasted_iota(jnp.int32, (sub, HEAD_PAD), 1)
    in_rope = (lane >= MLA_NOPE) & (lane < MLA_NOPE + MLA_ROPE)
    for part in parts:
        r = rows[part]
        cosv = cos_ref[r, :]
        sinv = sin_ref[r, :]
        cfac = jnp.where(lane < MLA_NOPE, 1.0, jnp.where(in_rope, cosv, 0.0))
        sfac = jnp.where(in_rope, jnp.where(lane < MLA_NOPE + ROPE_HALF, -sinv, sinv), 0.0)

        def rot(t, cfac=cfac, sfac=sfac):
            return t * cfac + pltpu.roll(t, HEAD_PAD - ROPE_HALF, 1) * sfac

        kr_rot = rot(lat[part][:, MLA_Q_LORA + MLA_KV_LORA:])
        for h in range(MLA_HEADS):
            cols = slice(h * HEAD_PAD, (h + 1) * HEAD_PAD)
            q_ref[r, cols] = (rot(qa[part][:, cols]) * scale).astype(BF16)
            k_ref[r, cols] = (kn[part][:, cols] + kr_rot).astype(BF16)
        v_ref[:, r] = (_dot_nt(w_uvv_ref[...], ckv[part]) + b_v_ref[...]).astype(BF16)


def _mixer_in(x2d, cos, sin, p):
    t, d = x2d.shape
    tm = TM_IN
    scale = (MLA_NOPE + MLA_ROPE) ** -0.5 * math.log2(math.e)
    row = lambda n: pl.BlockSpec((tm, n), lambda i: (i, 0))
    weights = [p["w_uv"], p["b_uv"], p["w_gate"], p["b_gate"], p["w_lat"], p["b_lat"],
               p["gm_ln_g"], p["gm_ln_b"], p["ws"], p["bs_full"], p["w_gm_out"],
               p["qn_g"], p["kvn_g"], p["w_uq"], p["w_uk"], p["w_uvv"], p["b_v"]]
    qk_w = MLA_HEADS * HEAD_PAD
    v_w = MLA_HEADS * V_ROWS
    return pl.pallas_call(
        functools.partial(_mixer_in_kernel, tm=tm, sub=TM_SUB, scale=scale),
        grid=(t // tm,),
        out_shape=(jax.ShapeDtypeStruct((t, d), BF16), jax.ShapeDtypeStruct((t, d), BF16),
                   jax.ShapeDtypeStruct((t, qk_w), BF16), jax.ShapeDtypeStruct((t, qk_w), BF16),
                   jax.ShapeDtypeStruct((v_w, t), BF16)),
        in_specs=[row(d), row(LANES), row(LANES)] + [_const_spec(w.shape) for w in weights],
        out_specs=(row(d), row(d), row(qk_w), row(qk_w), pl.BlockSpec((v_w, tm), lambda i: (0, i))),
        compiler_params=pltpu.CompilerParams(dimension_semantics=("arbitrary",),
                                             vmem_limit_bytes=VMEM_LIMIT),
        name="mixer_in",
    )(x2d, cos, sin, *weights)


def _attn_kernel(q_ref, k_ref, vt_ref, o_ref, acc_ref, m_ref, r0_ref, r1_ref, s0_ref, s1_ref, p0_ref, p1_ref,
                 *, tq, tk):
    i = pl.program_id(2)
    acc_ref[...] = jnp.zeros_like(acc_ref)
    m_ref[...] = jnp.full(m_ref.shape, -jnp.inf, F32)

    every = slice(0, tq)

    def scores(j, s_ref, qs=every):
        kb = k_ref[pl.ds(pl.multiple_of(j * tk, tk), tk), :]
        for a in range(ATTN_HEADS):
            cols = slice(a * HEAD_PAD, (a + 1) * HEAD_PAD)
            s_ref[a, :, qs] = _dot_nt(kb[:, cols], q_ref[qs, cols])

    def softmax(s_ref, p_ref, r_ref, qs=every, diagonal=False):
        nch = tk // SM_CHUNK
        width = qs.stop - qs.start
        if diagonal:
            q_loc = lax.broadcasted_iota(jnp.int32, (SM_CHUNK, width), 1)
            k_loc = lax.broadcasted_iota(jnp.int32, (SM_CHUNK, width), 0)

        def chunk(a, c):
            s = s_ref[a, c * SM_CHUNK:(c + 1) * SM_CHUNK, qs]
            if diagonal:
                s = jnp.where(k_loc + c * SM_CHUNK <= q_loc, s, -jnp.inf)
            return s

        for a in range(ATTN_HEADS):
            m = m_ref[a:a + 1, qs]
            cm = chunk(a, 0)
            for c in range(1, nch):
                cm = jnp.maximum(cm, chunk(a, c))
            m_new = jnp.maximum(m, jnp.max(cm, axis=0, keepdims=True))
            r_ref[a:a + 1, qs] = jnp.exp2(m - m_new)
            m_ref[a:a + 1, qs] = m_new
            mb = jnp.broadcast_to(m_new, (SM_CHUNK, width))
            for c in range(nch):
                p_ref[a, c * SM_CHUNK:(c + 1) * SM_CHUNK, qs] = jnp.exp2(chunk(a, c) - mb).astype(BF16)

    def accumulate(j, p_ref, r_ref, qs=every):
        vb = vt_ref[:, pl.ds(pl.multiple_of(j * tk, tk), tk)]
        for a in range(ATTN_HEADS):
            rows = slice(a * V_ROWS, (a + 1) * V_ROWS)
            alpha = r_ref[a:a + 1, qs]
            acc_ref[rows, qs] = acc_ref[rows, qs] * alpha + _dot(vb[rows, :], p_ref[a, :, qs])

    def pair(base):
        scores(base + 1, s1_ref)
        softmax(s0_ref, p0_ref, r0_ref)
        accumulate(base, p0_ref, r0_ref)
        scores(base + 2, s0_ref)
        softmax(s1_ref, p1_ref, r1_ref)
        accumulate(base + 1, p1_ref, r1_ref)

    def main(t, carry):
        pair(4 * t)
        pair(4 * t + 2)
        return carry

    nsub = tq // tk
    first = nsub * i
    scores(0, s0_ref)
    lax.fori_loop(0, first // 4, main, 0)
    if nsub % 4:
        @pl.when(first % 4 == 2)
        def _():
            pair(first - 2)

    bufs = ((s0_ref, p0_ref, r0_ref), (s1_ref, p1_ref, r1_ref))
    for j in range(nsub):
        s_ref, p_ref, r_ref = bufs[j % 2]
        if j + 1 < nsub:
            scores(first + j + 1, bufs[(j + 1) % 2][0], slice((j + 1) * tk, tq))
        softmax(s_ref, p_ref, r_ref, slice(j * tk, (j + 1) * tk), diagonal=True)
        if j + 1 < nsub:
            softmax(s_ref, p_ref, r_ref, slice((j + 1) * tk, tq))
        accumulate(first + j, p_ref, r_ref, slice(j * tk, tq))
    out = [acc_ref[a * V_ROWS:a * V_ROWS + MLA_V, :] / acc_ref[a * V_ROWS + MLA_V:a * V_ROWS + MLA_V + 1, :]
           for a in range(ATTN_HEADS)]
    o_ref[...] = jnp.concatenate(out, axis=0).T.astype(BF16)


def _mla_attn(q, k, vt, batch, seq):
    assert TQ % (2 * TK) == 0, "visible key blocks are walked in pairs"
    nq = seq // TQ
    nh = ATTN_HEADS
    return pl.pallas_call(
        functools.partial(_attn_kernel, tq=TQ, tk=TK),
        grid=(batch, MLA_HEADS // nh, nq),
        out_shape=jax.ShapeDtypeStruct((batch * seq, MLA_HEADS * MLA_V), BF16),
        in_specs=[pl.BlockSpec((TQ, nh * HEAD_PAD), lambda b, h, i: (b * nq + i, h)),
                  pl.BlockSpec((seq, nh * HEAD_PAD), lambda b, h, i: (b, h)),
                  pl.BlockSpec((nh * V_ROWS, seq), lambda b, h, i: (h, b))],
        out_specs=pl.BlockSpec((TQ, nh * MLA_V), lambda b, h, i: (b * nq + i, h)),
        scratch_shapes=[pltpu.VMEM((nh * V_ROWS, TQ), F32)] + [pltpu.VMEM((8, TQ), F32)] * 3 + [
                        pltpu.VMEM((nh, TK, TQ), F32), pltpu.VMEM((nh, TK, TQ), F32),
                        pltpu.VMEM((nh, TK, TQ), BF16), pltpu.VMEM((nh, TK, TQ), BF16)],
        compiler_params=pltpu.CompilerParams(
            dimension_semantics=("arbitrary", "arbitrary", "arbitrary"), vmem_limit_bytes=VMEM_LIMIT),
        name="mla_attn",
    )(q, k, vt)


def _mem_kv_kernel(mem_ref, wk_ref, wv_ref, k_ref, v_ref):
    mb = mem_ref[...].astype(BF16)
    k_ref[...] = _dot(mb, wk_ref[...]).astype(BF16)
    v_ref[...] = _dot(mb, wv_ref[...]).astype(BF16)


def _mem_kv(mem2d, w_mk, w_mv, mem_len):
    rows, d = mem2d.shape
    blk = pl.BlockSpec((mem_len, d), lambda i: (i, 0))
    return pl.pallas_call(
        _mem_kv_kernel,
        grid=(rows // mem_len,),
        out_shape=(jax.ShapeDtypeStruct((rows, d), BF16),) * 2,
        in_specs=[blk, _const_spec(w_mk.shape), _const_spec(w_mv.shape)],
        out_specs=(blk, blk),
        compiler_params=pltpu.CompilerParams(dimension_semantics=("arbitrary",)),
        name="mem_kv",
    )(mem2d, w_mk, w_mv)


def _post_mixer_kernel(x_ref, o_ref, gm_ref, sg_ref, mk_ref, mv_ref,
                       w_mla_out_ref, w_o_ref, ln1_g_ref, ln1_b_ref,
                       w_mq_ref, w_mo_ref, ln2_g_ref, ln2_b_ref,
                       wr_hi_ref, wr_lo_ref, br_ref,
                       x2_ref, route_ref, idx_ref, counts_ref, run_ref, *, tm, sub, alpha, mem_scale):
    i = pl.program_id(0)

    @pl.when(i == 0)
    def _():
        run_ref[...] = jnp.zeros_like(run_ref)

    parts = range(tm // sub)
    rows = [slice(part * sub, (part + 1) * sub) for part in parts]
    y_mla = [_dot(o_ref[r, :], w_mla_out_ref[...]) for r in rows]
    merged = [(gm_ref[r, :].astype(F32) + sg_ref[r, :].astype(F32) * y).astype(BF16) for r, y in zip(rows, y_mla)]
    mixed = [_dot(m, w_o_ref[...]) for m in merged]
    x1 = [_layer_norm(alpha * x_ref[r, :] + h, ln1_g_ref[...], ln1_b_ref[...]) for r, h in zip(rows, mixed)]

    qm = [(_dot(x.astype(BF16), w_mq_ref[...]) * mem_scale).astype(BF16) for x in x1]
    hd = mk_ref.shape[1] // MEM_HEADS
    heads = [[] for _ in parts]
    for h in range(MEM_HEADS):
        cols = slice(h * hd, (h + 1) * hd)
        s = [_dot_nt(q[:, cols], mk_ref[:, cols]) for q in qm]
        pr = [jnp.exp(v - jnp.max(v, axis=1, keepdims=True)) for v in s]
        for part in parts:
            heads[part].append(_dot(pr[part].astype(BF16), mv_ref[:, cols])
                               / jnp.sum(pr[part], axis=1, keepdims=True))
    om = [jnp.concatenate(hs, axis=1).astype(BF16) for hs in heads]
    mem_out = [_dot(v, w_mo_ref[...]) for v in om]
    x2 = [_layer_norm(alpha * x + h, ln2_g_ref[...], ln2_b_ref[...]) for x, h in zip(x1, mem_out)]
    for r, v in zip(rows, x2):
        x2_ref[r, :] = v

    x_hi = [v.astype(BF16) for v in x2]
    x_lo = [(v - hi.astype(F32)).astype(BF16) for v, hi in zip(x2, x_hi)]
    logit = [_dot(hi, wr_hi_ref[...]) + _dot(lo, wr_hi_ref[...]) + _dot(hi, wr_lo_ref[...]) + br_ref[...]
             for hi, lo in zip(x_hi, x_lo)]
    lane = lax.broadcasted_iota(jnp.int32, (sub, LANES), 1).astype(F32)
    big = jnp.float32(1e9)
    r_i = lax.broadcasted_iota(jnp.int32, (sub, sub), 0)
    c_i = lax.broadcasted_iota(jnp.int32, (sub, sub), 1)
    tri = jnp.where(c_i < r_i, 1.0, 0.0).astype(BF16)

    def first_argmax(vals, vmax):
        return jnp.min(jnp.where(vals == vmax, lane, big), axis=1, keepdims=True)

    for part in parts:
        logits = logit[part]
        g_mask = (lane >= N_EXPERTS) & (lane < N_EXPERTS + N_GROUPS)
        lg = jnp.where(g_mask, logits, -jnp.inf)
        g_max = jnp.max(lg, axis=1, keepdims=True)
        g_sel = first_argmax(lg, g_max) - N_EXPERTS
        g_w = 1.0 / jnp.sum(jnp.where(g_mask, jnp.exp(logits - g_max), 0.0), axis=1, keepdims=True)
        in_group = jnp.floor(lane * (1.0 / EXPERTS_PER_GROUP)) == g_sel
        le = jnp.where(in_group, logits, -jnp.inf)
        v1 = jnp.max(le, axis=1, keepdims=True)
        e1 = first_argmax(le, v1)
        le2 = jnp.where(lane == e1, -jnp.inf, le)
        v2 = jnp.max(le2, axis=1, keepdims=True)
        e2 = first_argmax(le2, v2)
        t2 = jnp.exp(v2 - v1)
        w1 = (1.0 / (1.0 + t2)) * g_w
        w2 = (t2 / (1.0 + t2)) * g_w

        hit1 = lane == e1
        hit2 = lane == e2
        onehot = jnp.where(hit1 | hit2, 1.0, 0.0)
        before = _dot(tri, onehot.astype(BF16)) + run_ref[...]
        r1 = jnp.sum(jnp.where(hit1, before, 0.0), axis=1, keepdims=True)
        r2 = jnp.sum(jnp.where(hit2, before, 0.0), axis=1, keepdims=True)
        run_ref[...] = run_ref[...] + jnp.sum(onehot, axis=0, keepdims=True)

        packed = jnp.zeros((sub, LANES), F32)
        for pos, val in enumerate((e1, e2, w1, w2, r1, r2)):
            packed = jnp.where(lane == pos, val, packed)
        route_ref[rows[part], :] = packed
        idx_ref[part] = packed.T[:8, :]
    counts_ref[...] = jnp.broadcast_to(run_ref[...], counts_ref.shape)


def _post_mixer(x2d, o, gm, sg, mk, mv, p, seq, mem_len):
    t, d = x2d.shape
    tm, sub = TM_POST, POST_SUB
    assert seq % tm == 0, "a post_mixer row tile must not straddle two sequences (memory blocks are per sequence)"
    per_batch = seq // tm
    row = lambda n: pl.BlockSpec((tm, n), lambda i: (i, 0))
    memblk = pl.BlockSpec((mem_len, d), lambda i: (i // per_batch, 0))
    weights = [p["w_mla_out"], p["w_o"], p["ln1_g"], p["ln1_b"], p["w_mq"], p["w_mo"],
               p["ln2_g"], p["ln2_b"], p["wr_hi"], p["wr_lo"], p["br"]]
    return pl.pallas_call(
        functools.partial(_post_mixer_kernel, tm=tm, sub=sub, alpha=p["alpha"],
                          mem_scale=(d // MEM_HEADS) ** -0.5),
        grid=(t // tm,),
        out_shape=(jax.ShapeDtypeStruct((t, d), F32), jax.ShapeDtypeStruct((t, LANES), F32),
                   jax.ShapeDtypeStruct((t // sub, 8, sub), F32), jax.ShapeDtypeStruct((8, LANES), F32)),
        in_specs=[row(d), row(d), row(d), row(d), memblk, memblk]
                 + [_const_spec(w.shape) for w in weights],
        out_specs=(row(d), row(LANES), pl.BlockSpec((tm // sub, 8, sub), lambda i: (i, 0, 0)),
                   pl.BlockSpec((8, LANES), lambda i: (0, 0))),
        scratch_shapes=[pltpu.VMEM((1, LANES), F32)],
        compiler_params=pltpu.CompilerParams(dimension_semantics=("arbitrary",),
                                             vmem_limit_bytes=VMEM_LIMIT),
        name="post_mixer",
    )(x2d, o, gm, sg, mk, mv, *weights)


def _dispatch_kernel(last_ref, has_ref, nvalid_ref, dst_ref, x_hbm, xd_hbm, zbuf, xbuf, zsem, tsem, lsems, ssems,
                     *, tm, nsteps, rows, nblocks):
    i = pl.program_id(0)

    def load(step, s):
        return pltpu.make_async_copy(x_hbm.at[pl.ds(pl.multiple_of(step * tm, tm), tm)], xbuf.at[s], lsems.at[s])

    def drain(s):
        pltpu.make_async_copy(xd_hbm.at[pl.ds(0, 2 * tm)], xd_hbm.at[pl.ds(0, 2 * tm)], ssems.at[s]).wait()

    def zero_copy(start, sem):
        return pltpu.make_async_copy(zbuf, xd_hbm.at[pl.ds(pl.multiple_of(start, rows), rows)], sem)

    def unused_blocks(action):
        def body(blk, carry):
            action(zero_copy(blk * rows, tsem))
            return carry
        lax.fori_loop(nvalid_ref[0], nblocks, body, 0)

    @pl.when(i == 0)
    def _():
        load(0, 0).start()
        zbuf[...] = jnp.zeros_like(zbuf)
        for e in range(N_EXPERTS):
            @pl.when(has_ref[e] > 0)
            def _():
                zero_copy(last_ref[e], zsem).start()

        unused_blocks(lambda copy: copy.start())
        for e in range(N_EXPERTS):
            @pl.when(has_ref[e] > 0)
            def _():
                zero_copy(last_ref[e], zsem).wait()

    for s in range(3):
        @pl.when(i % 3 == s)
        def _():
            nxt = (s + 1) % 3
            load(i, s).wait()

            @pl.when(i >= 2)
            def _():
                drain(nxt)

            @pl.when(i + 1 < nsteps)
            def _():
                load(i + 1, nxt).start()

            for k in range(2):
                for r in range(tm):
                    pltpu.make_async_copy(xbuf.at[s, pl.ds(r, 1)],
                                          xd_hbm.at[pl.ds(dst_ref[0, 0, k * tm + r], 1)],
                                          ssems.at[s]).start(priority=r % 2)

            @pl.when(i == nsteps - 1)
            def _():
                if nsteps >= 2:
                    drain((s + 2) % 3)
                drain(s)
                unused_blocks(lambda copy: copy.wait())


def _moe_dispatch(x2, dst3, last_start, has_rows, nvalid, total_rows):
    t, d = x2.shape
    tm = TM_OUT
    nsteps = t // tm
    assert dst3.shape == (nsteps, 1, 2 * tm), "post_mixer and dispatch/combine tiles must coincide"
    grid_spec = pltpu.PrefetchScalarGridSpec(
        num_scalar_prefetch=3,
        grid=(nsteps,),
        in_specs=[pl.BlockSpec((1, 1, 2 * tm), lambda i, la, ha, nv: (i, 0, 0), memory_space=pltpu.SMEM),
                  pl.BlockSpec(memory_space=pl.ANY)],
        out_specs=pl.BlockSpec(memory_space=pl.ANY),
        scratch_shapes=[pltpu.VMEM((MOE_ROWS, d), F32), pltpu.VMEM((3, tm, d), F32),
                        pltpu.SemaphoreType.DMA(()), pltpu.SemaphoreType.DMA(()),
                        pltpu.SemaphoreType.DMA((3,)), pltpu.SemaphoreType.DMA((3,))],
    )
    return pl.pallas_call(
        functools.partial(_dispatch_kernel, tm=tm, nsteps=nsteps, rows=MOE_ROWS,
                          nblocks=total_rows // MOE_ROWS),
        grid_spec=grid_spec,
        out_shape=jax.ShapeDtypeStruct((total_rows, d), F32),
        compiler_params=pltpu.CompilerParams(dimension_semantics=("arbitrary",), has_side_effects=True),
        name="moe_dispatch",
    )(last_start, has_rows, nvalid, dst3, x2)


def _experts_kernel(be_ref, nvalid_ref, xd_ref, wg_ref, wu_ref, wd_ref, yd_ref, wgu_bf, wd_bf):
    i = pl.program_id(0)
    changed = jnp.logical_or(i == 0, be_ref[i] != be_ref[jnp.maximum(i - 1, 0)])

    @pl.when(changed)
    def _():
        wgu_bf[:, :D_EXPERT] = wg_ref[...].astype(BF16)
        wgu_bf[:, D_EXPERT:] = wu_ref[...].astype(BF16)
        wd_bf[...] = wd_ref[...].astype(BF16)

    @pl.when(i < nvalid_ref[0])
    def _():
        gu = _dot(xd_ref[...].astype(BF16), wgu_bf[...])
        gate = gu[:, :D_EXPERT]
        hidden = (gate * jax.nn.sigmoid(gate) * gu[:, D_EXPERT:]).astype(BF16)
        yd_ref[...] = _dot(hidden, wd_bf[...])

    @pl.when(i >= nvalid_ref[0])
    def _():
        yd_ref[...] = jnp.zeros_like(yd_ref)


def _moe_experts(xd, block_e, nvalid, w_gate, w_up, w_down):
    total, d = xd.shape
    rows = MOE_ROWS
    nblocks = total // rows
    grid_spec = pltpu.PrefetchScalarGridSpec(
        num_scalar_prefetch=2,
        grid=(nblocks,),
        in_specs=[pl.BlockSpec((rows, d), lambda i, be, nv: (jnp.minimum(i, nv[0] - 1), 0)),
                  pl.BlockSpec((None, d, D_EXPERT), lambda i, be, nv: (be[i], 0, 0)),
                  pl.BlockSpec((None, d, D_EXPERT), lambda i, be, nv: (be[i], 0, 0)),
                  pl.BlockSpec((None, D_EXPERT, d), lambda i, be, nv: (be[i], 0, 0))],
        out_specs=pl.BlockSpec((rows, d), lambda i, be, nv: (i, 0)),
        scratch_shapes=[pltpu.VMEM((d, 2 * D_EXPERT), BF16), pltpu.VMEM((D_EXPERT, d), BF16)],
    )
    return pl.pallas_call(
        _experts_kernel,
        grid_spec=grid_spec,
        out_shape=jax.ShapeDtypeStruct((total, d), F32),
        compiler_params=pltpu.CompilerParams(dimension_semantics=("arbitrary",),
                                             vmem_limit_bytes=VMEM_LIMIT),
        name="moe_experts",
    )(block_e, nvalid, xd, w_gate, w_up, w_down)


def _combine_kernel(dst_cur_ref, dst_nxt_ref, dst_nx2_ref, x2_ref, route_ref, g_ref, b_ref, yd_hbm,
                    out_ref, ybuf, sems, *, tm, nsteps, alpha):
    i = pl.program_id(0)

    def issue(idx_ref, s):
        for r in range(2 * tm):
            pltpu.make_async_copy(yd_hbm.at[pl.ds(idx_ref[0, 0, r], 1)], ybuf.at[s, pl.ds(r, 1)],
                                  sems.at[s]).start(priority=r % 2)

    def wait(s):
        pltpu.make_async_copy(yd_hbm.at[pl.ds(0, 2 * tm)], ybuf.at[s], sems.at[s]).wait()

    @pl.when(i == 0)
    def _():
        issue(dst_cur_ref, 0)
        issue(dst_nxt_ref, 1)

    for s in range(3):
        @pl.when(i % 3 == s)
        def _():
            wait(s)
            issue(dst_nx2_ref, (s + 2) % 3)
            route = route_ref[...]
            y = ybuf[s, :tm, :] * route[:, 2:3] + ybuf[s, tm:, :] * route[:, 3:4]
            out_ref[...] = _layer_norm(alpha * x2_ref[...] + y, g_ref[...], b_ref[...])

            @pl.when(i == nsteps - 1)
            def _():
                wait((s + 1) % 3)
                wait((s + 2) % 3)


def _moe_combine(x2, route, dst3, yd, ln_g, ln_b, alpha):
    t, d = x2.shape
    tm = TM_OUT
    nsteps = t // tm
    smem_blk = lambda f: pl.BlockSpec((1, 1, 2 * tm), f, memory_space=pltpu.SMEM)
    row = lambda n: pl.BlockSpec((tm, n), lambda i: (i, 0))
    return pl.pallas_call(
        functools.partial(_combine_kernel, tm=tm, nsteps=nsteps, alpha=alpha),
        grid=(nsteps,),
        out_shape=jax.ShapeDtypeStruct((t, d), F32),
        in_specs=[smem_blk(lambda i: (i, 0, 0)),
                  smem_blk(lambda i: (jnp.minimum(i + 1, nsteps - 1), 0, 0)),
                  smem_blk(lambda i: (jnp.minimum(i + 2, nsteps - 1), 0, 0)),
                  row(d), row(LANES), _const_spec(ln_g.shape), _const_spec(ln_b.shape),
                  pl.BlockSpec(memory_space=pl.ANY)],
        out_specs=row(d),
        scratch_shapes=[pltpu.VMEM((3, 2 * tm, d), F32), pltpu.SemaphoreType.DMA((3,))],
        compiler_params=pltpu.CompilerParams(dimension_semantics=("arbitrary",),
                                             vmem_limit_bytes=VMEM_LIMIT),
        name="moe_combine",
    )(dst3, dst3, dst3, x2, route, ln_g, ln_b, yd)


def _prep_layer(w_in, b_in, gm_ln_g, gm_ln_b, gm_w_s, gm_b_s, w_gm_out, mla_q_norm_g, mla_kv_norm_g,
                w_uq, w_uk, w_uv, w_mla_out, w_o, ln1_g, ln1_b, w_mq, w_mk, w_mv, w_mo, ln2_g, ln2_b,
                w_group_router, b_group_router, w_expert_router, b_expert_router, depth):
    d = w_in.shape[0]
    s_v = 2 * GM_WIDTH
    s_q = s_v + MLA_Q_LORA
    s_kv = s_q + MLA_KV_LORA
    s_r = s_kv + MLA_ROPE
    rowv = lambda a: a.reshape(1, -1).astype(F32)

    def lat_cols(a):
        z = jnp.zeros(a.shape[:-1] + (MLA_NOPE,), a.dtype)
        kr = a[..., s_kv:s_r]
        return jnp.concatenate([a[..., s_v:s_kv], z, kr, kr], axis=-1)

    wq3 = w_uq.reshape(MLA_Q_LORA, MLA_HEADS, MLA_NOPE + MLA_ROPE)
    wq_pad = jnp.concatenate([wq3, wq3[..., MLA_NOPE:]], axis=-1).reshape(MLA_Q_LORA, MLA_HEADS * HEAD_PAD)
    wk3 = w_uk.reshape(MLA_KV_LORA, MLA_HEADS, MLA_NOPE)
    wk_pad = jnp.pad(wk3, ((0, 0), (0, 0), (0, HEAD_PAD - MLA_NOPE))).reshape(MLA_KV_LORA, MLA_HEADS * HEAD_PAD)
    wv3 = w_uv.T.reshape(MLA_HEADS, MLA_V, MLA_KV_LORA)
    wv_pad = jnp.pad(wv3, ((0, 0), (0, V_ROWS - MLA_V), (0, 0))).reshape(MLA_HEADS * V_ROWS, MLA_KV_LORA)
    b_v = jnp.zeros((MLA_HEADS, V_ROWS, 1), F32).at[:, MLA_V].set(1.0).reshape(MLA_HEADS * V_ROWS, 1)
    w_r = jnp.zeros((d, LANES), F32)
    w_r = w_r.at[:, :N_EXPERTS].set(w_expert_router).at[:, N_EXPERTS:N_EXPERTS + N_GROUPS].set(w_group_router)
    b_r = jnp.zeros((LANES,), F32)
    b_r = b_r.at[:N_EXPERTS].set(b_expert_router).at[N_EXPERTS:N_EXPERTS + N_GROUPS].set(b_group_router)
    wr_hi = w_r.astype(BF16)
    return dict(
        w_uv=w_in[:, :s_v].astype(BF16), b_uv=rowv(b_in[:s_v]),
        w_gate=w_in[:, s_r:].astype(BF16), b_gate=rowv(b_in[s_r:]),
        w_lat=lat_cols(w_in).astype(BF16), b_lat=rowv(lat_cols(b_in)),
        gm_ln_g=rowv(gm_ln_g), gm_ln_b=rowv(gm_ln_b),
        ws=gm_w_s.astype(BF16), bs_full=jnp.repeat(gm_b_s.T, GM_CHUNK, axis=1).astype(F32),
        w_gm_out=w_gm_out.astype(BF16), qn_g=rowv(mla_q_norm_g), kvn_g=rowv(mla_kv_norm_g),
        w_uq=wq_pad.astype(BF16), w_uk=wk_pad.astype(BF16), w_uvv=wv_pad.astype(BF16), b_v=b_v,
        w_mla_out=w_mla_out.astype(BF16), w_o=w_o.astype(BF16), ln1_g=rowv(ln1_g), ln1_b=rowv(ln1_b),
        w_mq=w_mq.astype(BF16), w_mk=w_mk.astype(BF16), w_mv=w_mv.astype(BF16), w_mo=w_mo.astype(BF16),
        ln2_g=rowv(ln2_g), ln2_b=rowv(ln2_b),
        wr_hi=wr_hi, wr_lo=(w_r - wr_hi.astype(F32)).astype(BF16), br=rowv(b_r),
        alpha=(2 * depth) ** 0.25,
    )


def _dispatch_tables(idx, counts, t):
    rows = MOE_ROWS
    tiles, _, tm = idx.shape
    e = idx[:, 0:2, :].astype(jnp.int32)
    rank = idx[:, 4:6, :].astype(jnp.int32)
    cnt = counts[0, :N_EXPERTS].astype(jnp.int32)
    padded = (cnt + rows - 1) // rows * rows
    ends = jnp.cumsum(padded)
    starts = ends - padded
    ids = jnp.arange(N_EXPERTS, dtype=jnp.int32)[:, None, None, None]
    start_of = jnp.sum(jnp.where(e[None] == ids, starts[:, None, None, None], 0), axis=0)
    group = TM_OUT // tm
    dst3 = (start_of + rank).reshape(tiles // group, group, 2, tm).transpose(0, 2, 1, 3)
    dst3 = dst3.reshape(tiles // group, 1, 2 * TM_OUT)
    total = 2 * t + N_EXPERTS * rows
    block_start = jnp.arange(total // rows, dtype=jnp.int32) * rows
    block_e = jnp.minimum(jnp.sum(ends[None, :] <= block_start[:, None], axis=1), N_EXPERTS - 1)
    nvalid = (ends[-1] // rows).reshape(1).astype(jnp.int32)
    last_start = jnp.maximum(ends - rows, 0).astype(jnp.int32)
    has_rows = (cnt > 0).astype(jnp.int32)
    return dst3, block_e.astype(jnp.int32), nvalid, last_start, has_rows, total


def _layer(x, mem, positions, depth, w_in, b_in, gm_ln_g, gm_ln_b, gm_w_s, gm_b_s, w_gm_out,
           mla_q_norm_g, mla_kv_norm_g, w_uq, w_uk, w_uv, w_mla_out, w_o, ln1_g, ln1_b,
           w_mq, w_mk, w_mv, w_mo, ln2_g, ln2_b,
           w_group_router, b_group_router, w_expert_router, b_expert_router,
           w_exp_gate, w_exp_up, w_exp_down, ln3_g, ln3_b):
    b, s, d = x.shape
    t = b * s
    mem_len = mem.shape[1]
    p = _prep_layer(w_in, b_in, gm_ln_g, gm_ln_b, gm_w_s, gm_b_s, w_gm_out, mla_q_norm_g, mla_kv_norm_g,
                    w_uq, w_uk, w_uv, w_mla_out, w_o, ln1_g, ln1_b, w_mq, w_mk, w_mv, w_mo, ln2_g, ln2_b,
                    w_group_router, b_group_router, w_expert_router, b_expert_router, depth)
    x2d = x.reshape(t, d)
    inv_freq = ROPE_THETA ** (-jnp.arange(ROPE_HALF, dtype=F32) / ROPE_HALF)
    cos, sin = _rope_tables(positions, inv_freq)
    gm, sg, q, k, v = _mixer_in(x2d, cos, sin, p)
    o = _mla_attn(q, k, v, b, s)
    mk, mv = _mem_kv(mem.reshape(b * mem_len, d), p["w_mk"], p["w_mv"], mem_len)
    x2, route, idx, counts = _post_mixer(x2d, o, gm, sg, mk, mv, p, s, mem_len)
    dst3, block_e, nvalid, last_start, has_rows, total = _dispatch_tables(idx, counts, t)
    xd = _moe_dispatch(x2, dst3, last_start, has_rows, nvalid, total)
    yd = _moe_experts(xd, block_e, nvalid, w_exp_gate, w_exp_up, w_exp_down)
    out = _moe_combine(x2, route, dst3, yd, ln3_g.reshape(1, d), ln3_b.reshape(1, d), p["alpha"])
    return out.reshape(b, s, d)


def kernel(x, mem, positions, w_in, b_in, gm_ln_g, gm_ln_b, gm_w_s, gm_b_s, w_gm_out, mla_q_norm_g, mla_kv_norm_g, w_uq, w_uk, w_uv, w_mla_out, w_o, ln1_g, ln1_b, w_mq, w_mk, w_mv, w_mo, ln2_g, ln2_b, w_group_router, b_group_router, w_expert_router, b_expert_router, w_exp_gate, w_exp_up, w_exp_down, ln3_g, ln3_b):
    depth = w_in.shape[0]
    per_layer = (w_in, b_in, gm_ln_g, gm_ln_b, gm_w_s, gm_b_s, w_gm_out, mla_q_norm_g, mla_kv_norm_g,
                 w_uq, w_uk, w_uv, w_mla_out, w_o, ln1_g, ln1_b, w_mq, w_mk, w_mv, w_mo, ln2_g, ln2_b,
                 w_group_router, b_group_router, w_expert_router, b_expert_router,
                 w_exp_gate, w_exp_up, w_exp_down, ln3_g, ln3_b)
    h = x
    for l in range(depth):
        h = _layer(h, mem, positions, depth, *[w[l] for w in per_layer])
    return h
```

```python
import functools
import math

import jax
import jax.numpy as jnp
from jax import lax
from jax.experimental import pallas as pl
from jax.experimental.pallas import tpu as pltpu

F32 = jnp.float32
BF16 = jnp.bfloat16

GM_WIDTH = 1024
GM_GROUPS = 8
GM_CHUNK = 128
MLA_HEADS = 16
MLA_NOPE = 64
MLA_ROPE = 32
MLA_V = 64
MLA_Q_LORA = 384
MLA_KV_LORA = 256
ROPE_THETA = 10000.0
MEM_HEADS = 4
N_GROUPS = 8
EXPERTS_PER_GROUP = 8
N_EXPERTS = N_GROUPS * EXPERTS_PER_GROUP
D_EXPERT = 256
LN_EPS = 1e-5
RMS_EPS = 1e-6

LANES = 128
HEAD_PAD = 128
V_ROWS = 80
ROPE_HALF = MLA_ROPE // 2

ROPE_BLOCK = 4096
TM_IN = 512
TM_SUB = 256
TQ = 2048
TK = 256
ATTN_HEADS = 2
SM_CHUNK = 16
TM_POST = 1024
POST_SUB = 256
MOE_ROWS = 512
TM_OUT = 512
VMEM_LIMIT = 56 * 1024 * 1024


def _const_spec(shape):
    nd = len(shape)
    return pl.BlockSpec(shape, lambda *_: (0,) * nd, pipeline_mode=pl.Buffered(1))


def _layer_norm(x, g, b):
    mu = jnp.mean(x, axis=-1, keepdims=True)
    xc = x - mu
    var = jnp.mean(xc * xc, axis=-1, keepdims=True)
    return xc * lax.rsqrt(var + LN_EPS) * g + b


def _rms_norm(x, g):
    return x * lax.rsqrt(jnp.mean(x * x, axis=-1, keepdims=True) + RMS_EPS) * g


def _gelu(x):
    return x * (lax.erf(x * (1.0 / math.sqrt(2.0))) + 1.0) * 0.5


def _dot(a, b):
    return jnp.dot(a, b, preferred_element_type=F32)


def _dot_nt(a, b):
    return lax.dot_general(a, b, (((1,), (1,)), ((), ())), preferred_element_type=F32)


def _rope_kernel(inv_ref, pos_ref, spread_ref, cos_ref, sin_ref):
    ang = inv_ref[...] * pos_ref[...].astype(F32)
    def to_rows(v):
        out = None
        for _ in range(3):
            piece = v.astype(BF16)
            v = v - piece.astype(F32)
            term = lax.dot_general(piece, spread_ref[...], (((0,), (0,)), ((), ())),
                                   preferred_element_type=F32)
            out = term if out is None else out + term
        return out

    cos_ref[...] = to_rows(jnp.cos(ang))
    sin_ref[...] = to_rows(jnp.sin(ang))


def _rope_tables(positions, inv_freq):
    t = positions.size
    blk = min(t, ROPE_BLOCK)
    spread = (jnp.arange(LANES)[None, :] % ROPE_HALF == jnp.arange(ROPE_HALF)[:, None]).astype(BF16)
    out = pl.BlockSpec((blk, LANES), lambda i: (i, 0))
    return pl.pallas_call(
        _rope_kernel,
        grid=(t // blk,),
        out_shape=(jax.ShapeDtypeStruct((t, LANES), F32),) * 2,
        in_specs=[pl.BlockSpec((ROPE_HALF, 1), lambda i: (0, 0)),
                  pl.BlockSpec((1, blk), lambda i: (0, i)),
                  pl.BlockSpec((ROPE_HALF, LANES), lambda i: (0, 0))],
        out_specs=(out, out),
        compiler_params=pltpu.CompilerParams(dimension_semantics=("arbitrary",)),
        name="rope_tables",
    )(inv_freq.reshape(ROPE_HALF, 1), positions.reshape(1, t), spread)


def _mixer_in_kernel(x_ref, cos_ref, sin_ref, w_uv_ref, b_uv_ref, w_gate_ref, b_gate_ref,
                     w_lat_ref, b_lat_ref, ln_g_ref, ln_b_ref, ws_ref, bs_ref, w_gm_out_ref,
                     qn_g_ref, kvn_g_ref, w_uq_ref, w_uk_ref, w_uvv_ref, b_v_ref,
                     gm_ref, sg_ref, q_ref, k_ref, v_ref, *, tm, sub, scale):
    parts = range(tm // sub)
    rows = [slice(part * sub, (part + 1) * sub) for part in parts]
    xb = [x_ref[r, :].astype(BF16) for r in rows]

    uv = [_gelu(_dot(x, w_uv_ref[...]) + b_uv_ref[...]) for x in xb]
    vn = [_layer_norm(t[:, GM_WIDTH:], ln_g_ref[...], ln_b_ref[...]).astype(BF16) for t in uv]
    r_i = lax.broadcasted_iota(jnp.int32, (GM_CHUNK, GM_CHUNK), 0)
    c_i = lax.broadcasted_iota(jnp.int32, (GM_CHUNK, GM_CHUNK), 1)
    causal = c_i <= r_i
    nchunk = sub // GM_CHUNK
    blocks = [[[None] * GM_GROUPS for _ in range(nchunk)] for _ in parts]
    for g in range(GM_GROUPS):
        w = jnp.where(causal, ws_ref[g], jnp.zeros_like(ws_ref[g]))
        cols = slice(g * GM_CHUNK, (g + 1) * GM_CHUNK)
        for part in parts:
            for ch in range(nchunk):
                local = slice(ch * GM_CHUNK, (ch + 1) * GM_CHUNK)
                mixed = _dot(w, vn[part][local, cols]) + bs_ref[:, cols]
                blocks[part][ch][g] = (uv[part][local, cols] * mixed).astype(BF16)
    gated = [jnp.concatenate([jnp.concatenate(chunk, axis=1) for chunk in blocks[part]], axis=0)
             for part in parts]
    y_gm = [_dot(t, w_gm_out_ref[...]) for t in gated]
    gates = [_dot(x, w_gate_ref[...]) + b_gate_ref[...] for x in xb]
    for r, gt, y in zip(rows, gates, y_gm):
        gm_ref[r, :] = (jax.nn.sigmoid(gt[:, :GM_WIDTH]) * y).astype(BF16)
        sg_ref[r, :] = jax.nn.sigmoid(gt[:, GM_WIDTH:]).astype(BF16)

    lat = [_dot(x, w_lat_ref[...]) + b_lat_ref[...] for x in xb]
    cq = [_rms_norm(t[:, :MLA_Q_LORA], qn_g_ref[...]).astype(BF16) for t in lat]
    ckv = [_rms_norm(t[:, MLA_Q_LORA:MLA_Q_LORA + MLA_KV_LORA], kvn_g_ref[...]).astype(BF16) for t in lat]
    qa = [_dot(t, w_uq_ref[...]) for t in cq]
    kn = [_dot(t, w_uk_ref[...]) for t in ckv]

    lane = lax.broadcasted_iota(jnp.int32, (sub, HEAD_PAD), 1)
    in_rope = (lane >= MLA_NOPE) & (lane < MLA_NOPE + MLA_ROPE)
    for part in parts:
        r = rows[part]
        cosv = cos_ref[r, :]
        sinv = sin_ref[r, :]
        cfac = jnp.where(lane < MLA_NOPE, 1.0, jnp.where(in_rope, cosv, 0.0))
        sfac = jnp.where(in_rope, jnp.where(lane < MLA_NOPE + ROPE_HALF, -sinv, sinv), 0.0)

        def rot(t, cfac=cfac, sfac=sfac):
            return t * cfac + pltpu.roll(t, HEAD_PAD - ROPE_HALF, 1) * sfac

        kr_rot = rot(lat[part][:, MLA_Q_LORA + MLA_KV_LORA:])
        for h in range(MLA_HEADS):
            cols = slice(h * HEAD_PAD, (h + 1) * HEAD_PAD)
            q_ref[r, cols] = (rot(qa[part][:, cols]) * scale).astype(BF16)
            k_ref[r, cols] = (kn[part][:, cols] + kr_rot).astype(BF16)
        v_ref[:, r] = (_dot_nt(w_uvv_ref[...], ckv[part]) + b_v_ref[...]).astype(BF16)


def _mixer_in(x2d, cos, sin, p):
    t, d = x2d.shape
    tm = TM_IN
    scale = (MLA_NOPE + MLA_ROPE) ** -0.5 * math.log2(math.e)
    row = lambda n: pl.BlockSpec((tm, n), lambda i: (i, 0))
    weights = [p["w_uv"], p["b_uv"], p["w_gate"], p["b_gate"], p["w_lat"], p["b_lat"],
               p["gm_ln_g"], p["gm_ln_b"], p["ws"], p["bs_full"], p["w_gm_out"],
               p["qn_g"], p["kvn_g"], p["w_uq"], p["w_uk"], p["w_uvv"], p["b_v"]]
    qk_w = MLA_HEADS * HEAD_PAD
    v_w = MLA_HEADS * V_ROWS
    return pl.pallas_call(
        functools.partial(_mixer_in_kernel, tm=tm, sub=TM_SUB, scale=scale),
        grid=(t // tm,),
        out_shape=(jax.ShapeDtypeStruct((t, d), BF16), jax.ShapeDtypeStruct((t, d), BF16),
                   jax.ShapeDtypeStruct((t, qk_w), BF16), jax.ShapeDtypeStruct((t, qk_w), BF16),
                   jax.ShapeDtypeStruct((v_w, t), BF16)),
        in_specs=[row(d), row(LANES), row(LANES)] + [_const_spec(w.shape) for w in weights],
        out_specs=(row(d), row(d), row(qk_w), row(qk_w), pl.BlockSpec((v_w, tm), lambda i: (0, i))),
        compiler_params=pltpu.CompilerParams(dimension_semantics=("arbitrary",),
                                             vmem_limit_bytes=VMEM_LIMIT),
        name="mixer_in",
    )(x2d, cos, sin, *weights)


def _attn_kernel(q_ref, k_ref, vt_ref, o_ref, acc_ref, m_ref, r0_ref, r1_ref, s0_ref, s1_ref, p0_ref, p1_ref,
                 *, tq, tk):
    i = pl.program_id(2)
    acc_ref[...] = jnp.zeros_like(acc_ref)
    m_ref[...] = jnp.full(m_ref.shape, -jnp.inf, F32)

    every = slice(0, tq)

    def scores(j, s_ref, qs=every):
        kb = k_ref[pl.ds(pl.multiple_of(j * tk, tk), tk), :]
        for a in range(ATTN_HEADS):
            cols = slice(a * HEAD_PAD, (a + 1) * HEAD_PAD)
            s_ref[a, :, qs] = _dot_nt(kb[:, cols], q_ref[qs, cols])

    def softmax(s_ref, p_ref, r_ref, qs=every, diagonal=False):
        nch = tk // SM_CHUNK
        width = qs.stop - qs.start
        if diagonal:
            q_loc = lax.broadcasted_iota(jnp.int32, (SM_CHUNK, width), 1)
            k_loc = lax.broadcasted_iota(jnp.int32, (SM_CHUNK, width), 0)

        def chunk(a, c):
            s = s_ref[a, c * SM_CHUNK:(c + 1) * SM_CHUNK, qs]
            if diagonal:
                s = jnp.where(k_loc + c * SM_CHUNK <= q_loc, s, -jnp.inf)
            return s

        for a in range(ATTN_HEADS):
            m = m_ref[a:a + 1, qs]
            cm = chunk(a, 0)
            for c in range(1, nch):
                cm = jnp.maximum(cm, chunk(a, c))
            m_new = jnp.maximum(m, jnp.max(cm, axis=0, keepdims=True))
            r_ref[a:a + 1, qs] = jnp.exp2(m - m_new)
            m_ref[a:a + 1, qs] = m_new
            mb = jnp.broadcast_to(m_new, (SM_CHUNK, width))
            for c in range(nch):
                p_ref[a, c * SM_CHUNK:(c + 1) * SM_CHUNK, qs] = jnp.exp2(chunk(a, c) - mb).astype(BF16)

    def accumulate(j, p_ref, r_ref, qs=every):
        vb = vt_ref[:, pl.ds(pl.multiple_of(j * tk, tk), tk)]
        for a in range(ATTN_HEADS):
            rows = slice(a * V_ROWS, (a + 1) * V_ROWS)
            alpha = r_ref[a:a + 1, qs]
            acc_ref[rows, qs] = acc_ref[rows, qs] * alpha + _dot(vb[rows, :], p_ref[a, :, qs])

    def pair(base):
        scores(base + 1, s1_ref)
        softmax(s0_ref, p0_ref, r0_ref)
        accumulate(base, p0_ref, r0_ref)
        scores(base + 2, s0_ref)
        softmax(s1_ref, p1_ref, r1_ref)
        accumulate(base + 1, p1_ref, r1_ref)

    def main(t, carry):
        pair(4 * t)
        pair(4 * t + 2)
        return carry

    nsub = tq // tk
    first = nsub * i
    scores(0, s0_ref)
    lax.fori_loop(0, first // 4, main, 0)
    if nsub % 4:
        @pl.when(first % 4 == 2)
        def _():
            pair(first - 2)

    bufs = ((s0_ref, p0_ref, r0_ref), (s1_ref, p1_ref, r1_ref))
    for j in range(nsub):
        s_ref, p_ref, r_ref = bufs[j % 2]
        if j + 1 < nsub:
            scores(first + j + 1, bufs[(j + 1) % 2][0], slice((j + 1) * tk, tq))
        softmax(s_ref, p_ref, r_ref, slice(j * tk, (j + 1) * tk), diagonal=True)
        if j + 1 < nsub:
            softmax(s_ref, p_ref, r_ref, slice((j + 1) * tk, tq))
        accumulate(first + j, p_ref, r_ref, slice(j * tk, tq))
    out = [acc_ref[a * V_ROWS:a * V_ROWS + MLA_V, :] / acc_ref[a * V_ROWS + MLA_V:a * V_ROWS + MLA_V + 1, :]
           for a in range(ATTN_HEADS)]
    o_ref[...] = jnp.concatenate(out, axis=0).T.astype(BF16)


def _mla_attn(q, k, vt, batch, seq):
    assert TQ % (2 * TK) == 0, "visible key blocks are walked in pairs"
    nq = seq // TQ
    nh = ATTN_HEADS
    return pl.pallas_call(
        functools.partial(_attn_kernel, tq=TQ, tk=TK),
        grid=(batch, MLA_HEADS // nh, nq),
        out_shape=jax.ShapeDtypeStruct((batch * seq, MLA_HEADS * MLA_V), BF16),
        in_specs=[pl.BlockSpec((TQ, nh * HEAD_PAD), lambda b, h, i: (b * nq + i, h)),
                  pl.BlockSpec((seq, nh * HEAD_PAD), lambda b, h, i: (b, h)),
                  pl.BlockSpec((nh * V_ROWS, seq), lambda b, h, i: (h, b))],
        out_specs=pl.BlockSpec((TQ, nh * MLA_V), lambda b, h, i: (b * nq + i, h)),
        scratch_shapes=[pltpu.VMEM((nh * V_ROWS, TQ), F32)] + [pltpu.VMEM((8, TQ), F32)] * 3 + [
                        pltpu.VMEM((nh, TK, TQ), F32), pltpu.VMEM((nh, TK, TQ), F32),
                        pltpu.VMEM((nh, TK, TQ), BF16), pltpu.VMEM((nh, TK, TQ), BF16)],
        compiler_params=pltpu.CompilerParams(
            dimension_semantics=("arbitrary", "arbitrary", "arbitrary"), vmem_limit_bytes=VMEM_LIMIT),
        name="mla_attn",
    )(q, k, vt)


def _mem_kv_kernel(mem_ref, wk_ref, wv_ref, k_ref, v_ref):
    mb = mem_ref[...].astype(BF16)
    k_ref[...] = _dot(mb, wk_ref[...]).astype(BF16)
    v_ref[...] = _dot(mb, wv_ref[...]).astype(BF16)


def _mem_kv(mem2d, w_mk, w_mv, mem_len):
    rows, d = mem2d.shape
    blk = pl.BlockSpec((mem_len, d), lambda i: (i, 0))
    return pl.pallas_call(
        _mem_kv_kernel,
        grid=(rows // mem_len,),
        out_shape=(jax.ShapeDtypeStruct((rows, d), BF16),) * 2,
        in_specs=[blk, _const_spec(w_mk.shape), _const_spec(w_mv.shape)],
        out_specs=(blk, blk),
        compiler_params=pltpu.CompilerParams(dimension_semantics=("arbitrary",)),
        name="mem_kv",
    )(mem2d, w_mk, w_mv)


def _post_mixer_kernel(x_ref, o_ref, gm_ref, sg_ref, mk_ref, mv_ref,
                       w_mla_out_ref, w_o_ref, ln1_g_ref, ln1_b_ref,
                       w_mq_ref, w_mo_ref, ln2_g_ref, ln2_b_ref,
                       wr_hi_ref, wr_lo_ref, br_ref,
                       x2_ref, route_ref, idx_ref, counts_ref, run_ref, *, tm, sub, alpha, mem_scale):
    i = pl.program_id(0)

    @pl.when(i == 0)
    def _():
        run_ref[...] = jnp.zeros_like(run_ref)

    parts = range(tm // sub)
    rows = [slice(part * sub, (part + 1) * sub) for part in parts]
    y_mla = [_dot(o_ref[r, :], w_mla_out_ref[...]) for r in rows]
    merged = [(gm_ref[r, :].astype(F32) + sg_ref[r, :].astype(F32) * y).astype(BF16) for r, y in zip(rows, y_mla)]
    mixed = [_dot(m, w_o_ref[...]) for m in merged]
    x1 = [_layer_norm(alpha * x_ref[r, :] + h, ln1_g_ref[...], ln1_b_ref[...]) for r, h in zip(rows, mixed)]

    qm = [(_dot(x.astype(BF16), w_mq_ref[...]) * mem_scale).astype(BF16) for x in x1]
    hd = mk_ref.shape[1] // MEM_HEADS
    heads = [[] for _ in parts]
    for h in range(MEM_HEADS):
        cols = slice(h * hd, (h + 1) * hd)
        s = [_dot_nt(q[:, cols], mk_ref[:, cols]) for q in qm]
        pr = [jnp.exp(v - jnp.max(v, axis=1, keepdims=True)) for v in s]
        for part in parts:
            heads[part].append(_dot(pr[part].astype(BF16), mv_ref[:, cols])
                               / jnp.sum(pr[part], axis=1, keepdims=True))
    om = [jnp.concatenate(hs, axis=1).astype(BF16) for hs in heads]
    mem_out = [_dot(v, w_mo_ref[...]) for v in om]
    x2 = [_layer_norm(alpha * x + h, ln2_g_ref[...], ln2_b_ref[...]) for x, h in zip(x1, mem_out)]
    for r, v in zip(rows, x2):
        x2_ref[r, :] = v

    x_hi = [v.astype(BF16) for v in x2]
    x_lo = [(v - hi.astype(F32)).astype(BF16) for v, hi in zip(x2, x_hi)]
    logit = [_dot(hi, wr_hi_ref[...]) + _dot(lo, wr_hi_ref[...]) + _dot(hi, wr_lo_ref[...]) + br_ref[...]
             for hi, lo in zip(x_hi, x_lo)]
    lane = lax.broadcasted_iota(jnp.int32, (sub, LANES), 1).astype(F32)
    big = jnp.float32(1e9)
    r_i = lax.broadcasted_iota(jnp.int32, (sub, sub), 0)
    c_i = lax.broadcasted_iota(jnp.int32, (sub, sub), 1)
    tri = jnp.where(c_i < r_i, 1.0, 0.0).astype(BF16)

    def first_argmax(vals, vmax):
        return jnp.min(jnp.where(vals == vmax, lane, big), axis=1, keepdims=True)

    for part in parts:
        logits = logit[part]
        g_mask = (lane >= N_EXPERTS) & (lane < N_EXPERTS + N_GROUPS)
        lg = jnp.where(g_mask, logits, -jnp.inf)
        g_max = jnp.max(lg, axis=1, keepdims=True)
        g_sel = first_argmax(lg, g_max) - N_EXPERTS
        g_w = 1.0 / jnp.sum(jnp.where(g_mask, jnp.exp(logits - g_max), 0.0), axis=1, keepdims=True)
        in_group = jnp.floor(lane * (1.0 / EXPERTS_PER_GROUP)) == g_sel
        le = jnp.where(in_group, logits, -jnp.inf)
        v1 = jnp.max(le, axis=1, keepdims=True)
        e1 = first_argmax(le, v1)
        le2 = jnp.where(lane == e1, -jnp.inf, le)
        v2 = jnp.max(le2, axis=1, keepdims=True)
        e2 = first_argmax(le2, v2)
        t2 = jnp.exp(v2 - v1)
        w1 = (1.0 / (1.0 + t2)) * g_w
        w2 = (t2 / (1.0 + t2)) * g_w

        hit1 = lane == e1
        hit2 = lane == e2
        onehot = jnp.where(hit1 | hit2, 1.0, 0.0)
        before = _dot(tri, onehot.astype(BF16)) + run_ref[...]
        r1 = jnp.sum(jnp.where(hit1, before, 0.0), axis=1, keepdims=True)
        r2 = jnp.sum(jnp.where(hit2, before, 0.0), axis=1, keepdims=True)
        run_ref[...] = run_ref[...] + jnp.sum(onehot, axis=0, keepdims=True)

        packed = jnp.zeros((sub, LANES), F32)
        for pos, val in enumerate((e1, e2, w1, w2, r1, r2)):
            packed = jnp.where(lane == pos, val, packed)
        route_ref[rows[part], :] = packed
        idx_ref[part] = packed.T[:8, :]
    counts_ref[...] = jnp.broadcast_to(run_ref[...], counts_ref.shape)


def _post_mixer(x2d, o, gm, sg, mk, mv, p, seq, mem_len):
    t, d = x2d.shape
    tm, sub = TM_POST, POST_SUB
    assert seq % tm == 0, "a post_mixer row tile must not straddle two sequences (memory blocks are per sequence)"
    per_batch = seq // tm
    row = lambda n: pl.BlockSpec((tm, n), lambda i: (i, 0))
    memblk = pl.BlockSpec((mem_len, d), lambda i: (i // per_batch, 0))
    weights = [p["w_mla_out"], p["w_o"], p["ln1_g"], p["ln1_b"], p["w_mq"], p["w_mo"],
               p["ln2_g"], p["ln2_b"], p["wr_hi"], p["wr_lo"], p["br"]]
    return pl.pallas_call(
        functools.partial(_post_mixer_kernel, tm=tm, sub=sub, alpha=p["alpha"],
                          mem_scale=(d // MEM_HEADS) ** -0.5),
        grid=(t // tm,),
        out_shape=(jax.ShapeDtypeStruct((t, d), F32), jax.ShapeDtypeStruct((t, LANES), F32),
                   jax.ShapeDtypeStruct((t // sub, 8, sub), F32), jax.ShapeDtypeStruct((8, LANES), F32)),
        in_specs=[row(d), row(d), row(d), row(d), memblk, memblk]
                 + [_const_spec(w.shape) for w in weights],
        out_specs=(row(d), row(LANES), pl.BlockSpec((tm // sub, 8, sub), lambda i: (i, 0, 0)),
                   pl.BlockSpec((8, LANES), lambda i: (0, 0))),
        scratch_shapes=[pltpu.VMEM((1, LANES), F32)],
        compiler_params=pltpu.CompilerParams(dimension_semantics=("arbitrary",),
                                             vmem_limit_bytes=VMEM_LIMIT),
        name="post_mixer",
    )(x2d, o, gm, sg, mk, mv, *weights)


def _dispatch_kernel(last_ref, has_ref, nvalid_ref, dst_ref, x_hbm, xd_hbm, zbuf, xbuf, zsem, tsem, lsems, ssems,
                     *, tm, nsteps, rows, nblocks):
    i = pl.program_id(0)

    def load(step, s):
        return pltpu.make_async_copy(x_hbm.at[pl.ds(pl.multiple_of(step * tm, tm), tm)], xbuf.at[s], lsems.at[s])

    def drain(s):
        pltpu.make_async_copy(xd_hbm.at[pl.ds(0, 2 * tm)], xd_hbm.at[pl.ds(0, 2 * tm)], ssems.at[s]).wait()

    def zero_copy(start, sem):
        return pltpu.make_async_copy(zbuf, xd_hbm.at[pl.ds(pl.multiple_of(start, rows), rows)], sem)

    def unused_blocks(action):
        def body(blk, carry):
            action(zero_copy(blk * rows, tsem))
            return carry
        lax.fori_loop(nvalid_ref[0], nblocks, body, 0)

    @pl.when(i == 0)
    def _():
        load(0, 0).start()
        zbuf[...] = jnp.zeros_like(zbuf)
        for e in range(N_EXPERTS):
            @pl.when(has_ref[e] > 0)
            def _():
                zero_copy(last_ref[e], zsem).start()

        unused_blocks(lambda copy: copy.start())
        for e in range(N_EXPERTS):
            @pl.when(has_ref[e] > 0)
            def _():
                zero_copy(last_ref[e], zsem).wait()

    for s in range(3):
        @pl.when(i % 3 == s)
        def _():
            nxt = (s + 1) % 3
            load(i, s).wait()

            @pl.when(i >= 2)
            def _():
                drain(nxt)

            @pl.when(i + 1 < nsteps)
            def _():
                load(i + 1, nxt).start()

            for k in range(2):
                for r in range(tm):
                    pltpu.make_async_copy(xbuf.at[s, pl.ds(r, 1)],
                                          xd_hbm.at[pl.ds(dst_ref[0, 0, k * tm + r], 1)],
                                          ssems.at[s]).start(priority=r % 2)

            @pl.when(i == nsteps - 1)
            def _():
                if nsteps >= 2:
                    drain((s + 2) % 3)
                drain(s)
                unused_blocks(lambda copy: copy.wait())


def _moe_dispatch(x2, dst3, last_start, has_rows, nvalid, total_rows):
    t, d = x2.shape
    tm = TM_OUT
    nsteps = t // tm
    assert dst3.shape == (nsteps, 1, 2 * tm), "post_mixer and dispatch/combine tiles must coincide"
    grid_spec = pltpu.PrefetchScalarGridSpec(
        num_scalar_prefetch=3,
        grid=(nsteps,),
        in_specs=[pl.BlockSpec((1, 1, 2 * tm), lambda i, la, ha, nv: (i, 0, 0), memory_space=pltpu.SMEM),
                  pl.BlockSpec(memory_space=pl.ANY)],
        out_specs=pl.BlockSpec(memory_space=pl.ANY),
        scratch_shapes=[pltpu.VMEM((MOE_ROWS, d), F32), pltpu.VMEM((3, tm, d), F32),
                        pltpu.SemaphoreType.DMA(()), pltpu.SemaphoreType.DMA(()),
                        pltpu.SemaphoreType.DMA((3,)), pltpu.SemaphoreType.DMA((3,))],
    )
    return pl.pallas_call(
        functools.partial(_dispatch_kernel, tm=tm, nsteps=nsteps, rows=MOE_ROWS,
                          nblocks=total_rows // MOE_ROWS),
        grid_spec=grid_spec,
        out_shape=jax.ShapeDtypeStruct((total_rows, d), F32),
        compiler_params=pltpu.CompilerParams(dimension_semantics=("arbitrary",), has_side_effects=True),
        name="moe_dispatch",
    )(last_start, has_rows, nvalid, dst3, x2)


def _experts_kernel(be_ref, nvalid_ref, xd_ref, wg_ref, wu_ref, wd_ref, yd_ref, wgu_bf, wd_bf):
    i = pl.program_id(0)
    changed = jnp.logical_or(i == 0, be_ref[i] != be_ref[jnp.maximum(i - 1, 0)])

    @pl.when(changed)
    def _():
        wgu_bf[:, :D_EXPERT] = wg_ref[...].astype(BF16)
        wgu_bf[:, D_EXPERT:] = wu_ref[...].astype(BF16)
        wd_bf[...] = wd_ref[...].astype(BF16)

    @pl.when(i < nvalid_ref[0])
    def _():
        gu = _dot(xd_ref[...].astype(BF16), wgu_bf[...])
        gate = gu[:, :D_EXPERT]
        hidden = (gate * jax.nn.sigmoid(gate) * gu[:, D_EXPERT:]).astype(BF16)
        yd_ref[...] = _dot(hidden, wd_bf[...])

    @pl.when(i >= nvalid_ref[0])
    def _():
        yd_ref[...] = jnp.zeros_like(yd_ref)


def _moe_experts(xd, block_e, nvalid, w_gate, w_up, w_down):
    total, d = xd.shape
    rows = MOE_ROWS
    nblocks = total // rows
    grid_spec = pltpu.PrefetchScalarGridSpec(
        num_scalar_prefetch=2,
        grid=(nblocks,),
        in_specs=[pl.BlockSpec((rows, d), lambda i, be, nv: (jnp.minimum(i, nv[0] - 1), 0)),
                  pl.BlockSpec((None, d, D_EXPERT), lambda i, be, nv: (be[i], 0, 0)),
                  pl.BlockSpec((None, d, D_EXPERT), lambda i, be, nv: (be[i], 0, 0)),
                  pl.BlockSpec((None, D_EXPERT, d), lambda i, be, nv: (be[i], 0, 0))],
        out_specs=pl.BlockSpec((rows, d), lambda i, be, nv: (i, 0)),
        scratch_shapes=[pltpu.VMEM((d, 2 * D_EXPERT), BF16), pltpu.VMEM((D_EXPERT, d), BF16)],
    )
    return pl.pallas_call(
        _experts_kernel,
        grid_spec=grid_spec,
        out_shape=jax.ShapeDtypeStruct((total, d), F32),
        compiler_params=pltpu.CompilerParams(dimension_semantics=("arbitrary",),
                                             vmem_limit_bytes=VMEM_LIMIT),
        name="moe_experts",
    )(block_e, nvalid, xd, w_gate, w_up, w_down)


def _combine_kernel(dst_cur_ref, dst_nxt_ref, dst_nx2_ref, x2_ref, route_ref, g_ref, b_ref, yd_hbm,
                    out_ref, ybuf, sems, *, tm, nsteps, alpha):
    i = pl.program_id(0)

    def issue(idx_ref, s):
        for r in range(2 * tm):
            pltpu.make_async_copy(yd_hbm.at[pl.ds(idx_ref[0, 0, r], 1)], ybuf.at[s, pl.ds(r, 1)],
                                  sems.at[s]).start(priority=r % 2)

    def wait(s):
        pltpu.make_async_copy(yd_hbm.at[pl.ds(0, 2 * tm)], ybuf.at[s], sems.at[s]).wait()

    @pl.when(i == 0)
    def _():
        issue(dst_cur_ref, 0)
        issue(dst_nxt_ref, 1)

    for s in range(3):
        @pl.when(i % 3 == s)
        def _():
            wait(s)
            issue(dst_nx2_ref, (s + 2) % 3)
            route = route_ref[...]
            y = ybuf[s, :tm, :] * route[:, 2:3] + ybuf[s, tm:, :] * route[:, 3:4]
            out_ref[...] = _layer_norm(alpha * x2_ref[...] + y, g_ref[...], b_ref[...])

            @pl.when(i == nsteps - 1)
            def _():
                wait((s + 1) % 3)
                wait((s + 2) % 3)


def _moe_combine(x2, route, dst3, yd, ln_g, ln_b, alpha):
    t, d = x2.shape
    tm = TM_OUT
    nsteps = t // tm
    smem_blk = lambda f: pl.BlockSpec((1, 1, 2 * tm), f, memory_space=pltpu.SMEM)
    row = lambda n: pl.BlockSpec((tm, n), lambda i: (i, 0))
    return pl.pallas_call(
        functools.partial(_combine_kernel, tm=tm, nsteps=nsteps, alpha=alpha),
        grid=(nsteps,),
        out_shape=jax.ShapeDtypeStruct((t, d), F32),
        in_specs=[smem_blk(lambda i: (i, 0, 0)),
                  smem_blk(lambda i: (jnp.minimum(i + 1, nsteps - 1), 0, 0)),
                  smem_blk(lambda i: (jnp.minimum(i + 2, nsteps - 1), 0, 0)),
                  row(d), row(LANES), _const_spec(ln_g.shape), _const_spec(ln_b.shape),
                  pl.BlockSpec(memory_space=pl.ANY)],
        out_specs=row(d),
        scratch_shapes=[pltpu.VMEM((3, 2 * tm, d), F32), pltpu.SemaphoreType.DMA((3,))],
        compiler_params=pltpu.CompilerParams(dimension_semantics=("arbitrary",),
                                             vmem_limit_bytes=VMEM_LIMIT),
        name="moe_combine",
    )(dst3, dst3, dst3, x2, route, ln_g, ln_b, yd)


def _prep_layer(w_in, b_in, gm_ln_g, gm_ln_b, gm_w_s, gm_b_s, w_gm_out, mla_q_norm_g, mla_kv_norm_g,
                w_uq, w_uk, w_uv, w_mla_out, w_o, ln1_g, ln1_b, w_mq, w_mk, w_mv, w_mo, ln2_g, ln2_b,
                w_group_router, b_group_router, w_expert_router, b_expert_router, depth):
    d = w_in.shape[0]
    s_v = 2 * GM_WIDTH
    s_q = s_v + MLA_Q_LORA
    s_kv = s_q + MLA_KV_LORA
    s_r = s_kv + MLA_ROPE
    rowv = lambda a: a.reshape(1, -1).astype(F32)

    def lat_cols(a):
        z = jnp.zeros(a.shape[:-1] + (MLA_NOPE,), a.dtype)
        kr = a[..., s_kv:s_r]
        return jnp.concatenate([a[..., s_v:s_kv], z, kr, kr], axis=-1)

    wq3 = w_uq.reshape(MLA_Q_LORA, MLA_HEADS, MLA_NOPE + MLA_ROPE)
    wq_pad = jnp.concatenate([wq3, wq3[..., MLA_NOPE:]], axis=-1).reshape(MLA_Q_LORA, MLA_HEADS * HEAD_PAD)
    wk3 = w_uk.reshape(MLA_KV_LORA, MLA_HEADS, MLA_NOPE)
    wk_pad = jnp.pad(wk3, ((0, 0), (0, 0), (0, HEAD_PAD - MLA_NOPE))).reshape(MLA_KV_LORA, MLA_HEADS * HEAD_PAD)
    wv3 = w_uv.T.reshape(MLA_HEADS, MLA_V, MLA_KV_LORA)
    wv_pad = jnp.pad(wv3, ((0, 0), (0, V_ROWS - MLA_V), (0, 0))).reshape(MLA_HEADS * V_ROWS, MLA_KV_LORA)
    b_v = jnp.zeros((MLA_HEADS, V_ROWS, 1), F32).at[:, MLA_V].set(1.0).reshape(MLA_HEADS * V_ROWS, 1)
    w_r = jnp.zeros((d, LANES), F32)
    w_r = w_r.at[:, :N_EXPERTS].set(w_expert_router).at[:, N_EXPERTS:N_EXPERTS + N_GROUPS].set(w_group_router)
    b_r = jnp.zeros((LANES,), F32)
    b_r = b_r.at[:N_EXPERTS].set(b_expert_router).at[N_EXPERTS:N_EXPERTS + N_GROUPS].set(b_group_router)
    wr_hi = w_r.astype(BF16)
    return dict(
        w_uv=w_in[:, :s_v].astype(BF16), b_uv=rowv(b_in[:s_v]),
        w_gate=w_in[:, s_r:].astype(BF16), b_gate=rowv(b_in[s_r:]),
        w_lat=lat_cols(w_in).astype(BF16), b_lat=rowv(lat_cols(b_in)),
        gm_ln_g=rowv(gm_ln_g), gm_ln_b=rowv(gm_ln_b),
        ws=gm_w_s.astype(BF16), bs_full=jnp.repeat(gm_b_s.T, GM_CHUNK, axis=1).astype(F32),
        w_gm_out=w_gm_out.astype(BF16), qn_g=rowv(mla_q_norm_g), kvn_g=rowv(mla_kv_norm_g),
        w_uq=wq_pad.astype(BF16), w_uk=wk_pad.astype(BF16), w_uvv=wv_pad.astype(BF16), b_v=b_v,
        w_mla_out=w_mla_out.astype(BF16), w_o=w_o.astype(BF16), ln1_g=rowv(ln1_g), ln1_b=rowv(ln1_b),
        w_mq=w_mq.astype(BF16), w_mk=w_mk.astype(BF16), w_mv=w_mv.astype(BF16), w_mo=w_mo.astype(BF16),
        ln2_g=rowv(ln2_g), ln2_b=rowv(ln2_b),
        wr_hi=wr_hi, wr_lo=(w_r - wr_hi.astype(F32)).astype(BF16), br=rowv(b_r),
        alpha=(2 * depth) ** 0.25,
    )


def _dispatch_tables(idx, counts, t):
    rows = MOE_ROWS
    tiles, _, tm = idx.shape
    e = idx[:, 0:2, :].astype(jnp.int32)
    rank = idx[:, 4:6, :].astype(jnp.int32)
    cnt = counts[0, :N_EXPERTS].astype(jnp.int32)
    padded = (cnt + rows - 1) // rows * rows
    ends = jnp.cumsum(padded)
    starts = ends - padded
    ids = jnp.arange(N_EXPERTS, dtype=jnp.int32)[:, None, None, None]
    start_of = jnp.sum(jnp.where(e[None] == ids, starts[:, None, None, None], 0), axis=0)
    group = TM_OUT // tm
    dst3 = (start_of + rank).reshape(tiles // group, group, 2, tm).transpose(0, 2, 1, 3)
    dst3 = dst3.reshape(tiles // group, 1, 2 * TM_OUT)
    total = 2 * t + N_EXPERTS * rows
    block_start = jnp.arange(total // rows, dtype=jnp.int32) * rows
    block_e = jnp.minimum(jnp.sum(ends[None, :] <= block_start[:, None], axis=1), N_EXPERTS - 1)
    nvalid = (ends[-1] // rows).reshape(1).astype(jnp.int32)
    last_start = jnp.maximum(ends - rows, 0).astype(jnp.int32)
    has_rows = (cnt > 0).astype(jnp.int32)
    return dst3, block_e.astype(jnp.int32), nvalid, last_start, has_rows, total


def _layer(x, mem, positions, depth, w_in, b_in, gm_ln_g, gm_ln_b, gm_w_s, gm_b_s, w_gm_out,
           mla_q_norm_g, mla_kv_norm_g, w_uq, w_uk, w_uv, w_mla_out, w_o, ln1_g, ln1_b,
           w_mq, w_mk, w_mv, w_mo, ln2_g, ln2_b,
           w_group_router, b_group_router, w_expert_router, b_expert_router,
           w_exp_gate, w_exp_up, w_exp_down, ln3_g, ln3_b):
    b, s, d = x.shape
    t = b * s
    mem_len = mem.shape[1]
    p = _prep_layer(w_in, b_in, gm_ln_g, gm_ln_b, gm_w_s, gm_b_s, w_gm_out, mla_q_norm_g, mla_kv_norm_g,
                    w_uq, w_uk, w_uv, w_mla_out, w_o, ln1_g, ln1_b, w_mq, w_mk, w_mv, w_mo, ln2_g, ln2_b,
                    w_group_router, b_group_router, w_expert_router, b_expert_router, depth)
    x2d = x.reshape(t, d)
    inv_freq = ROPE_THETA ** (-jnp.arange(ROPE_HALF, dtype=F32) / ROPE_HALF)
    cos, sin = _rope_tables(positions, inv_freq)
    gm, sg, q, k, v = _mixer_in(x2d, cos, sin, p)
    o = _mla_attn(q, k, v, b, s)
    mk, mv = _mem_kv(mem.reshape(b * mem_len, d), p["w_mk"], p["w_mv"], mem_len)
    x2, route, idx, counts = _post_mixer(x2d, o, gm, sg, mk, mv, p, s, mem_len)
    dst3, block_e, nvalid, last_start, has_rows, total = _dispatch_tables(idx, counts, t)
    xd = _moe_dispatch(x2, dst3, last_start, has_rows, nvalid, total)
    yd = _moe_experts(xd, block_e, nvalid, w_exp_gate, w_exp_up, w_exp_down)
    out = _moe_combine(x2, route, dst3, yd, ln3_g.reshape(1, d), ln3_b.reshape(1, d), p["alpha"])
    return out.reshape(b, s, d)


def kernel(x, mem, positions, w_in, b_in, gm_ln_g, gm_ln_b, gm_w_s, gm_b_s, w_gm_out, mla_q_norm_g, mla_kv_norm_g, w_uq, w_uk, w_uv, w_mla_out, w_o, ln1_g, ln1_b, w_mq, w_mk, w_mv, w_mo, ln2_g, ln2_b, w_group_router, b_group_router, w_expert_router, b_expert_router, w_exp_gate, w_exp_up, w_exp_down, ln3_g, ln3_b):
    depth = w_in.shape[0]
    per_layer = (w_in, b_in, gm_ln_g, gm_ln_b, gm_w_s, gm_b_s, w_gm_out, mla_q_norm_g, mla_kv_norm_g,
                 w_uq, w_uk, w_uv, w_mla_out, w_o, ln1_g, ln1_b, w_mq, w_mk, w_mv, w_mo, ln2_g, ln2_b,
                 w_group_router, b_group_router, w_expert_router, b_expert_router,
                 w_exp_gate, w_exp_up, w_exp_down, ln3_g, ln3_b)
    h = x
    for l in range(depth):
        h = _layer(h, mem, positions, depth, *[w[l] for w in per_layer])
    return h
```

```python
import functools
import math

import jax
import jax.numpy as jnp
from jax import lax
from jax.experimental import pallas as pl
from jax.experimental.pallas import tpu as pltpu

F32 = jnp.float32
BF16 = jnp.bfloat16

GM_WIDTH = 1024
GM_GROUPS = 8
GM_CHUNK = 128
MLA_HEADS = 16
MLA_NOPE = 64
MLA_ROPE = 32
MLA_V = 64
MLA_Q_LORA = 384
MLA_KV_LORA = 256
ROPE_THETA = 10000.0
MEM_HEADS = 4
N_GROUPS = 8
EXPERTS_PER_GROUP = 8
N_EXPERTS = N_GROUPS * EXPERTS_PER_GROUP
D_EXPERT = 256
LN_EPS = 1e-5
RMS_EPS = 1e-6

LANES = 128
HEAD_PAD = 128
V_ROWS = 80
ROPE_HALF = MLA_ROPE // 2

ROPE_BLOCK = 4096
TM_IN = 512
TM_SUB = 256
TQ = 4096
TK = 256
ATTN_HEADS = 2
SM_CHUNK = 16
TM_POST = 1024
POST_SUB = 256
MOE_ROWS = 512
TM_OUT = 512
VMEM_LIMIT = 56 * 1024 * 1024


def _const_spec(shape):
    nd = len(shape)
    return pl.BlockSpec(shape, lambda *_: (0,) * nd, pipeline_mode=pl.Buffered(1))


def _layer_norm(x, g, b):
    mu = jnp.mean(x, axis=-1, keepdims=True)
    xc = x - mu
    var = jnp.mean(xc * xc, axis=-1, keepdims=True)
    return xc * lax.rsqrt(var + LN_EPS) * g + b


def _rms_norm(x, g):
    return x * lax.rsqrt(jnp.mean(x * x, axis=-1, keepdims=True) + RMS_EPS) * g


def _gelu(x):
    return x * (lax.erf(x * (1.0 / math.sqrt(2.0))) + 1.0) * 0.5


def _dot(a, b):
    return jnp.dot(a, b, preferred_element_type=F32)


def _dot_nt(a, b):
    return lax.dot_general(a, b, (((1,), (1,)), ((), ())), preferred_element_type=F32)


def _rope_kernel(inv_ref, pos_ref, spread_ref, cos_ref, sin_ref):
    ang = inv_ref[...] * pos_ref[...].astype(F32)
    def to_rows(v):
        out = None
        for _ in range(3):
            piece = v.astype(BF16)
            v = v - piece.astype(F32)
            term = lax.dot_general(piece, spread_ref[...], (((0,), (0,)), ((), ())),
                                   preferred_element_type=F32)
            out = term if out is None else out + term
        return out

    cos_ref[...] = to_rows(jnp.cos(ang))
    sin_ref[...] = to_rows(jnp.sin(ang))


def _rope_tables(positions, inv_freq):
    t = positions.size
    blk = min(t, ROPE_BLOCK)
    spread = (jnp.arange(LANES)[None, :] % ROPE_HALF == jnp.arange(ROPE_HALF)[:, None]).astype(BF16)
    out = pl.BlockSpec((blk, LANES), lambda i: (i, 0))
    return pl.pallas_call(
        _rope_kernel,
        grid=(t // blk,),
        out_shape=(jax.ShapeDtypeStruct((t, LANES), F32),) * 2,
        in_specs=[pl.BlockSpec((ROPE_HALF, 1), lambda i: (0, 0)),
                  pl.BlockSpec((1, blk), lambda i: (0, i)),
                  pl.BlockSpec((ROPE_HALF, LANES), lambda i: (0, 0))],
        out_specs=(out, out),
        compiler_params=pltpu.CompilerParams(dimension_semantics=("arbitrary",)),
        name="rope_tables",
    )(inv_freq.reshape(ROPE_HALF, 1), positions.reshape(1, t), spread)


def _mixer_in_kernel(x_ref, cos_ref, sin_ref, w_uv_ref, b_uv_ref, w_gate_ref, b_gate_ref,
                     w_lat_ref, b_lat_ref, ln_g_ref, ln_b_ref, ws_ref, bs_ref, w_gm_out_ref,
                     qn_g_ref, kvn_g_ref, w_uq_ref, w_uk_ref, w_uvv_ref, b_v_ref,
                     gm_ref, sg_ref, q_ref, k_ref, v_ref, *, tm, sub, scale):
    parts = range(tm // sub)
    rows = [slice(part * sub, (part + 1) * sub) for part in parts]
    xb = [x_ref[r, :].astype(BF16) for r in rows]

    uv = [_gelu(_dot(x, w_uv_ref[...]) + b_uv_ref[...]) for x in xb]
    vn = [_layer_norm(t[:, GM_WIDTH:], ln_g_ref[...], ln_b_ref[...]).astype(BF16) for t in uv]
    r_i = lax.broadcasted_iota(jnp.int32, (GM_CHUNK, GM_CHUNK), 0)
    c_i = lax.broadcasted_iota(jnp.int32, (GM_CHUNK, GM_CHUNK), 1)
    causal = c_i <= r_i
    nchunk = sub // GM_CHUNK
    blocks = [[[None] * GM_GROUPS for _ in range(nchunk)] for _ in parts]
    for g in range(GM_GROUPS):
        w = jnp.where(causal, ws_ref[g], jnp.zeros_like(ws_ref[g]))
        cols = slice(g * GM_CHUNK, (g + 1) * GM_CHUNK)
        for part in parts:
            for ch in range(nchunk):
                local = slice(ch * GM_CHUNK, (ch + 1) * GM_CHUNK)
                mixed = _dot(w, vn[part][local, cols]) + bs_ref[:, cols]
                blocks[part][ch][g] = (uv[part][local, cols] * mixed).astype(BF16)
    gated = [jnp.concatenate([jnp.concatenate(chunk, axis=1) for chunk in blocks[part]], axis=0)
             for part in parts]
    y_gm = [_dot(t, w_gm_out_ref[...]) for t in gated]
    gates = [_dot(x, w_gate_ref[...]) + b_gate_ref[...] for x in xb]
    for r, gt, y in zip(rows, gates, y_gm):
        gm_ref[r, :] = (jax.nn.sigmoid(gt[:, :GM_WIDTH]) * y).astype(BF16)
        sg_ref[r, :] = jax.nn.sigmoid(gt[:, GM_WIDTH:]).astype(BF16)

    lat = [_dot(x, w_lat_ref[...]) + b_lat_ref[...] for x in xb]
    cq = [_rms_norm(t[:, :MLA_Q_LORA], qn_g_ref[...]).astype(BF16) for t in lat]
    ckv = [_rms_norm(t[:, MLA_Q_LORA:MLA_Q_LORA + MLA_KV_LORA], kvn_g_ref[...]).astype(BF16) for t in lat]
    qa = [_dot(t, w_uq_ref[...]) for t in cq]
    kn = [_dot(t, w_uk_ref[...]) for t in ckv]

    lane = lax.broadcasted_iota(jnp.int32, (sub, HEAD_PAD), 1)
    in_rope = (lane >= MLA_NOPE) & (lane < MLA_NOPE + MLA_ROPE)
    for part in parts:
        r = rows[part]
        cosv = cos_ref[r, :]
        sinv = sin_ref[r, :]
        cfac = jnp.where(lane < MLA_NOPE, 1.0, jnp.where(in_rope, cosv, 0.0))
        sfac = jnp.where(in_rope, jnp.where(lane < MLA_NOPE + ROPE_HALF, -sinv, sinv), 0.0)

        def rot(t, cfac=cfac, sfac=sfac):
            return t * cfac + pltpu.roll(t, HEAD_PAD - ROPE_HALF, 1) * sfac

        kr_rot = rot(lat[part][:, MLA_Q_LORA + MLA_KV_LORA:])
        for h in range(MLA_HEADS):
            cols = slice(h * HEAD_PAD, (h + 1) * HEAD_PAD)
            q_ref[r, cols] = (rot(qa[part][:, cols]) * scale).astype(BF16)
            k_ref[r, cols] = (kn[part][:, cols] + kr_rot).astype(BF16)
        v_ref[:, r] = (_dot_nt(w_uvv_ref[...], ckv[part]) + b_v_ref[...]).astype(BF16)


def _mixer_in(x2d, cos, sin, p):
    t, d = x2d.shape
    tm = TM_IN
    scale = (MLA_NOPE + MLA_ROPE) ** -0.5 * math.log2(math.e)
    row = lambda n: pl.BlockSpec((tm, n), lambda i: (i, 0))
    weights = [p["w_uv"], p["b_uv"], p["w_gate"], p["b_gate"], p["w_lat"], p["b_lat"],
               p["gm_ln_g"], p["gm_ln_b"], p["ws"], p["bs_full"], p["w_gm_out"],
               p["qn_g"], p["kvn_g"], p["w_uq"], p["w_uk"], p["w_uvv"], p["b_v"]]
    qk_w = MLA_HEADS * HEAD_PAD
    v_w = MLA_HEADS * V_ROWS
    return pl.pallas_call(
        functools.partial(_mixer_in_kernel, tm=tm, sub=TM_SUB, scale=scale),
        grid=(t // tm,),
        out_shape=(jax.ShapeDtypeStruct((t, d), BF16), jax.ShapeDtypeStruct((t, d), BF16),
                   jax.ShapeDtypeStruct((t, qk_w), BF16), jax.ShapeDtypeStruct((t, qk_w), BF16),
                   jax.ShapeDtypeStruct((v_w, t), BF16)),
        in_specs=[row(d), row(LANES), row(LANES)] + [_const_spec(w.shape) for w in weights],
        out_specs=(row(d), row(d), row(qk_w), row(qk_w), pl.BlockSpec((v_w, tm), lambda i: (0, i))),
        compiler_params=pltpu.CompilerParams(dimension_semantics=("arbitrary",),
                                             vmem_limit_bytes=VMEM_LIMIT),
        name="mixer_in",
    )(x2d, cos, sin, *weights)


def _attn_kernel(q_ref, k_ref, vt_ref, o_ref, acc_ref, m_ref, r0_ref, r1_ref, s0_ref, s1_ref, p0_ref, p1_ref,
                 *, tq, tk):
    i = pl.program_id(2)
    acc_ref[...] = jnp.zeros_like(acc_ref)
    m_ref[...] = jnp.full(m_ref.shape, -jnp.inf, F32)

    every = slice(0, tq)

    def scores(j, s_ref, qs=every):
        kb = k_ref[pl.ds(pl.multiple_of(j * tk, tk), tk), :]
        for a in range(ATTN_HEADS):
            cols = slice(a * HEAD_PAD, (a + 1) * HEAD_PAD)
            s_ref[a, :, qs] = _dot_nt(kb[:, cols], q_ref[qs, cols])

    def softmax(s_ref, p_ref, r_ref, qs=every, diagonal=False):
        nch = tk // SM_CHUNK
        width = qs.stop - qs.start
        if diagonal:
            q_loc = lax.broadcasted_iota(jnp.int32, (SM_CHUNK, width), 1)
            k_loc = lax.broadcasted_iota(jnp.int32, (SM_CHUNK, width), 0)

        def chunk(a, c):
            s = s_ref[a, c * SM_CHUNK:(c + 1) * SM_CHUNK, qs]
            if diagonal:
                s = jnp.where(k_loc + c * SM_CHUNK <= q_loc, s, -jnp.inf)
            return s

        for a in range(ATTN_HEADS):
            m = m_ref[a:a + 1, qs]
            cm = chunk(a, 0)
            for c in range(1, nch):
                cm = jnp.maximum(cm, chunk(a, c))
            m_new = jnp.maximum(m, jnp.max(cm, axis=0, keepdims=True))
            r_ref[a:a + 1, qs] = jnp.exp2(m - m_new)
            m_ref[a:a + 1, qs] = m_new
            mb = jnp.broadcast_to(m_new, (SM_CHUNK, width))
            for c in range(nch):
                p_ref[a, c * SM_CHUNK:(c + 1) * SM_CHUNK, qs] = jnp.exp2(chunk(a, c) - mb).astype(BF16)

    def accumulate(j, p_ref, r_ref, qs=every):
        vb = vt_ref[:, pl.ds(pl.multiple_of(j * tk, tk), tk)]
        for a in range(ATTN_HEADS):
            rows = slice(a * V_ROWS, (a + 1) * V_ROWS)
            alpha = r_ref[a:a + 1, qs]
            acc_ref[rows, qs] = acc_ref[rows, qs] * alpha + _dot(vb[rows, :], p_ref[a, :, qs])

    def pair(base):
        scores(base + 1, s1_ref)
        softmax(s0_ref, p0_ref, r0_ref)
        accumulate(base, p0_ref, r0_ref)
        scores(base + 2, s0_ref)
        softmax(s1_ref, p1_ref, r1_ref)
        accumulate(base + 1, p1_ref, r1_ref)

    def main(t, carry):
        pair(4 * t)
        pair(4 * t + 2)
        return carry

    nsub = tq // tk
    first = nsub * i
    scores(0, s0_ref)
    lax.fori_loop(0, first // 4, main, 0)
    if nsub % 4:
        @pl.when(first % 4 == 2)
        def _():
            pair(first - 2)

    bufs = ((s0_ref, p0_ref, r0_ref), (s1_ref, p1_ref, r1_ref))
    for j in range(nsub):
        s_ref, p_ref, r_ref = bufs[j % 2]
        if j + 1 < nsub:
            scores(first + j + 1, bufs[(j + 1) % 2][0], slice((j + 1) * tk, tq))
        softmax(s_ref, p_ref, r_ref, slice(j * tk, (j + 1) * tk), diagonal=True)
        if j + 1 < nsub:
            softmax(s_ref, p_ref, r_ref, slice((j + 1) * tk, tq))
        accumulate(first + j, p_ref, r_ref, slice(j * tk, tq))
    out = [acc_ref[a * V_ROWS:a * V_ROWS + MLA_V, :] / acc_ref[a * V_ROWS + MLA_V:a * V_ROWS + MLA_V + 1, :]
           for a in range(ATTN_HEADS)]
    o_ref[...] = jnp.concatenate(out, axis=0).T.astype(BF16)


def _mla_attn(q, k, vt, batch, seq):
    assert TQ % (2 * TK) == 0, "visible key blocks are walked in pairs"
    nq = seq // TQ
    nh = ATTN_HEADS
    return pl.pallas_call(
        functools.partial(_attn_kernel, tq=TQ, tk=TK),
        grid=(batch, MLA_HEADS // nh, nq),
        out_shape=jax.ShapeDtypeStruct((batch * seq, MLA_HEADS * MLA_V), BF16),
        in_specs=[pl.BlockSpec((TQ, nh * HEAD_PAD), lambda b, h, i: (b * nq + i, h)),
                  pl.BlockSpec((seq, nh * HEAD_PAD), lambda b, h, i: (b, h)),
                  pl.BlockSpec((nh * V_ROWS, seq), lambda b, h, i: (h, b))],
        out_specs=pl.BlockSpec((TQ, nh * MLA_V), lambda b, h, i: (b * nq + i, h)),
        scratch_shapes=[pltpu.VMEM((nh * V_ROWS, TQ), F32)] + [pltpu.VMEM((8, TQ), F32)] * 3 + [
                        pltpu.VMEM((nh, TK, TQ), F32), pltpu.VMEM((nh, TK, TQ), F32),
                        pltpu.VMEM((nh, TK, TQ), BF16), pltpu.VMEM((nh, TK, TQ), BF16)],
        compiler_params=pltpu.CompilerParams(
            dimension_semantics=("arbitrary", "arbitrary", "arbitrary"), vmem_limit_bytes=VMEM_LIMIT),
        name="mla_attn",
    )(q, k, vt)


def _mem_kv_kernel(mem_ref, wk_ref, wv_ref, k_ref, v_ref):
    mb = mem_ref[...].astype(BF16)
    k_ref[...] = _dot(mb, wk_ref[...]).astype(BF16)
    v_ref[...] = _dot(mb, wv_ref[...]).astype(BF16)


def _mem_kv(mem2d, w_mk, w_mv, mem_len):
    rows, d = mem2d.shape
    blk = pl.BlockSpec((mem_len, d), lambda i: (i, 0))
    return pl.pallas_call(
        _mem_kv_kernel,
        grid=(rows // mem_len,),
        out_shape=(jax.ShapeDtypeStruct((rows, d), BF16),) * 2,
        in_specs=[blk, _const_spec(w_mk.shape), _const_spec(w_mv.shape)],
        out_specs=(blk, blk),
        compiler_params=pltpu.CompilerParams(dimension_semantics=("arbitrary",)),
        name="mem_kv",
    )(mem2d, w_mk, w_mv)


def _post_mixer_kernel(x_ref, o_ref, gm_ref, sg_ref, mk_ref, mv_ref,
                       w_mla_out_ref, w_o_ref, ln1_g_ref, ln1_b_ref,
                       w_mq_ref, w_mo_ref, ln2_g_ref, ln2_b_ref,
                       wr_hi_ref, wr_lo_ref, br_ref,
                       x2_ref, route_ref, idx_ref, counts_ref, run_ref, *, tm, sub, alpha, mem_scale):
    i = pl.program_id(0)

    @pl.when(i == 0)
    def _():
        run_ref[...] = jnp.zeros_like(run_ref)

    parts = range(tm // sub)
    rows = [slice(part * sub, (part + 1) * sub) for part in parts]
    y_mla = [_dot(o_ref[r, :], w_mla_out_ref[...]) for r in rows]
    merged = [(gm_ref[r, :].astype(F32) + sg_ref[r, :].astype(F32) * y).astype(BF16) for r, y in zip(rows, y_mla)]
    mixed = [_dot(m, w_o_ref[...]) for m in merged]
    x1 = [_layer_norm(alpha * x_ref[r, :] + h, ln1_g_ref[...], ln1_b_ref[...]) for r, h in zip(rows, mixed)]

    qm = [(_dot(x.astype(BF16), w_mq_ref[...]) * mem_scale).astype(BF16) for x in x1]
    hd = mk_ref.shape[1] // MEM_HEADS
    heads = [[] for _ in parts]
    for h in range(MEM_HEADS):
        cols = slice(h * hd, (h + 1) * hd)
        s = [_dot_nt(q[:, cols], mk_ref[:, cols]) for q in qm]
        pr = [jnp.exp(v - jnp.max(v, axis=1, keepdims=True)) for v in s]
        for part in parts:
            heads[part].append(_dot(pr[part].astype(BF16), mv_ref[:, cols])
                               / jnp.sum(pr[part], axis=1, keepdims=True))
    om = [jnp.concatenate(hs, axis=1).astype(BF16) for hs in heads]
    mem_out = [_dot(v, w_mo_ref[...]) for v in om]
    x2 = [_layer_norm(alpha * x + h, ln2_g_ref[...], ln2_b_ref[...]) for x, h in zip(x1, mem_out)]
    for r, v in zip(rows, x2):
        x2_ref[r, :] = v

    x_hi = [v.astype(BF16) for v in x2]
    x_lo = [(v - hi.astype(F32)).astype(BF16) for v, hi in zip(x2, x_hi)]
    logit = [_dot(hi, wr_hi_ref[...]) + _dot(lo, wr_hi_ref[...]) + _dot(hi, wr_lo_ref[...]) + br_ref[...]
             for hi, lo in zip(x_hi, x_lo)]
    lane = lax.broadcasted_iota(jnp.int32, (sub, LANES), 1).astype(F32)
    big = jnp.float32(1e9)
    r_i = lax.broadcasted_iota(jnp.int32, (sub, sub), 0)
    c_i = lax.broadcasted_iota(jnp.int32, (sub, sub), 1)
    tri = jnp.where(c_i < r_i, 1.0, 0.0).astype(BF16)

    def first_argmax(vals, vmax):
        return jnp.min(jnp.where(vals == vmax, lane, big), axis=1, keepdims=True)

    for part in parts:
        logits = logit[part]
        g_mask = (lane >= N_EXPERTS) & (lane < N_EXPERTS + N_GROUPS)
        lg = jnp.where(g_mask, logits, -jnp.inf)
        g_max = jnp.max(lg, axis=1, keepdims=True)
        g_sel = first_argmax(lg, g_max) - N_EXPERTS
        g_w = 1.0 / jnp.sum(jnp.where(g_mask, jnp.exp(logits - g_max), 0.0), axis=1, keepdims=True)
        in_group = jnp.floor(lane * (1.0 / EXPERTS_PER_GROUP)) == g_sel
        le = jnp.where(in_group, logits, -jnp.inf)
        v1 = jnp.max(le, axis=1, keepdims=True)
        e1 = first_argmax(le, v1)
        le2 = jnp.where(lane == e1, -jnp.inf, le)
        v2 = jnp.max(le2, axis=1, keepdims=True)
        e2 = first_argmax(le2, v2)
        t2 = jnp.exp(v2 - v1)
        w1 = (1.0 / (1.0 + t2)) * g_w
        w2 = (t2 / (1.0 + t2)) * g_w

        hit1 = lane == e1
        hit2 = lane == e2
        onehot = jnp.where(hit1 | hit2, 1.0, 0.0)
        before = _dot(tri, onehot.astype(BF16)) + run_ref[...]
        r1 = jnp.sum(jnp.where(hit1, before, 0.0), axis=1, keepdims=True)
        r2 = jnp.sum(jnp.where(hit2, before, 0.0), axis=1, keepdims=True)
        run_ref[...] = run_ref[...] + jnp.sum(onehot, axis=0, keepdims=True)

        packed = jnp.zeros((sub, LANES), F32)
        for pos, val in enumerate((e1, e2, w1, w2, r1, r2)):
            packed = jnp.where(lane == pos, val, packed)
        route_ref[rows[part], :] = packed
        idx_ref[part] = packed.T[:8, :]
    counts_ref[...] = jnp.broadcast_to(run_ref[...], counts_ref.shape)


def _post_mixer(x2d, o, gm, sg, mk, mv, p, seq, mem_len):
    t, d = x2d.shape
    tm, sub = TM_POST, POST_SUB
    assert seq % tm == 0, "a post_mixer row tile must not straddle two sequences (memory blocks are per sequence)"
    per_batch = seq // tm
    row = lambda n: pl.BlockSpec((tm, n), lambda i: (i, 0))
    memblk = pl.BlockSpec((mem_len, d), lambda i: (i // per_batch, 0))
    weights = [p["w_mla_out"], p["w_o"], p["ln1_g"], p["ln1_b"], p["w_mq"], p["w_mo"],
               p["ln2_g"], p["ln2_b"], p["wr_hi"], p["wr_lo"], p["br"]]
    return pl.pallas_call(
        functools.partial(_post_mixer_kernel, tm=tm, sub=sub, alpha=p["alpha"],
                          mem_scale=(d // MEM_HEADS) ** -0.5),
        grid=(t // tm,),
        out_shape=(jax.ShapeDtypeStruct((t, d), F32), jax.ShapeDtypeStruct((t, LANES), F32),
                   jax.ShapeDtypeStruct((t // sub, 8, sub), F32), jax.ShapeDtypeStruct((8, LANES), F32)),
        in_specs=[row(d), row(d), row(d), row(d), memblk, memblk]
                 + [_const_spec(w.shape) for w in weights],
        out_specs=(row(d), row(LANES), pl.BlockSpec((tm // sub, 8, sub), lambda i: (i, 0, 0)),
                   pl.BlockSpec((8, LANES), lambda i: (0, 0))),
        scratch_shapes=[pltpu.VMEM((1, LANES), F32)],
        compiler_params=pltpu.CompilerParams(dimension_semantics=("arbitrary",),
                                             vmem_limit_bytes=VMEM_LIMIT),
        name="post_mixer",
    )(x2d, o, gm, sg, mk, mv, *weights)


def _dispatch_kernel(last_ref, has_ref, nvalid_ref, dst_ref, x_hbm, xd_hbm, zbuf, xbuf, zsem, tsem, lsems, ssems,
                     *, tm, nsteps, rows, nblocks):
    i = pl.program_id(0)

    def load(step, s):
        return pltpu.make_async_copy(x_hbm.at[pl.ds(pl.multiple_of(step * tm, tm), tm)], xbuf.at[s], lsems.at[s])

    def drain(s):
        pltpu.make_async_copy(xd_hbm.at[pl.ds(0, 2 * tm)], xd_hbm.at[pl.ds(0, 2 * tm)], ssems.at[s]).wait()

    def zero_copy(start, sem):
        return pltpu.make_async_copy(zbuf, xd_hbm.at[pl.ds(pl.multiple_of(start, rows), rows)], sem)

    def unused_blocks(action):
        def body(blk, carry):
            action(zero_copy(blk * rows, tsem))
            return carry
        lax.fori_loop(nvalid_ref[0], nblocks, body, 0)

    @pl.when(i == 0)
    def _():
        load(0, 0).start()
        zbuf[...] = jnp.zeros_like(zbuf)
        for e in range(N_EXPERTS):
            @pl.when(has_ref[e] > 0)
            def _():
                zero_copy(last_ref[e], zsem).start()

        unused_blocks(lambda copy: copy.start())
        for e in range(N_EXPERTS):
            @pl.when(has_ref[e] > 0)
            def _():
                zero_copy(last_ref[e], zsem).wait()

    for s in range(3):
        @pl.when(i % 3 == s)
        def _():
            nxt = (s + 1) % 3
            load(i, s).wait()

            @pl.when(i >= 2)
            def _():
                drain(nxt)

            @pl.when(i + 1 < nsteps)
            def _():
                load(i + 1, nxt).start()

            for k in range(2):
                for r in range(tm):
                    pltpu.make_async_copy(xbuf.at[s, pl.ds(r, 1)],
                                          xd_hbm.at[pl.ds(dst_ref[0, 0, k * tm + r], 1)],
                                          ssems.at[s]).start(priority=r % 2)

            @pl.when(i == nsteps - 1)
            def _():
                if nsteps >= 2:
                    drain((s + 2) % 3)
                drain(s)
                unused_blocks(lambda copy: copy.wait())


def _moe_dispatch(x2, dst3, last_start, has_rows, nvalid, total_rows):
    t, d = x2.shape
    tm = TM_OUT
    nsteps = t // tm
    assert dst3.shape == (nsteps, 1, 2 * tm), "post_mixer and dispatch/combine tiles must coincide"
    grid_spec = pltpu.PrefetchScalarGridSpec(
        num_scalar_prefetch=3,
        grid=(nsteps,),
        in_specs=[pl.BlockSpec((1, 1, 2 * tm), lambda i, la, ha, nv: (i, 0, 0), memory_space=pltpu.SMEM),
                  pl.BlockSpec(memory_space=pl.ANY)],
        out_specs=pl.BlockSpec(memory_space=pl.ANY),
        scratch_shapes=[pltpu.VMEM((MOE_ROWS, d), F32), pltpu.VMEM((3, tm, d), F32),
                        pltpu.SemaphoreType.DMA(()), pltpu.SemaphoreType.DMA(()),
                        pltpu.SemaphoreType.DMA((3,)), pltpu.SemaphoreType.DMA((3,))],
    )
    return pl.pallas_call(
        functools.partial(_dispatch_kernel, tm=tm, nsteps=nsteps, rows=MOE_ROWS,
                          nblocks=total_rows // MOE_ROWS),
        grid_spec=grid_spec,
        out_shape=jax.ShapeDtypeStruct((total_rows, d), F32),
        compiler_params=pltpu.CompilerParams(dimension_semantics=("arbitrary",), has_side_effects=True),
        name="moe_dispatch",
    )(last_start, has_rows, nvalid, dst3, x2)


def _experts_kernel(be_ref, nvalid_ref, xd_ref, wg_ref, wu_ref, wd_ref, yd_ref, wgu_bf, wd_bf):
    i = pl.program_id(0)
    changed = jnp.logical_or(i == 0, be_ref[i] != be_ref[jnp.maximum(i - 1, 0)])

    @pl.when(changed)
    def _():
        wgu_bf[:, :D_EXPERT] = wg_ref[...].astype(BF16)
        wgu_bf[:, D_EXPERT:] = wu_ref[...].astype(BF16)
        wd_bf[...] = wd_ref[...].astype(BF16)

    @pl.when(i < nvalid_ref[0])
    def _():
        gu = _dot(xd_ref[...].astype(BF16), wgu_bf[...])
        gate = gu[:, :D_EXPERT]
        hidden = (gate * jax.nn.sigmoid(gate) * gu[:, D_EXPERT:]).astype(BF16)
        yd_ref[...] = _dot(hidden, wd_bf[...])

    @pl.when(i >= nvalid_ref[0])
    def _():
        yd_ref[...] = jnp.zeros_like(yd_ref)


def _moe_experts(xd, block_e, nvalid, w_gate, w_up, w_down):
    total, d = xd.shape
    rows = MOE_ROWS
    nblocks = total // rows
    grid_spec = pltpu.PrefetchScalarGridSpec(
        num_scalar_prefetch=2,
        grid=(nblocks,),
        in_specs=[pl.BlockSpec((rows, d), lambda i, be, nv: (jnp.minimum(i, nv[0] - 1), 0)),
                  pl.BlockSpec((None, d, D_EXPERT), lambda i, be, nv: (be[i], 0, 0)),
                  pl.BlockSpec((None, d, D_EXPERT), lambda i, be, nv: (be[i], 0, 0)),
                  pl.BlockSpec((None, D_EXPERT, d), lambda i, be, nv: (be[i], 0, 0))],
        out_specs=pl.BlockSpec((rows, d), lambda i, be, nv: (i, 0)),
        scratch_shapes=[pltpu.VMEM((d, 2 * D_EXPERT), BF16), pltpu.VMEM((D_EXPERT, d), BF16)],
    )
    return pl.pallas_call(
        _experts_kernel,
        grid_spec=grid_spec,
        out_shape=jax.ShapeDtypeStruct((total, d), F32),
        compiler_params=pltpu.CompilerParams(dimension_semantics=("arbitrary",),
                                             vmem_limit_bytes=VMEM_LIMIT),
        name="moe_experts",
    )(block_e, nvalid, xd, w_gate, w_up, w_down)


def _combine_kernel(dst_cur_ref, dst_nxt_ref, dst_nx2_ref, x2_ref, route_ref, g_ref, b_ref, yd_hbm,
                    out_ref, ybuf, sems, *, tm, nsteps, alpha):
    i = pl.program_id(0)

    def issue(idx_ref, s):
        for r in range(2 * tm):
            pltpu.make_async_copy(yd_hbm.at[pl.ds(idx_ref[0, 0, r], 1)], ybuf.at[s, pl.ds(r, 1)],
                                  sems.at[s]).start(priority=r % 2)

    def wait(s):
        pltpu.make_async_copy(yd_hbm.at[pl.ds(0, 2 * tm)], ybuf.at[s], sems.at[s]).wait()

    @pl.when(i == 0)
    def _():
        issue(dst_cur_ref, 0)
        issue(dst_nxt_ref, 1)

    for s in range(3):
        @pl.when(i % 3 == s)
        def _():
            wait(s)
            issue(dst_nx2_ref, (s + 2) % 3)
            route = route_ref[...]
            y = ybuf[s, :tm, :] * route[:, 2:3] + ybuf[s, tm:, :] * route[:, 3:4]
            out_ref[...] = _layer_norm(alpha * x2_ref[...] + y, g_ref[...], b_ref[...])

            @pl.when(i == nsteps - 1)
            def _():
                wait((s + 1) % 3)
                wait((s + 2) % 3)


def _moe_combine(x2, route, dst3, yd, ln_g, ln_b, alpha):
    t, d = x2.shape
    tm = TM_OUT
    nsteps = t // tm
    smem_blk = lambda f: pl.BlockSpec((1, 1, 2 * tm), f, memory_space=pltpu.SMEM)
    row = lambda n: pl.BlockSpec((tm, n), lambda i: (i, 0))
    return pl.pallas_call(
        functools.partial(_combine_kernel, tm=tm, nsteps=nsteps, alpha=alpha),
        grid=(nsteps,),
        out_shape=jax.ShapeDtypeStruct((t, d), F32),
        in_specs=[smem_blk(lambda i: (i, 0, 0)),
                  smem_blk(lambda i: (jnp.minimum(i + 1, nsteps - 1), 0, 0)),
                  smem_blk(lambda i: (jnp.minimum(i + 2, nsteps - 1), 0, 0)),
                  row(d), row(LANES), _const_spec(ln_g.shape), _const_spec(ln_b.shape),
                  pl.BlockSpec(memory_space=pl.ANY)],
        out_specs=row(d),
        scratch_shapes=[pltpu.VMEM((3, 2 * tm, d), F32), pltpu.SemaphoreType.DMA((3,))],
        compiler_params=pltpu.CompilerParams(dimension_semantics=("arbitrary",),
                                             vmem_limit_bytes=VMEM_LIMIT),
        name="moe_combine",
    )(dst3, dst3, dst3, x2, route, ln_g, ln_b, yd)


def _prep_layer(w_in, b_in, gm_ln_g, gm_ln_b, gm_w_s, gm_b_s, w_gm_out, mla_q_norm_g, mla_kv_norm_g,
                w_uq, w_uk, w_uv, w_mla_out, w_o, ln1_g, ln1_b, w_mq, w_mk, w_mv, w_mo, ln2_g, ln2_b,
                w_group_router, b_group_router, w_expert_router, b_expert_router, depth):
    d = w_in.shape[0]
    s_v = 2 * GM_WIDTH
    s_q = s_v + MLA_Q_LORA
    s_kv = s_q + MLA_KV_LORA
    s_r = s_kv + MLA_ROPE
    rowv = lambda a: a.reshape(1, -1).astype(F32)

    def lat_cols(a):
        z = jnp.zeros(a.shape[:-1] + (MLA_NOPE,), a.dtype)
        kr = a[..., s_kv:s_r]
        return jnp.concatenate([a[..., s_v:s_kv], z, kr, kr], axis=-1)

    wq3 = w_uq.reshape(MLA_Q_LORA, MLA_HEADS, MLA_NOPE + MLA_ROPE)
    wq_pad = jnp.concatenate([wq3, wq3[..., MLA_NOPE:]], axis=-1).reshape(MLA_Q_LORA, MLA_HEADS * HEAD_PAD)
    wk3 = w_uk.reshape(MLA_KV_LORA, MLA_HEADS, MLA_NOPE)
    wk_pad = jnp.pad(wk3, ((0, 0), (0, 0), (0, HEAD_PAD - MLA_NOPE))).reshape(MLA_KV_LORA, MLA_HEADS * HEAD_PAD)
    wv3 = w_uv.T.reshape(MLA_HEADS, MLA_V, MLA_KV_LORA)
    wv_pad = jnp.pad(wv3, ((0, 0), (0, V_ROWS - MLA_V), (0, 0))).reshape(MLA_HEADS * V_ROWS, MLA_KV_LORA)
    b_v = jnp.zeros((MLA_HEADS, V_ROWS, 1), F32).at[:, MLA_V].set(1.0).reshape(MLA_HEADS * V_ROWS, 1)
    w_r = jnp.zeros((d, LANES), F32)
    w_r = w_r.at[:, :N_EXPERTS].set(w_expert_router).at[:, N_EXPERTS:N_EXPERTS + N_GROUPS].set(w_group_router)
    b_r = jnp.zeros((LANES,), F32)
    b_r = b_r.at[:N_EXPERTS].set(b_expert_router).at[N_EXPERTS:N_EXPERTS + N_GROUPS].set(b_group_router)
    wr_hi = w_r.astype(BF16)
    return dict(
        w_uv=w_in[:, :s_v].astype(BF16), b_uv=rowv(b_in[:s_v]),
        w_gate=w_in[:, s_r:].astype(BF16), b_gate=rowv(b_in[s_r:]),
        w_lat=lat_cols(w_in).astype(BF16), b_lat=rowv(lat_cols(b_in)),
        gm_ln_g=rowv(gm_ln_g), gm_ln_b=rowv(gm_ln_b),
        ws=gm_w_s.astype(BF16), bs_full=jnp.repeat(gm_b_s.T, GM_CHUNK, axis=1).astype(F32),
        w_gm_out=w_gm_out.astype(BF16), qn_g=rowv(mla_q_norm_g), kvn_g=rowv(mla_kv_norm_g),
        w_uq=wq_pad.astype(BF16), w_uk=wk_pad.astype(BF16), w_uvv=wv_pad.astype(BF16), b_v=b_v,
        w_mla_out=w_mla_out.astype(BF16), w_o=w_o.astype(BF16), ln1_g=rowv(ln1_g), ln1_b=rowv(ln1_b),
        w_mq=w_mq.astype(BF16), w_mk=w_mk.astype(BF16), w_mv=w_mv.astype(BF16), w_mo=w_mo.astype(BF16),
        ln2_g=rowv(ln2_g), ln2_b=rowv(ln2_b),
        wr_hi=wr_hi, wr_lo=(w_r - wr_hi.astype(F32)).astype(BF16), br=rowv(b_r),
        alpha=(2 * depth) ** 0.25,
    )


def _dispatch_tables(idx, counts, t):
    rows = MOE_ROWS
    tiles, _, tm = idx.shape
    e = idx[:, 0:2, :].astype(jnp.int32)
    rank = idx[:, 4:6, :].astype(jnp.int32)
    cnt = counts[0, :N_EXPERTS].astype(jnp.int32)
    padded = (cnt + rows - 1) // rows * rows
    ends = jnp.cumsum(padded)
    starts = ends - padded
    ids = jnp.arange(N_EXPERTS, dtype=jnp.int32)[:, None, None, None]
    start_of = jnp.sum(jnp.where(e[None] == ids, starts[:, None, None, None], 0), axis=0)
    group = TM_OUT // tm
    dst3 = (start_of + rank).reshape(tiles // group, group, 2, tm).transpose(0, 2, 1, 3)
    dst3 = dst3.reshape(tiles // group, 1, 2 * TM_OUT)
    total = 2 * t + N_EXPERTS * rows
    block_start = jnp.arange(total // rows, dtype=jnp.int32) * rows
    block_e = jnp.minimum(jnp.sum(ends[None, :] <= block_start[:, None], axis=1), N_EXPERTS - 1)
    nvalid = (ends[-1] // rows).reshape(1).astype(jnp.int32)
    last_start = jnp.maximum(ends - rows, 0).astype(jnp.int32)
    has_rows = (cnt > 0).astype(jnp.int32)
    return dst3, block_e.astype(jnp.int32), nvalid, last_start, has_rows, total


def _layer(x, mem, positions, depth, w_in, b_in, gm_ln_g, gm_ln_b, gm_w_s, gm_b_s, w_gm_out,
           mla_q_norm_g, mla_kv_norm_g, w_uq, w_uk, w_uv, w_mla_out, w_o, ln1_g, ln1_b,
           w_mq, w_mk, w_mv, w_mo, ln2_g, ln2_b,
           w_group_router, b_group_router, w_expert_router, b_expert_router,
           w_exp_gate, w_exp_up, w_exp_down, ln3_g, ln3_b):
    b, s, d = x.shape
    t = b * s
    mem_len = mem.shape[1]
    p = _prep_layer(w_in, b_in, gm_ln_g, gm_ln_b, gm_w_s, gm_b_s, w_gm_out, mla_q_norm_g, mla_kv_norm_g,
                    w_uq, w_uk, w_uv, w_mla_out, w_o, ln1_g, ln1_b, w_mq, w_mk, w_mv, w_mo, ln2_g, ln2_b,
                    w_group_router, b_group_router, w_expert_router, b_expert_router, depth)
    x2d = x.reshape(t, d)
    inv_freq = ROPE_THETA ** (-jnp.arange(ROPE_HALF, dtype=F32) / ROPE_HALF)
    cos, sin = _rope_tables(positions, inv_freq)
    gm, sg, q, k, v = _mixer_in(x2d, cos, sin, p)
    o = _mla_attn(q, k, v, b, s)
    mk, mv = _mem_kv(mem.reshape(b * mem_len, d), p["w_mk"], p["w_mv"], mem_len)
    x2, route, idx, counts = _post_mixer(x2d, o, gm, sg, mk, mv, p, s, mem_len)
    dst3, block_e, nvalid, last_start, has_rows, total = _dispatch_tables(idx, counts, t)
    xd = _moe_dispatch(x2, dst3, last_start, has_rows, nvalid, total)
    yd = _moe_experts(xd, block_e, nvalid, w_exp_gate, w_exp_up, w_exp_down)
    out = _moe_combine(x2, route, dst3, yd, ln3_g.reshape(1, d), ln3_b.reshape(1, d), p["alpha"])
    return out.reshape(b, s, d)


def kernel(x, mem, positions, w_in, b_in, gm_ln_g, gm_ln_b, gm_w_s, gm_b_s, w_gm_out, mla_q_norm_g, mla_kv_norm_g, w_uq, w_uk, w_uv, w_mla_out, w_o, ln1_g, ln1_b, w_mq, w_mk, w_mv, w_mo, ln2_g, ln2_b, w_group_router, b_group_router, w_expert_router, b_expert_router, w_exp_gate, w_exp_up, w_exp_down, ln3_g, ln3_b):
    depth = w_in.shape[0]
    per_layer = (w_in, b_in, gm_ln_g, gm_ln_b, gm_w_s, gm_b_s, w_gm_out, mla_q_norm_g, mla_kv_norm_g,
                 w_uq, w_uk, w_uv, w_mla_out, w_o, ln1_g, ln1_b, w_mq, w_mk, w_mv, w_mo, ln2_g, ln2_b,
                 w_group_router, b_group_router, w_expert_router, b_expert_router,
                 w_exp_gate, w_exp_up, w_exp_down, ln3_g, ln3_b)
    h = x
    for l in range(depth):
        h = _layer(h, mem, positions, depth, *[w[l] for w in per_layer])
    return h
```
